```python
import jax, jax.numpy as jnp
from jax import lax
import numpy as np

D_MODEL = 1024
BATCH = 1
SEQ = 16384
DEPTH = 1

D_MIX = D_MODEL
GLA_HEADS = 4
GLA_DV = 128
GLA_DK = 64
GLA_KEY = GLA_HEADS * GLA_DK
GLA_VAL = GLA_HEADS * GLA_DV
GATE_RANK = 16
GATE_TAU = 16.0
CHUNK = 64
CONV_CH = D_MIX - GLA_VAL
CONV_GROUPS = 8
CONV_W = 31
D_FF = 4 * D_MODEL
EPS = 1e-6

OFF_Q = 0
OFF_K = OFF_Q + GLA_KEY
OFF_V = OFF_K + GLA_KEY
OFF_G = OFF_V + GLA_VAL
OFF_A = OFF_G + GLA_VAL
OFF_C = OFF_A + GATE_RANK
D_IN = OFF_C + 2 * CONV_CH

kernel_name = "hybrid_gla_conformer_conv_block"


def rmsnorm(x, g):
    xf = x.astype(jnp.float32)
    y = xf * lax.rsqrt(jnp.mean(xf * xf, axis=-1, keepdims=True) + EPS)
    return (y * g.astype(jnp.float32)).astype(x.dtype)


def gla_chunked(q, k, v, log_a):
    B, S, H, DK = q.shape
    DV = v.shape[-1]
    nc = S // CHUNK

    def to_chunks(t):
        return t.astype(jnp.float32).reshape(B, nc, CHUNK, H, t.shape[-1]).transpose(1, 0, 3, 2, 4)

    qc, kc, vc, ac = map(to_chunks, (q, k, v, log_a))
    bc = jnp.cumsum(ac, axis=3)
    causal = jnp.tril(jnp.ones((CHUNK, CHUNK), dtype=bool))

    def step(state, inp):
        qi, ki, vi, bi = inp
        o_inter = jnp.einsum('bhcd,bhde->bhce', qi * jnp.exp(bi), state)
        diff = bi[:, :, :, None, :] - bi[:, :, None, :, :]
        decay = jnp.exp(jnp.where(causal[:, :, None], diff, -jnp.inf))
        scores = jnp.einsum('bhid,bhjd,bhijd->bhij', qi, ki, decay)
        o_intra = jnp.einsum('bhij,bhje->bhie', scores, vi)
        b_last = bi[:, :, -1, :]
        k_dec = ki * jnp.exp(b_last[:, :, None, :] - bi)
        new_state = jnp.exp(b_last)[..., None] * state + jnp.einsum('bhcd,bhce->bhde', k_dec, vi)
        return new_state, o_inter + o_intra

    s0 = jnp.zeros((B, H, DK, DV), jnp.float32)
    _, o = lax.scan(step, s0, (qc, kc, vc, bc))
    return o.transpose(1, 0, 3, 2, 4).reshape(B, S, H, DV).astype(v.dtype)


def causal_depthwise_conv(u, w, b):
    u_pad = jnp.pad(u, ((0, 0), (CONV_W - 1, 0), (0, 0)))
    y = lax.conv_general_dilated(
        u_pad, w[:, None, :].astype(u.dtype), window_strides=(1,), padding='VALID',
        dimension_numbers=('NWC', 'WIO', 'NWC'), feature_group_count=u.shape[-1])
    return y + b.astype(u.dtype)


def channel_group_norm(u, g, b):
    B, S, C = u.shape
    uf = u.astype(jnp.float32).reshape(B, S, CONV_GROUPS, C // CONV_GROUPS)
    mu = jnp.mean(uf, axis=-1, keepdims=True)
    var = jnp.mean(jnp.square(uf - mu), axis=-1, keepdims=True)
    y = ((uf - mu) * lax.rsqrt(var + EPS)).reshape(B, S, C)
    return (y * g.astype(jnp.float32) + b.astype(jnp.float32)).astype(u.dtype)


def setup_inputs(seed: int = 0) -> dict:
    key = jax.random.key(seed)
    ks = jax.random.split(key, 16)
    f32 = jnp.float32
    nrm = lambda k, shape, s: jax.random.normal(k, shape, f32) * s
    return {
        "x": jax.random.normal(ks[0], (BATCH, SEQ, D_MODEL), f32),
        "norm1_g": 1.0 + nrm(ks[1], (DEPTH, D_MODEL), 0.02),
        "w_in": nrm(ks[2], (DEPTH, D_MODEL, D_IN), D_MODEL ** -0.5),
        "w_gate_up": nrm(ks[3], (DEPTH, GATE_RANK, GLA_KEY), GATE_RANK ** -0.5),
        "b_gate": nrm(ks[4], (DEPTH, GLA_KEY), 0.1),
        "gla_norm_g": 1.0 + nrm(ks[5], (DEPTH, GLA_DV), 0.02),
        "conv_w": nrm(ks[6], (DEPTH, CONV_W, CONV_CH), CONV_W ** -0.5),
        "conv_b": nrm(ks[7], (DEPTH, CONV_CH), 0.02),
        "conv_norm_g": 1.0 + nrm(ks[8], (DEPTH, CONV_CH), 0.02),
        "conv_norm_b": nrm(ks[9], (DEPTH, CONV_CH), 0.02),
        "w_out": nrm(ks[10], (DEPTH, D_MIX, D_MODEL), D_MIX ** -0.5),
        "norm2_g": 1.0 + nrm(ks[11], (DEPTH, D_MODEL), 0.02),
        "w_mlp_in": nrm(ks[12], (DEPTH, D_MODEL, D_FF), D_MODEL ** -0.5),
        "w_mlp_out": nrm(ks[13], (DEPTH, D_FF, D_MODEL), D_FF ** -0.5),
        "final_norm_g": 1.0 + nrm(ks[14], (D_MODEL,), 0.02),
    }


def reference(x, norm1_g, w_in, w_gate_up, b_gate, gla_norm_g, conv_w, conv_b,
              conv_norm_g, conv_norm_b, w_out, norm2_g, w_mlp_in, w_mlp_out,
              final_norm_g):
    B, S, _ = x.shape
    h = x
    for l in range(DEPTH):
        xn = rmsnorm(h, norm1_g[l])
        proj = jnp.einsum('bsd,de->bse', xn, w_in[l])

        q = proj[..., OFF_Q:OFF_K].reshape(B, S, GLA_HEADS, GLA_DK) * (GLA_DK ** -0.5)
        k = proj[..., OFF_K:OFF_V].reshape(B, S, GLA_HEADS, GLA_DK)
        v = proj[..., OFF_V:OFF_G].reshape(B, S, GLA_HEADS, GLA_DV)
        g = proj[..., OFF_G:OFF_A]
        z = proj[..., OFF_A:OFF_C]
        a_logit = jnp.einsum('bsr,rk->bsk', z, w_gate_up[l]) + b_gate[l]
        log_a = (jax.nn.log_sigmoid(a_logit.astype(jnp.float32)) / GATE_TAU).reshape(B, S, GLA_HEADS, GLA_DK)
        o_gla = gla_chunked(q, k, v, log_a)
        o_gla = rmsnorm(o_gla, gla_norm_g[l]).reshape(B, S, GLA_VAL) * jax.nn.silu(g)

        c_in = proj[..., OFF_C:OFF_C + CONV_CH]
        c_gate = proj[..., OFF_C + CONV_CH:D_IN]
        u = c_in * jax.nn.sigmoid(c_gate)
        u = causal_depthwise_conv(u, conv_w[l], conv_b[l])
        o_conv = jax.nn.silu(channel_group_norm(u, conv_norm_g[l], conv_norm_b[l]))

        mixed = jnp.concatenate([o_gla, o_conv], axis=-1)
        h = h + jnp.einsum('bse,ed->bsd', mixed, w_out[l])

        hn = rmsnorm(h, norm2_g[l])
        ff = jnp.square(jax.nn.relu(jnp.einsum('bsd,df->bsf', hn, w_mlp_in[l])))
        h = h + jnp.einsum('bsf,fd->bsd', ff, w_mlp_out[l])
    return rmsnorm(h, final_norm_g)
```

```python
import functools

import jax
import jax.numpy as jnp
from jax import lax
from jax.experimental import pallas as pl
from jax.experimental.pallas import tpu as pltpu

D_MODEL = 1024
GLA_HEADS = 4
GLA_DV = 128
GLA_DK = 64
GLA_KEY = GLA_HEADS * GLA_DK
GLA_VAL = GLA_HEADS * GLA_DV
GATE_RANK = 16
GATE_TAU = 16.0
CONV_CH = 512
CONV_GROUPS = 8
CONV_W = 31
D_FF = 4 * D_MODEL
EPS = 1e-6

OFF_Q = 0
OFF_K = OFF_Q + GLA_KEY
OFF_V = OFF_K + GLA_KEY
OFF_G = OFF_V + GLA_VAL
OFF_A = OFF_G + GLA_VAL
OFF_C = OFF_A + GATE_RANK
D_IN = OFF_C + 2 * CONV_CH

LANES = 128
Z_PAD = LANES
P_Q = 0
P_K = P_Q + GLA_KEY
P_V = P_K + GLA_KEY
P_G = P_V + GLA_VAL
P_CI = P_G + GLA_VAL
P_CG = P_CI + CONV_CH
P_Z = P_CG + CONV_CH
D_INP = P_Z + Z_PAD

TM_PROJ = 512
TM_GLA = 256
TM_CONV = 512
TM_MLP = 512
CONV_HALO = 32
CONV_RB = 32
SUBLANES = 8
CONV_SH_ROWS = TM_CONV + CONV_HALO - SUBLANES
GLA_FAST_MIN_B = -60.0

VMEM_LIMIT = 56 * 1024 * 1024

F32 = jnp.float32
BF16 = jnp.bfloat16


def _sigmoid(x):
    return 1.0 / (1.0 + jnp.exp(-x))


def _split_bf16(x):
    hi = x.astype(BF16)
    lo = (x - hi.astype(F32)).astype(BF16)
    return hi, lo


def _dot(a, b):
    return jnp.dot(a, b, preferred_element_type=F32)


def _dot_nt(a, b):
    return lax.dot_general(a, b, (((1,), (1,)), ((), ())), preferred_element_type=F32)


def _dot_tn(a, b):
    return lax.dot_general(a, b, (((0,), (0,)), ((), ())), preferred_element_type=F32)


def _proj_kernel(x_ref, g1_ref, w_ref, wg_ref, bg_ref,
                 q_ref, k_ref, la_ref, v_ref, sg_ref, u_ref):
    x = x_ref[...]
    ms = jnp.mean(x * x, axis=-1, keepdims=True)
    xn = (x * lax.rsqrt(ms + EPS) * g1_ref[...]).astype(BF16)
    proj = _dot(xn, w_ref[...])
    q_ref[...] = proj[:, P_Q:P_K] * (GLA_DK ** -0.5)
    k_ref[...] = proj[:, P_K:P_V]
    v_ref[...] = proj[:, P_V:P_G].astype(BF16)
    g = proj[:, P_G:P_CI]
    sg_ref[...] = (g * _sigmoid(g)).astype(BF16)
    u_ref[...] = proj[:, P_CI:P_CG] * _sigmoid(proj[:, P_CG:P_Z])
    z = proj[:, P_Z:D_INP].astype(BF16)
    a_logit = _dot(z, wg_ref[...]) + bg_ref[...]
    la = jnp.minimum(a_logit, 0.0) - jnp.log1p(jnp.exp(-jnp.abs(a_logit)))
    la_ref[...] = la * (1.0 / GATE_TAU)


def _proj_call(x2, g1, w_p, wg_p, bg):
    s = x2.shape[0]
    row = lambda i: (i, 0)
    const = lambda i: (0, 0)
    return pl.pallas_call(
        _proj_kernel,
        grid=(s // TM_PROJ,),
        in_specs=[
            pl.BlockSpec((TM_PROJ, D_MODEL), row),
            pl.BlockSpec((1, D_MODEL), const),
            pl.BlockSpec((D_MODEL, D_INP), const, pipeline_mode=pl.Buffered(1)),
            pl.BlockSpec((Z_PAD, GLA_KEY), const),
            pl.BlockSpec((1, GLA_KEY), const),
        ],
        out_specs=[
            pl.BlockSpec((TM_PROJ, GLA_KEY), row),
            pl.BlockSpec((TM_PROJ, GLA_KEY), row),
            pl.BlockSpec((TM_PROJ, GLA_KEY), row),
            pl.BlockSpec((TM_PROJ, GLA_VAL), row),
            pl.BlockSpec((TM_PROJ, GLA_VAL), row),
            pl.BlockSpec((TM_PROJ, CONV_CH), row),
        ],
        out_shape=[
            jax.ShapeDtypeStruct((s, GLA_KEY), F32),
            jax.ShapeDtypeStruct((s, GLA_KEY), F32),
            jax.ShapeDtypeStruct((s, GLA_KEY), F32),
            jax.ShapeDtypeStruct((s, GLA_VAL), BF16),
            jax.ShapeDtypeStruct((s, GLA_VAL), BF16),
            jax.ShapeDtypeStruct((s, CONV_CH), F32),
        ],
        compiler_params=pltpu.CompilerParams(
            dimension_semantics=("parallel",), vmem_limit_bytes=VMEM_LIMIT),
        name="proj",
    )(x2, g1, w_p, wg_p, bg)


def _gla_finish(o_h, h, gn_ref, sg_ref, o_ref):
    ms = jnp.mean(o_h * o_h, axis=-1, keepdims=True)
    y = o_h * lax.rsqrt(ms + EPS) * gn_ref[...]
    cols = slice(h * GLA_DV, (h + 1) * GLA_DV)
    o_ref[:, cols] = (y * sg_ref[:, cols].astype(F32)).astype(BF16)


def _gla_kernel(q_ref, k_ref, la_ref, v_ref, sg_ref, gn_ref, o_ref, st_ref):
    c = TM_GLA

    @pl.when(pl.program_id(0) == 0)
    def _():
        st_ref[...] = jnp.zeros_like(st_ref)

    ri = lax.broadcasted_iota(jnp.int32, (c, c), 0)
    ci = lax.broadcasted_iota(jnp.int32, (c, c), 1)
    la_hi, la_lo = _split_bf16(la_ref[...])

    def masked_sum(m):
        mb = m.astype(BF16)
        return _dot(mb, la_hi) + _dot(mb, la_lo)

    b = masked_sum(ci <= ri)
    b_last = b[c - 1:c, :]
    q = q_ref[...]
    k = k_ref[...]
    q_in = (q * jnp.exp(b)).astype(BF16)
    k_out = (k * jnp.exp(b_last - b)).astype(BF16)
    a_tile = jnp.exp(b_last)
    fast = jnp.min(b_last) > GLA_FAST_MIN_B

    def finish(scores, h):
        dk = slice(h * GLA_DK, (h + 1) * GLA_DK)
        dv = slice(h * GLA_DV, (h + 1) * GLA_DV)
        v_h = v_ref[:, dv]
        st = st_ref[h]
        o_h = _dot(scores.astype(BF16), v_h) + _dot_nt(q_in[:, dk], st.astype(BF16))
        _gla_finish(o_h, h, gn_ref, sg_ref, o_ref)
        st_ref[h] = a_tile[:, dk] * st + _dot_tn(v_h, k_out[:, dk])

    @pl.when(fast)
    def _():
        k_div = (k * jnp.exp(-b)).astype(BF16)
        for h in range(GLA_HEADS):
            dk = slice(h * GLA_DK, (h + 1) * GLA_DK)
            s_h = _dot_nt(q_in[:, dk], k_div[:, dk])
            finish(jnp.where(ri >= ci, s_h, 0.0), h)

    @pl.when(jnp.logical_not(fast))
    def _():
        kb = k.astype(BF16)
        qb = q.astype(BF16)
        scores = []
        for h in range(GLA_HEADS):
            dk = slice(h * GLA_DK, (h + 1) * GLA_DK)
            scores.append(jnp.where(ri == ci, _dot_nt(qb[:, dk], kb[:, dk]), 0.0))
        half = c // 2
        while half >= 1:
            ref = (ri & (-2 * half)) + (half - 1)
            lo = jnp.minimum(ri, ref)
            hi = jnp.maximum(ri, ref)
            e = jnp.exp(masked_sum((ci > lo) & (ci <= hi)))
            q_l = (q * e).astype(BF16)
            k_l = (k * e).astype(BF16)
            pair = (((ri ^ ci) & (-2 * half)) == 0) & ((ri & half) != 0) & ((ci & half) == 0)
            for h in range(GLA_HEADS):
                dk = slice(h * GLA_DK, (h + 1) * GLA_DK)
                scores[h] = jnp.where(pair, _dot_nt(q_l[:, dk], k_l[:, dk]), scores[h])
            half //= 2
        for h in range(GLA_HEADS):
            finish(scores[h], h)


def _gla_call(q, k, la, v, sg, gn):
    s = q.shape[0]
    row = lambda i: (i, 0)
    const = lambda i: (0, 0)
    return pl.pallas_call(
        _gla_kernel,
        grid=(s // TM_GLA,),
        in_specs=[
            pl.BlockSpec((TM_GLA, GLA_KEY), row),
            pl.BlockSpec((TM_GLA, GLA_KEY), row),
            pl.BlockSpec((TM_GLA, GLA_KEY), row),
            pl.BlockSpec((TM_GLA, GLA_VAL), row),
            pl.BlockSpec((TM_GLA, GLA_VAL), row),
            pl.BlockSpec((1, GLA_DV), const),
        ],
        out_specs=pl.BlockSpec((TM_GLA, GLA_VAL), row),
        out_shape=jax.ShapeDtypeStruct((s, GLA_VAL), BF16),
        scratch_shapes=[pltpu.VMEM((GLA_HEADS, GLA_DV, GLA_DK), F32)],
        compiler_params=pltpu.CompilerParams(
            dimension_semantics=("arbitrary",), vmem_limit_bytes=VMEM_LIMIT),
        name="gla",
    )(q, k, la, v, sg, gn)


def _conv_kernel(u_ref, w_ref, cb_ref, gg_ref, gb_ref, gm_ref, o_ref, ext_ref, sh_ref, y_ref):
    tm = TM_CONV

    @pl.when(pl.program_id(0) == 0)
    def _():
        ext_ref[0:CONV_HALO, :] = jnp.zeros((CONV_HALO, CONV_CH), F32)

    @pl.when(pl.program_id(0) > 0)
    def _():
        ext_ref[0:CONV_HALO, :] = ext_ref[tm:tm + CONV_HALO, :]

    ext_ref[CONV_HALO:, :] = u_ref[...]
    for p in range(1, SUBLANES):
        sh_ref[p - 1] = ext_ref[p:p + CONV_SH_ROWS, :]
    first = CONV_HALO - (CONV_W - 1)

    def row_block(r, carry):
        base = pl.multiple_of(r * CONV_RB, CONV_RB)
        for cg in range(CONV_CH // LANES):
            cols = slice(cg * LANES, (cg + 1) * LANES)
            acc = jnp.broadcast_to(cb_ref[:, cols], (CONV_RB, LANES))
            for t in range(CONV_W):
                shift = first + t
                p = shift % SUBLANES
                rows = pl.ds(base + (shift - p), CONV_RB)
                tap = ext_ref[rows, cols] if p == 0 else sh_ref[p - 1, rows, cols]
                acc = acc + w_ref[t:t + 1, cols] * tap
            y_ref[pl.ds(base, CONV_RB), cols] = acc
        return carry

    lax.fori_loop(0, tm // CONV_RB, row_block, 0)

    y = y_ref[...]
    gm = gm_ref[...]
    y_hi, y_lo = _split_bf16(y)
    mu = _dot(y_hi, gm) + _dot(y_lo, gm)
    d = y - mu
    d2_hi, d2_lo = _split_bf16(d * d)
    var = _dot(d2_hi, gm) + _dot(d2_lo, gm)
    yn = d * lax.rsqrt(var + EPS) * gg_ref[...] + gb_ref[...]
    o_ref[...] = (yn * _sigmoid(yn)).astype(BF16)


def _conv_call(u, w, cb, gg, gb, gm):
    s = u.shape[0]
    row = lambda i: (i, 0)
    const = lambda i: (0, 0)
    return pl.pallas_call(
        _conv_kernel,
        grid=(s // TM_CONV,),
        in_specs=[
            pl.BlockSpec((TM_CONV, CONV_CH), row),
            pl.BlockSpec((CONV_W, CONV_CH), const),
            pl.BlockSpec((1, CONV_CH), const),
            pl.BlockSpec((1, CONV_CH), const),
            pl.BlockSpec((1, CONV_CH), const),
            pl.BlockSpec((CONV_CH, CONV_CH), const),
        ],
        out_specs=pl.BlockSpec((TM_CONV, CONV_CH), row),
        out_shape=jax.ShapeDtypeStruct((s, CONV_CH), BF16),
        scratch_shapes=[
            pltpu.VMEM((TM_CONV + CONV_HALO, CONV_CH), F32),
            pltpu.VMEM((SUBLANES - 1, CONV_SH_ROWS, CONV_CH), F32),
            pltpu.VMEM((TM_CONV, CONV_CH), F32),
        ],
        compiler_params=pltpu.CompilerParams(
            dimension_semantics=("arbitrary",), vmem_limit_bytes=VMEM_LIMIT),
        name="conv",
    )(u, w, cb, gg, gb, gm)


def _mlp_kernel(x_ref, og_ref, oc_ref, wo_ref, g2_ref, w1_ref, w2_ref, gf_ref, o_ref):
    h = x_ref[...] + _dot(og_ref[...], wo_ref[0:GLA_VAL, :]) + _dot(oc_ref[...], wo_ref[GLA_VAL:, :])
    ms = jnp.mean(h * h, axis=-1, keepdims=True)
    hn = (h * lax.rsqrt(ms + EPS) * g2_ref[...]).astype(BF16)
    f = jnp.maximum(_dot(hn, w1_ref[...]), 0.0)
    acc = h + _dot((f * f).astype(BF16), w2_ref[...])
    ms2 = jnp.mean(acc * acc, axis=-1, keepdims=True)
    o_ref[...] = acc * lax.rsqrt(ms2 + EPS) * gf_ref[...]


def _mlp_call(x2, og, oc, wo, g2, w1, w2, gf):
    s = x2.shape[0]
    row = lambda i: (i, 0)
    const = lambda i: (0, 0)
    single = pl.Buffered(1)
    return pl.pallas_call(
        _mlp_kernel,
        grid=(s // TM_MLP,),
        in_specs=[
            pl.BlockSpec((TM_MLP, D_MODEL), row),
            pl.BlockSpec((TM_MLP, GLA_VAL), row),
            pl.BlockSpec((TM_MLP, CONV_CH), row),
            pl.BlockSpec((D_MODEL, D_MODEL), const, pipeline_mode=single),
            pl.BlockSpec((1, D_MODEL), const),
            pl.BlockSpec((D_MODEL, D_FF), const, pipeline_mode=single),
            pl.BlockSpec((D_FF, D_MODEL), const, pipeline_mode=single),
            pl.BlockSpec((1, D_MODEL), const),
        ],
        out_specs=pl.BlockSpec((TM_MLP, D_MODEL), row),
        out_shape=jax.ShapeDtypeStruct((s, D_MODEL), F32),
        compiler_params=pltpu.CompilerParams(
            dimension_semantics=("parallel",), vmem_limit_bytes=VMEM_LIMIT),
        name="mlp",
    )(x2, og, oc, wo, g2, w1, w2, gf)


def kernel(x, norm1_g, w_in, w_gate_up, b_gate, gla_norm_g, conv_w, conv_b, conv_norm_g,
           conv_norm_b, w_out, norm2_g, w_mlp_in, w_mlp_out, final_norm_g):
    bsz, seq, _ = x.shape
    x2 = x.reshape(bsz * seq, D_MODEL)
    assert bsz == 1, "state / halo carry across grid steps assumes one sequence"
    l = 0
    wi = w_in[l]
    w_p = jnp.concatenate(
        [wi[:, OFF_Q:OFF_A], wi[:, OFF_C:D_IN], wi[:, OFF_A:OFF_C],
         jnp.zeros((D_MODEL, Z_PAD - GATE_RANK), wi.dtype)], axis=1).astype(BF16)
    wg_p = jnp.concatenate(
        [w_gate_up[l], jnp.zeros((Z_PAD - GATE_RANK, GLA_KEY), w_gate_up.dtype)], axis=0).astype(BF16)
    q, k, la, v, sg, u = _proj_call(x2, norm1_g[l][None, :], w_p, wg_p, b_gate[l][None, :])
    o_gla = _gla_call(q, k, la, v, sg, gla_norm_g[l][None, :])
    grp = CONV_CH // CONV_GROUPS
    gidx = jnp.arange(CONV_CH) // grp
    gm = jnp.where(gidx[:, None] == gidx[None, :], 1.0 / grp, 0.0).astype(BF16)
    o_conv = _conv_call(u, conv_w[l], conv_b[l][None, :], conv_norm_g[l][None, :],
                        conv_norm_b[l][None, :], gm)
    out = _mlp_call(x2, o_gla, o_conv, w_out[l].astype(BF16), norm2_g[l][None, :],
                    w_mlp_in[l].astype(BF16), w_mlp_out[l].astype(BF16), final_norm_g[None, :])
    return out.reshape(bsz, seq, D_MODEL)
```

```python
import jax
import jax.numpy as jnp
from jax import lax
from jax.experimental import pallas as pl
from jax.experimental.pallas import tpu as pltpu

D_MODEL = 1024
GLA_HEADS = 4
GLA_DV = 128
GLA_DK = 64
GLA_KEY = GLA_HEADS * GLA_DK
GLA_VAL = GLA_HEADS * GLA_DV
GATE_RANK = 16
GATE_TAU = 16.0
CONV_CH = 512
CONV_GROUPS = 8
CONV_W = 31
D_FF = 4 * D_MODEL
EPS = 1e-6

OFF_Q = 0
OFF_K = OFF_Q + GLA_KEY
OFF_V = OFF_K + GLA_KEY
OFF_G = OFF_V + GLA_VAL
OFF_A = OFF_G + GLA_VAL
OFF_C = OFF_A + GATE_RANK
D_IN = OFF_C + 2 * CONV_CH

LANES = 128
SUBLANES = 8
Z_PAD = LANES
P_Q = 0
P_K = P_Q + GLA_KEY
P_V = P_K + GLA_KEY
P_G = P_V + GLA_VAL
P_CI = P_G + GLA_VAL
P_CG = P_CI + CONV_CH
P_Z = P_CG + CONV_CH
D_INP = P_Z + Z_PAD

TM_PROJ = 512
GLA_SUB = 256
GLA_NSUB = 4
TM_GLA = GLA_SUB * GLA_NSUB
TM_CONV = 512
TM_MLP = 512
CONV_HALO = 32
CONV_RB = 32
CONV_SH_ROWS = TM_CONV + CONV_HALO - SUBLANES
GLA_FAST_MIN_B = -60.0

VMEM_LIMIT = 56 * 1024 * 1024

F32 = jnp.float32
BF16 = jnp.bfloat16


def _sigmoid(x):
    return 1.0 / (1.0 + jnp.exp(-x))


def _split_bf16(x):
    hi = x.astype(BF16)
    lo = (x - hi.astype(F32)).astype(BF16)
    return hi, lo


def _dot(a, b):
    return jnp.dot(a, b, preferred_element_type=F32)


def _dot_nt(a, b):
    return lax.dot_general(a, b, (((1,), (1,)), ((), ())), preferred_element_type=F32)


def _dot_tn(a, b):
    return lax.dot_general(a, b, (((0,), (0,)), ((), ())), preferred_element_type=F32)


def _proj_kernel(x_ref, g1_ref, w_ref, wg_ref, bg_ref,
                 q_ref, k_ref, la_ref, v_ref, sg_ref, u_ref):
    x = x_ref[...]
    ms = jnp.mean(x * x, axis=-1, keepdims=True)
    xn = (x * lax.rsqrt(ms + EPS) * g1_ref[...]).astype(BF16)
    proj = _dot(xn, w_ref[...])
    q_ref[...] = proj[:, P_Q:P_K] * (GLA_DK ** -0.5)
    k_ref[...] = proj[:, P_K:P_V]
    v_ref[...] = proj[:, P_V:P_G].astype(BF16)
    g = proj[:, P_G:P_CI]
    sg_ref[...] = (g * _sigmoid(g)).astype(BF16)
    u_ref[...] = proj[:, P_CI:P_CG] * _sigmoid(proj[:, P_CG:P_Z])
    z = proj[:, P_Z:D_INP].astype(BF16)
    a_logit = _dot(z, wg_ref[...]) + bg_ref[...]
    la = jnp.minimum(a_logit, 0.0) - jnp.log1p(jnp.exp(-jnp.abs(a_logit)))
    la_ref[...] = la * (1.0 / GATE_TAU)


def _proj_call(x2, g1, w_p, wg_p, bg):
    s = x2.shape[0]
    row = lambda i: (i, 0)
    const = lambda i: (0, 0)
    return pl.pallas_call(
        _proj_kernel,
        grid=(s // TM_PROJ,),
        in_specs=[
            pl.BlockSpec((TM_PROJ, D_MODEL), row),
            pl.BlockSpec((1, D_MODEL), const),
            pl.BlockSpec((D_MODEL, D_INP), const, pipeline_mode=pl.Buffered(1)),
            pl.BlockSpec((Z_PAD, GLA_KEY), const),
            pl.BlockSpec((1, GLA_KEY), const),
        ],
        out_specs=[
            pl.BlockSpec((TM_PROJ, GLA_KEY), row),
            pl.BlockSpec((TM_PROJ, GLA_KEY), row),
            pl.BlockSpec((TM_PROJ, GLA_KEY), row),
            pl.BlockSpec((TM_PROJ, GLA_VAL), row),
            pl.BlockSpec((TM_PROJ, GLA_VAL), row),
            pl.BlockSpec((TM_PROJ, CONV_CH), row),
        ],
        out_shape=[
            jax.ShapeDtypeStruct((s, GLA_KEY), F32),
            jax.ShapeDtypeStruct((s, GLA_KEY), F32),
            jax.ShapeDtypeStruct((s, GLA_KEY), F32),
            jax.ShapeDtypeStruct((s, GLA_VAL), BF16),
            jax.ShapeDtypeStruct((s, GLA_VAL), BF16),
            jax.ShapeDtypeStruct((s, CONV_CH), F32),
        ],
        compiler_params=pltpu.CompilerParams(
            dimension_semantics=("parallel",), vmem_limit_bytes=VMEM_LIMIT),
        name="proj",
    )(x2, g1, w_p, wg_p, bg)


def _gla_kernel(q_ref, k_ref, la_ref, v_ref, sg_ref, gn_ref, o_ref, st_ref, b_ref):
    c = GLA_SUB

    @pl.when(pl.program_id(0) == 0)
    def _():
        st_ref[...] = jnp.zeros_like(st_ref)

    ri = lax.broadcasted_iota(jnp.int32, (c, c), 0)
    ci = lax.broadcasted_iota(jnp.int32, (c, c), 1)
    first_head = lax.broadcasted_iota(jnp.int32, (c, LANES), 1) < GLA_DK
    st_row = lax.broadcasted_iota(jnp.int32, (2 * GLA_DV, LANES), 0)
    st_lane = lax.broadcasted_iota(jnp.int32, (2 * GLA_DV, LANES), 1)
    same_head = (st_row < GLA_DV) == (st_lane < GLA_DK)

    def masked_sum(m, rows):
        mb = m.astype(BF16)
        la_hi, la_lo = _split_bf16(la_ref[rows, :])
        return _dot(mb, la_hi) + _dot(mb, la_lo)

    b_min = None
    for s in range(GLA_NSUB):
        rows = slice(s * c, (s + 1) * c)
        b = masked_sum(ci <= ri, rows)
        b_ref[rows, :] = b
        b_min = b[c - 1:c, :] if b_min is None else jnp.minimum(b_min, b[c - 1:c, :])
    fast = jnp.min(b_min) > GLA_FAST_MIN_B

    def chunk(rows, states, scores_fn):
        b = b_ref[rows, :]
        b_last = b[c - 1:c, :]
        q = q_ref[rows, :]
        k = k_ref[rows, :]
        q_in = (q * jnp.exp(b)).astype(BF16)
        k_out = (k * jnp.exp(b_last - b)).astype(BF16)
        a_chunk = jnp.exp(b_last)
        new_states = []
        for p in range(GLA_HEADS // 2):
            kc = slice(p * LANES, (p + 1) * LANES)
            v_p = v_ref[rows, 2 * p * GLA_DV:2 * (p + 1) * GLA_DV]
            st = states[p]
            o_inter = _dot_nt(q_in[:, kc], st.astype(BF16))
            for j, s_h in enumerate(scores_fn(p, q, k, b, q_in)):
                h = 2 * p + j
                dv = slice(j * GLA_DV, (j + 1) * GLA_DV)
                o_h = _dot(s_h.astype(BF16), v_p[:, dv]) + o_inter[:, dv]
                ms = jnp.mean(o_h * o_h, axis=-1, keepdims=True)
                y = o_h * lax.rsqrt(ms + EPS) * gn_ref[...]
                cols = slice(h * GLA_DV, (h + 1) * GLA_DV)
                o_ref[rows, cols] = (y * sg_ref[rows, cols].astype(F32)).astype(BF16)
            upd = _dot_tn(v_p, k_out[:, kc])
            new_states.append(a_chunk[:, kc] * st + jnp.where(same_head, upd, 0.0))
        return new_states

    def fast_scores(p, q, k, b, q_in):
        kc = slice(p * LANES, (p + 1) * LANES)
        k_div = k[:, kc] * jnp.exp(-b[:, kc])
        k_blk = jnp.concatenate([jnp.where(first_head, k_div, 0.0),
                                 jnp.where(first_head, 0.0, k_div)], axis=0).astype(BF16)
        s2 = _dot_nt(q_in[:, kc], k_blk)
        return [jnp.where(ri >= ci, s2[:, j * c:(j + 1) * c], 0.0) for j in range(2)]

    def safe_scores(rows):
        q = q_ref[rows, :]
        k = k_ref[rows, :]
        kb = k.astype(BF16)
        qb = q.astype(BF16)
        scores = []
        for h in range(GLA_HEADS):
            dk = slice(h * GLA_DK, (h + 1) * GLA_DK)
            scores.append(jnp.where(ri == ci, _dot_nt(qb[:, dk], kb[:, dk]), 0.0))
        half = c // 2
        while half >= 1:
            ref = (ri & (-2 * half)) + (half - 1)
            lo = jnp.minimum(ri, ref)
            hi = jnp.maximum(ri, ref)
            e = jnp.exp(masked_sum((ci > lo) & (ci <= hi), rows))
            q_l = (q * e).astype(BF16)
            k_l = (k * e).astype(BF16)
            pair = (((ri ^ ci) & (-2 * half)) == 0) & ((ri & half) != 0) & ((ci & half) == 0)
            for h in range(GLA_HEADS):
                dk = slice(h * GLA_DK, (h + 1) * GLA_DK)
                scores[h] = jnp.where(pair, _dot_nt(q_l[:, dk], k_l[:, dk]), scores[h])
            half //= 2
        return scores

    @pl.when(fast)
    def _():
        states = [st_ref[p] for p in range(GLA_HEADS // 2)]
        for s in range(GLA_NSUB):
            states = chunk(slice(s * c, (s + 1) * c), states, fast_scores)
        for p in range(GLA_HEADS // 2):
            st_ref[p] = states[p]

    @pl.when(jnp.logical_not(fast))
    def _():
        def body(s, carry):
            rows = pl.ds(pl.multiple_of(s * c, c), c)
            scores = safe_scores(rows)
            states = chunk(rows, [st_ref[p] for p in range(GLA_HEADS // 2)],
                           lambda p, *_: scores[2 * p:2 * p + 2])
            for p in range(GLA_HEADS // 2):
                st_ref[p] = states[p]
            return carry

        lax.fori_loop(0, GLA_NSUB, body, 0)


def _gla_call(q, k, la, v, sg, gn):
    s = q.shape[0]
    row = lambda i: (i, 0)
    const = lambda i: (0, 0)
    return pl.pallas_call(
        _gla_kernel,
        grid=(s // TM_GLA,),
        in_specs=[
            pl.BlockSpec((TM_GLA, GLA_KEY), row),
            pl.BlockSpec((TM_GLA, GLA_KEY), row),
            pl.BlockSpec((TM_GLA, GLA_KEY), row),
            pl.BlockSpec((TM_GLA, GLA_VAL), row),
            pl.BlockSpec((TM_GLA, GLA_VAL), row),
            pl.BlockSpec((1, GLA_DV), const),
        ],
        out_specs=pl.BlockSpec((TM_GLA, GLA_VAL), row),
        out_shape=jax.ShapeDtypeStruct((s, GLA_VAL), BF16),
        scratch_shapes=[
            pltpu.VMEM((GLA_HEADS // 2, 2 * GLA_DV, LANES), F32),
            pltpu.VMEM((TM_GLA, GLA_KEY), F32),
        ],
        compiler_params=pltpu.CompilerParams(
            dimension_semantics=("arbitrary",), vmem_limit_bytes=VMEM_LIMIT),
        name="gla",
    )(q, k, la, v, sg, gn)


def _conv_kernel(u_ref, w_ref, cb_ref, gg_ref, gb_ref, gm_ref, o_ref, ext_ref, sh_ref, y_ref):
    tm = TM_CONV

    @pl.when(pl.program_id(0) == 0)
    def _():
        ext_ref[0:CONV_HALO, :] = jnp.zeros((CONV_HALO, CONV_CH), F32)

    @pl.when(pl.program_id(0) > 0)
    def _():
        ext_ref[0:CONV_HALO, :] = ext_ref[tm:tm + CONV_HALO, :]

    ext_ref[CONV_HALO:, :] = u_ref[...]
    for p in range(1, SUBLANES):
        sh_ref[p - 1] = ext_ref[p:p + CONV_SH_ROWS, :]
    first = CONV_HALO - (CONV_W - 1)

    def row_block(r, carry):
        base = pl.multiple_of(r * CONV_RB, CONV_RB)
        for cg in range(CONV_CH // LANES):
            cols = slice(cg * LANES, (cg + 1) * LANES)
            acc = jnp.broadcast_to(cb_ref[:, cols], (CONV_RB, LANES))
            for t in range(CONV_W):
                shift = first + t
                p = shift % SUBLANES
                rows = pl.ds(base + (shift - p), CONV_RB)
                tap = ext_ref[rows, cols] if p == 0 else sh_ref[p - 1, rows, cols]
                acc = acc + w_ref[t:t + 1, cols] * tap
            y_ref[pl.ds(base, CONV_RB), cols] = acc
        return carry

    lax.fori_loop(0, tm // CONV_RB, row_block, 0)

    y = y_ref[...]
    gm = gm_ref[...]
    y_hi, y_lo = _split_bf16(y)
    mu = _dot(y_hi, gm) + _dot(y_lo, gm)
    d = y - mu
    d2_hi, d2_lo = _split_bf16(d * d)
    var = _dot(d2_hi, gm) + _dot(d2_lo, gm)
    yn = d * lax.rsqrt(var + EPS) * gg_ref[...] + gb_ref[...]
    o_ref[...] = (yn * _sigmoid(yn)).astype(BF16)


def _conv_call(u, w, cb, gg, gb, gm):
    s = u.shape[0]
    row = lambda i: (i, 0)
    const = lambda i: (0, 0)
    return pl.pallas_call(
        _conv_kernel,
        grid=(s // TM_CONV,),
        in_specs=[
            pl.BlockSpec((TM_CONV, CONV_CH), row),
            pl.BlockSpec((CONV_W, CONV_CH), const),
            pl.BlockSpec((1, CONV_CH), const),
            pl.BlockSpec((1, CONV_CH), const),
            pl.BlockSpec((1, CONV_CH), const),
            pl.BlockSpec((CONV_CH, CONV_CH), const),
        ],
        out_specs=pl.BlockSpec((TM_CONV, CONV_CH), row),
        out_shape=jax.ShapeDtypeStruct((s, CONV_CH), BF16),
        scratch_shapes=[
            pltpu.VMEM((TM_CONV + CONV_HALO, CONV_CH), F32),
            pltpu.VMEM((SUBLANES - 1, CONV_SH_ROWS, CONV_CH), F32),
            pltpu.VMEM((TM_CONV, CONV_CH), F32),
        ],
        compiler_params=pltpu.CompilerParams(
            dimension_semantics=("arbitrary",), vmem_limit_bytes=VMEM_LIMIT),
        name="conv",
    )(u, w, cb, gg, gb, gm)


def _mlp_kernel(x_ref, og_ref, oc_ref, wo_ref, g2_ref, w1_ref, w2_ref, gf_ref, o_ref):
    h = x_ref[...] + _dot(og_ref[...], wo_ref[0:GLA_VAL, :]) + _dot(oc_ref[...], wo_ref[GLA_VAL:, :])
    ms = jnp.mean(h * h, axis=-1, keepdims=True)
    hn = (h * lax.rsqrt(ms + EPS) * g2_ref[...]).astype(BF16)
    f = jnp.maximum(_dot(hn, w1_ref[...]), 0.0)
    acc = h + _dot((f * f).astype(BF16), w2_ref[...])
    ms2 = jnp.mean(acc * acc, axis=-1, keepdims=True)
    o_ref[...] = acc * lax.rsqrt(ms2 + EPS) * gf_ref[...]


def _mlp_call(x2, og, oc, wo, g2, w1, w2, gf):
    s = x2.shape[0]
    row = lambda i: (i, 0)
    const = lambda i: (0, 0)
    single = pl.Buffered(1)
    return pl.pallas_call(
        _mlp_kernel,
        grid=(s // TM_MLP,),
        in_specs=[
            pl.BlockSpec((TM_MLP, D_MODEL), row),
            pl.BlockSpec((TM_MLP, GLA_VAL), row),
            pl.BlockSpec((TM_MLP, CONV_CH), row),
            pl.BlockSpec((D_MODEL, D_MODEL), const, pipeline_mode=single),
            pl.BlockSpec((1, D_MODEL), const),
            pl.BlockSpec((D_MODEL, D_FF), const, pipeline_mode=single),
            pl.BlockSpec((D_FF, D_MODEL), const, pipeline_mode=single),
            pl.BlockSpec((1, D_MODEL), const),
        ],
        out_specs=pl.BlockSpec((TM_MLP, D_MODEL), row),
        out_shape=jax.ShapeDtypeStruct((s, D_MODEL), F32),
        compiler_params=pltpu.CompilerParams(
            dimension_semantics=("parallel",), vmem_limit_bytes=VMEM_LIMIT),
        name="mlp",
    )(x2, og, oc, wo, g2, w1, w2, gf)


def kernel(x, norm1_g, w_in, w_gate_up, b_gate, gla_norm_g, conv_w, conv_b, conv_norm_g,
           conv_norm_b, w_out, norm2_g, w_mlp_in, w_mlp_out, final_norm_g):
    bsz, seq, _ = x.shape
    x2 = x.reshape(bsz * seq, D_MODEL)
    assert bsz == 1, "state / halo carry across grid steps assumes one sequence"
    l = 0
    wi = w_in[l]
    w_p = jnp.concatenate(
        [wi[:, OFF_Q:OFF_A], wi[:, OFF_C:D_IN], wi[:, OFF_A:OFF_C],
         jnp.zeros((D_MODEL, Z_PAD - GATE_RANK), wi.dtype)], axis=1).astype(BF16)
    wg_p = jnp.concatenate(
        [w_gate_up[l], jnp.zeros((Z_PAD - GATE_RANK, GLA_KEY), w_gate_up.dtype)], axis=0).astype(BF16)
    q, k, la, v, sg, u = _proj_call(x2, norm1_g[l][None, :], w_p, wg_p, b_gate[l][None, :])
    o_gla = _gla_call(q, k, la, v, sg, gla_norm_g[l][None, :])
    grp = CONV_CH // CONV_GROUPS
    gidx = jnp.arange(CONV_CH) // grp
    gm = jnp.where(gidx[:, None] == gidx[None, :], 1.0 / grp, 0.0).astype(BF16)
    o_conv = _conv_call(u, conv_w[l], conv_b[l][None, :], conv_norm_g[l][None, :],
                        conv_norm_b[l][None, :], gm)
    out = _mlp_call(x2, o_gla, o_conv, w_out[l].astype(BF16), norm2_g[l][None, :],
                    w_mlp_in[l].astype(BF16), w_mlp_out[l].astype(BF16), final_norm_g[None, :])
    return out.reshape(bsz, seq, D_MODEL)
```

```python
import functools

import jax
import jax.numpy as jnp
from jax import lax
from jax.experimental import pallas as pl
from jax.experimental.pallas import tpu as pltpu

D_MODEL = 1024
GLA_HEADS = 4
GLA_DV = 128
GLA_DK = 64
GLA_KEY = GLA_HEADS * GLA_DK
GLA_VAL = GLA_HEADS * GLA_DV
GATE_RANK = 16
GATE_TAU = 16.0
CONV_CH = 512
CONV_GROUPS = 8
CONV_W = 31
D_FF = 4 * D_MODEL
EPS = 1e-6

OFF_Q = 0
OFF_K = OFF_Q + GLA_KEY
OFF_V = OFF_K + GLA_KEY
OFF_G = OFF_V + GLA_VAL
OFF_A = OFF_G + GLA_VAL
OFF_C = OFF_A + GATE_RANK
D_IN = OFF_C + 2 * CONV_CH

LANES = 128
SUBLANES = 8
Z_PAD = LANES
P_Q = 0
P_K = P_Q + GLA_KEY
P_V = P_K + GLA_KEY
P_G = P_V + GLA_VAL
P_CI = P_G + GLA_VAL
P_CG = P_CI + CONV_CH
P_Z = P_CG + CONV_CH
D_INP = P_Z + Z_PAD

TM_PROJ = 512
GLA_SUB = 256
GLA_NSUB = 4
TM_GLA = GLA_SUB * GLA_NSUB
TM_MLP = 512
CONV_HALO = 32
CONV_RB = 32
CONV_NORM_PARTS = 2
PROJ_CHUNK = 256
CONV_SH_ROWS = TM_PROJ + CONV_HALO - SUBLANES
GLA_FAST_MIN_B = -60.0

VMEM_LIMIT = 56 * 1024 * 1024

F32 = jnp.float32
BF16 = jnp.bfloat16


def _sigmoid(x):
    return 1.0 / (1.0 + jnp.exp(-x))


def _split_bf16(x):
    hi = x.astype(BF16)
    lo = (x - hi.astype(F32)).astype(BF16)
    return hi, lo


def _dot(a, b):
    return jnp.dot(a, b, preferred_element_type=F32)


def _dot_nt(a, b):
    return lax.dot_general(a, b, (((1,), (1,)), ((), ())), preferred_element_type=F32)


def _dot_tn(a, b):
    return lax.dot_general(a, b, (((0,), (0,)), ((), ())), preferred_element_type=F32)


def _proj_conv_kernel(x_ref, g1_ref, w_ref, wg_ref, bg_ref, cw_ref, cb_ref, gg_ref, gb_ref,
                      q_ref, k_ref, la_ref, v_ref, sg_ref, oc_ref, ext_ref, sh_ref, y_ref):
    tm = TM_PROJ

    @pl.when(pl.program_id(0) == 0)
    def _():
        ext_ref[...] = jnp.zeros_like(ext_ref)
        sh_ref[...] = jnp.zeros_like(sh_ref)

    x = x_ref[...]
    ms = jnp.mean(x * x, axis=-1, keepdims=True)
    xn = (x * lax.rsqrt(ms + EPS) * g1_ref[...]).astype(BF16)

    def proj_cols(lo, hi):
        return _dot(xn, w_ref[:, lo:hi])

    def do_q():
        q_ref[...] = proj_cols(P_Q, P_K) * (GLA_DK ** -0.5)

    def do_k():
        k_ref[...] = proj_cols(P_K, P_V)

    def do_v(j):
        v_ref[:, j * PROJ_CHUNK:(j + 1) * PROJ_CHUNK] = proj_cols(
            P_V + j * PROJ_CHUNK, P_V + (j + 1) * PROJ_CHUNK).astype(BF16)

    def do_g(j):
        g = proj_cols(P_G + j * PROJ_CHUNK, P_G + (j + 1) * PROJ_CHUNK)
        sg_ref[:, j * PROJ_CHUNK:(j + 1) * PROJ_CHUNK] = (g * _sigmoid(g)).astype(BF16)

    u_parts = {}

    def do_u(j):
        c_in = proj_cols(P_CI + j * PROJ_CHUNK, P_CI + (j + 1) * PROJ_CHUNK)
        c_gate = proj_cols(P_CG + j * PROJ_CHUNK, P_CG + (j + 1) * PROJ_CHUNK)
        u_parts[j] = c_in * _sigmoid(c_gate)

    def do_decay():
        z = proj_cols(P_Z, D_INP).astype(BF16)
        a_logit = _dot(z, wg_ref[...]) + bg_ref[...]
        la = jnp.minimum(a_logit, 0.0) - jnp.log(1.0 + jnp.exp(-jnp.abs(a_logit)))
        la_ref[...] = la * (1.0 / GATE_TAU)

    proj_items = [do_q, do_k, do_decay]
    for j in range(GLA_VAL // PROJ_CHUNK):
        proj_items += [functools.partial(do_v, j), functools.partial(do_g, j)]
    for j in range(CONV_CH // PROJ_CHUNK):
        proj_items.append(functools.partial(do_u, j))

    first = CONV_HALO - (CONV_W - 1)

    def conv_block(r):
        base = r * CONV_RB
        for cg in range(CONV_CH // LANES):
            cols = slice(cg * LANES, (cg + 1) * LANES)
            acc = jnp.broadcast_to(cb_ref[:, cols], (CONV_RB, LANES))
            for t in range(CONV_W):
                shift = first + t
                p = shift % SUBLANES
                rows = slice(base + shift - p, base + shift - p + CONV_RB)
                tap = ext_ref[rows, cols] if p == 0 else sh_ref[p - 1, rows, cols]
                acc = acc + cw_ref[t:t + 1, cols] * tap
            y_ref[base:base + CONV_RB, cols] = acc

    grp = CONV_CH // CONV_GROUPS
    n_rows = tm // CONV_NORM_PARTS
    low_half = lax.broadcasted_iota(jnp.int32, (n_rows, LANES), 1) < grp

    def group_mean(t):
        lo = jnp.sum(jnp.where(low_half, t, 0.0), axis=-1, keepdims=True)
        hi = jnp.sum(jnp.where(low_half, 0.0, t), axis=-1, keepdims=True)
        return jnp.where(low_half, lo, hi) * (1.0 / grp)

    def group_norm(rows):
        for cg in range(CONV_CH // LANES):
            cols = slice(cg * LANES, (cg + 1) * LANES)
            y = y_ref[rows, cols]
            d = y - group_mean(y)
            var = group_mean(d * d)
            yn = d * lax.rsqrt(var + EPS) * gg_ref[:, cols] + gb_ref[:, cols]
            oc_ref[rows, cols] = (yn * _sigmoid(yn)).astype(BF16)

    n_blk = tm // CONV_RB
    part_blk = n_blk // CONV_NORM_PARTS
    conv_items = []
    for r in range(n_blk):
        conv_items.append(functools.partial(conv_block, r))
        if (r + 1) % part_blk == 0:
            conv_items.append(functools.partial(
                group_norm, slice((r + 1 - part_blk) * CONV_RB, (r + 1) * CONV_RB)))

    done = 0
    for i, item in enumerate(proj_items):
        item()
        upto = (i + 1) * len(conv_items) // len(proj_items)
        for c_item in conv_items[done:upto]:
            c_item()
        done = upto

    ext_ref[0:CONV_HALO, :] = ext_ref[tm:tm + CONV_HALO, :]
    for j, u in u_parts.items():
        ext_ref[CONV_HALO:, j * PROJ_CHUNK:(j + 1) * PROJ_CHUNK] = u
    for p in range(1, SUBLANES):
        sh_ref[p - 1] = ext_ref[p:p + CONV_SH_ROWS, :]


def _proj_conv_call(x2, g1, w_p, wg_p, bg, cw, cb, gg, gb):
    s = x2.shape[0]
    n = s // TM_PROJ
    row = lambda i: (jnp.minimum(i, n - 1), 0)
    prev = lambda i: (jnp.maximum(i - 1, 0), 0)
    const = lambda i: (0, 0)
    return pl.pallas_call(
        _proj_conv_kernel,
        grid=(n + 1,),
        in_specs=[
            pl.BlockSpec((TM_PROJ, D_MODEL), row),
            pl.BlockSpec((1, D_MODEL), const),
            pl.BlockSpec((D_MODEL, D_INP), const, pipeline_mode=pl.Buffered(1)),
            pl.BlockSpec((Z_PAD, GLA_KEY), const),
            pl.BlockSpec((1, GLA_KEY), const),
            pl.BlockSpec((CONV_W, CONV_CH), const),
            pl.BlockSpec((1, CONV_CH), const),
            pl.BlockSpec((1, CONV_CH), const),
            pl.BlockSpec((1, CONV_CH), const),
        ],
        out_specs=[
            pl.BlockSpec((TM_PROJ, GLA_KEY), row),
            pl.BlockSpec((TM_PROJ, GLA_KEY), row),
            pl.BlockSpec((TM_PROJ, GLA_KEY), row),
            pl.BlockSpec((TM_PROJ, GLA_VAL), row),
            pl.BlockSpec((TM_PROJ, GLA_VAL), row),
            pl.BlockSpec((TM_PROJ, CONV_CH), prev),
        ],
        out_shape=[
            jax.ShapeDtypeStruct((s, GLA_KEY), F32),
            jax.ShapeDtypeStruct((s, GLA_KEY), F32),
            jax.ShapeDtypeStruct((s, GLA_KEY), F32),
            jax.ShapeDtypeStruct((s, GLA_VAL), BF16),
            jax.ShapeDtypeStruct((s, GLA_VAL), BF16),
            jax.ShapeDtypeStruct((s, CONV_CH), BF16),
        ],
        scratch_shapes=[
            pltpu.VMEM((TM_PROJ + CONV_HALO, CONV_CH), F32),
            pltpu.VMEM((SUBLANES - 1, CONV_SH_ROWS, CONV_CH), F32),
            pltpu.VMEM((TM_PROJ, CONV_CH), F32),
        ],
        compiler_params=pltpu.CompilerParams(
            dimension_semantics=("arbitrary",), vmem_limit_bytes=VMEM_LIMIT),
        name="proj_conv",
    )(x2, g1, w_p, wg_p, bg, cw, cb, gg, gb)


def _gla_kernel(q_ref, k_ref, la_ref, v_ref, sg_ref, gn_ref, o_ref, st_ref, b_ref):
    c = GLA_SUB

    @pl.when(pl.program_id(0) == 0)
    def _():
        st_ref[...] = jnp.zeros_like(st_ref)

    ri = lax.broadcasted_iota(jnp.int32, (c, c), 0)
    ci = lax.broadcasted_iota(jnp.int32, (c, c), 1)
    first_head = lax.broadcasted_iota(jnp.int32, (c, LANES), 1) < GLA_DK
    st_row = lax.broadcasted_iota(jnp.int32, (2 * GLA_DV, LANES), 0)
    st_lane = lax.broadcasted_iota(jnp.int32, (2 * GLA_DV, LANES), 1)
    same_head = (st_row < GLA_DV) == (st_lane < GLA_DK)

    def masked_sum(m, rows):
        mb = m.astype(BF16)
        la_hi, la_lo = _split_bf16(la_ref[rows, :])
        return _dot(mb, la_hi) + _dot(mb, la_lo)

    b_min = None
    for s in range(GLA_NSUB):
        rows = slice(s * c, (s + 1) * c)
        b = masked_sum(ci <= ri, rows)
        b_ref[rows, :] = b
        b_min = b[c - 1:c, :] if b_min is None else jnp.minimum(b_min, b[c - 1:c, :])
    fast = jnp.min(b_min) > GLA_FAST_MIN_B

    def chunk(rows, states, scores_fn):
        b = b_ref[rows, :]
        b_last = b[c - 1:c, :]
        q = q_ref[rows, :]
        k = k_ref[rows, :]
        q_in = (q * jnp.exp(b)).astype(BF16)
        k_out = (k * jnp.exp(b_last - b)).astype(BF16)
        a_chunk = jnp.exp(b_last)
        new_states = []
        for p in range(GLA_HEADS // 2):
            kc = slice(p * LANES, (p + 1) * LANES)
            v_p = v_ref[rows, 2 * p * GLA_DV:2 * (p + 1) * GLA_DV]
            st = states[p]
            o_inter = _dot_nt(q_in[:, kc], st.astype(BF16))
            for j, s_h in enumerate(scores_fn(p, q, k, b, q_in)):
                h = 2 * p + j
                dv = slice(j * GLA_DV, (j + 1) * GLA_DV)
                o_h = _dot(s_h.astype(BF16), v_p[:, dv]) + o_inter[:, dv]
                ms = jnp.mean(o_h * o_h, axis=-1, keepdims=True)
                y = o_h * lax.rsqrt(ms + EPS) * gn_ref[...]
                cols = slice(h * GLA_DV, (h + 1) * GLA_DV)
                o_ref[rows, cols] = (y * sg_ref[rows, cols].astype(F32)).astype(BF16)
            upd = _dot_tn(v_p, k_out[:, kc])
            new_states.append(a_chunk[:, kc] * st + jnp.where(same_head, upd, 0.0))
        return new_states

    def fast_scores(p, q, k, b, q_in):
        kc = slice(p * LANES, (p + 1) * LANES)
        k_div = k[:, kc] * jnp.exp(-b[:, kc])
        k_blk = jnp.concatenate([jnp.where(first_head, k_div, 0.0),
                                 jnp.where(first_head, 0.0, k_div)], axis=0).astype(BF16)
        s2 = _dot_nt(q_in[:, kc], k_blk)
        return [jnp.where(ri >= ci, s2[:, j * c:(j + 1) * c], 0.0) for j in range(2)]

    def safe_scores(rows):
        q = q_ref[rows, :]
        k = k_ref[rows, :]
        kb = k.astype(BF16)
        qb = q.astype(BF16)
        scores = []
        for h in range(GLA_HEADS):
            dk = slice(h * GLA_DK, (h + 1) * GLA_DK)
            scores.append(jnp.where(ri == ci, _dot_nt(qb[:, dk], kb[:, dk]), 0.0))
        half = c // 2
        while half >= 1:
            ref = (ri & (-2 * half)) + (half - 1)
            lo = jnp.minimum(ri, ref)
            hi = jnp.maximum(ri, ref)
            e = jnp.exp(masked_sum((ci > lo) & (ci <= hi), rows))
            q_l = (q * e).astype(BF16)
            k_l = (k * e).astype(BF16)
            pair = (((ri ^ ci) & (-2 * half)) == 0) & ((ri & half) != 0) & ((ci & half) == 0)
            for h in range(GLA_HEADS):
                dk = slice(h * GLA_DK, (h + 1) * GLA_DK)
                scores[h] = jnp.where(pair, _dot_nt(q_l[:, dk], k_l[:, dk]), scores[h])
            half //= 2
        return scores

    @pl.when(fast)
    def _():
        states = [st_ref[p] for p in range(GLA_HEADS // 2)]
        for s in range(GLA_NSUB):
            states = chunk(slice(s * c, (s + 1) * c), states, fast_scores)
        for p in range(GLA_HEADS // 2):
            st_ref[p] = states[p]

    @pl.when(jnp.logical_not(fast))
    def _():
        def body(s, carry):
            rows = pl.ds(pl.multiple_of(s * c, c), c)
            scores = safe_scores(rows)
            states = chunk(rows, [st_ref[p] for p in range(GLA_HEADS // 2)],
                           lambda p, *_: scores[2 * p:2 * p + 2])
            for p in range(GLA_HEADS // 2):
                st_ref[p] = states[p]
            return carry

        lax.fori_loop(0, GLA_NSUB, body, 0)


def _gla_call(q, k, la, v, sg, gn):
    s = q.shape[0]
    row = lambda i: (i, 0)
    const = lambda i: (0, 0)
    return pl.pallas_call(
        _gla_kernel,
        grid=(s // TM_GLA,),
        in_specs=[
            pl.BlockSpec((TM_GLA, GLA_KEY), row),
            pl.BlockSpec((TM_GLA, GLA_KEY), row),
            pl.BlockSpec((TM_GLA, GLA_KEY), row),
            pl.BlockSpec((TM_GLA, GLA_VAL), row),
            pl.BlockSpec((TM_GLA, GLA_VAL), row),
            pl.BlockSpec((1, GLA_DV), const),
        ],
        out_specs=pl.BlockSpec((TM_GLA, GLA_VAL), row),
        out_shape=jax.ShapeDtypeStruct((s, GLA_VAL), BF16),
        scratch_shapes=[
            pltpu.VMEM((GLA_HEADS // 2, 2 * GLA_DV, LANES), F32),
            pltpu.VMEM((TM_GLA, GLA_KEY), F32),
        ],
        compiler_params=pltpu.CompilerParams(
            dimension_semantics=("arbitrary",), vmem_limit_bytes=VMEM_LIMIT),
        name="gla",
    )(q, k, la, v, sg, gn)


def _mlp_kernel(x_ref, og_ref, oc_ref, wo_ref, g2_ref, w1_ref, w2_ref, gf_ref, o_ref):
    h = x_ref[...] + _dot(og_ref[...], wo_ref[0:GLA_VAL, :]) + _dot(oc_ref[...], wo_ref[GLA_VAL:, :])
    ms = jnp.mean(h * h, axis=-1, keepdims=True)
    hn = (h * lax.rsqrt(ms + EPS) * g2_ref[...]).astype(BF16)
    f = jnp.maximum(_dot(hn, w1_ref[...]), 0.0)
    acc = h + _dot((f * f).astype(BF16), w2_ref[...])
    ms2 = jnp.mean(acc * acc, axis=-1, keepdims=True)
    o_ref[...] = acc * lax.rsqrt(ms2 + EPS) * gf_ref[...]


def _mlp_call(x2, og, oc, wo, g2, w1, w2, gf):
    s = x2.shape[0]
    row = lambda i: (i, 0)
    const = lambda i: (0, 0)
    single = pl.Buffered(1)
    return pl.pallas_call(
        _mlp_kernel,
        grid=(s // TM_MLP,),
        in_specs=[
            pl.BlockSpec((TM_MLP, D_MODEL), row),
            pl.BlockSpec((TM_MLP, GLA_VAL), row),
            pl.BlockSpec((TM_MLP, CONV_CH), row),
            pl.BlockSpec((D_MODEL, D_MODEL), const, pipeline_mode=single),
            pl.BlockSpec((1, D_MODEL), const),
            pl.BlockSpec((D_MODEL, D_FF), const, pipeline_mode=single),
            pl.BlockSpec((D_FF, D_MODEL), const, pipeline_mode=single),
            pl.BlockSpec((1, D_MODEL), const),
        ],
        out_specs=pl.BlockSpec((TM_MLP, D_MODEL), row),
        out_shape=jax.ShapeDtypeStruct((s, D_MODEL), F32),
        compiler_params=pltpu.CompilerParams(
            dimension_semantics=("parallel",), vmem_limit_bytes=VMEM_LIMIT),
        name="mlp",
    )(x2, og, oc, wo, g2, w1, w2, gf)


def kernel(x, norm1_g, w_in, w_gate_up, b_gate, gla_norm_g, conv_w, conv_b, conv_norm_g,
           conv_norm_b, w_out, norm2_g, w_mlp_in, w_mlp_out, final_norm_g):
    bsz, seq, _ = x.shape
    x2 = x.reshape(bsz * seq, D_MODEL)
    assert bsz == 1, "state / halo carry across grid steps assumes one sequence"
    l = 0
    wi = w_in[l]
    w_p = jnp.concatenate(
        [wi[:, OFF_Q:OFF_A], wi[:, OFF_C:D_IN], wi[:, OFF_A:OFF_C],
         jnp.zeros((D_MODEL, Z_PAD - GATE_RANK), wi.dtype)], axis=1).astype(BF16)
    wg_p = jnp.concatenate(
        [w_gate_up[l], jnp.zeros((Z_PAD - GATE_RANK, GLA_KEY), w_gate_up.dtype)], axis=0).astype(BF16)
    q, k, la, v, sg, o_conv = _proj_conv_call(
        x2, norm1_g[l][None, :], w_p, wg_p, b_gate[l][None, :], conv_w[l], conv_b[l][None, :],
        conv_norm_g[l][None, :], conv_norm_b[l][None, :])
    o_gla = _gla_call(q, k, la, v, sg, gla_norm_g[l][None, :])
    out = _mlp_call(x2, o_gla, o_conv, w_out[l].astype(BF16), norm2_g[l][None, :],
                    w_mlp_in[l].astype(BF16), w_mlp_out[l].astype(BF16), final_norm_g[None, :])
    return out.reshape(bsz, seq, D_MODEL)
```

```python
import functools

import jax
import jax.numpy as jnp
from jax import lax
from jax.experimental import pallas as pl
from jax.experimental.pallas import tpu as pltpu

D_MODEL = 1024
GLA_HEADS = 4
GLA_DV = 128
GLA_DK = 64
GLA_KEY = GLA_HEADS * GLA_DK
GLA_VAL = GLA_HEADS * GLA_DV
GATE_RANK = 16
GATE_TAU = 16.0
CONV_CH = 512
CONV_GROUPS = 8
CONV_W = 31
D_FF = 4 * D_MODEL
EPS = 1e-6

OFF_Q = 0
OFF_K = OFF_Q + GLA_KEY
OFF_V = OFF_K + GLA_KEY
OFF_G = OFF_V + GLA_VAL
OFF_A = OFF_G + GLA_VAL
OFF_C = OFF_A + GATE_RANK
D_IN = OFF_C + 2 * CONV_CH

LANES = 128
SUBLANES = 8
Z_PAD = LANES
P_Q = 0
P_K = P_Q + GLA_KEY
P_V = P_K + GLA_KEY
P_G = P_V + GLA_VAL
P_CI = P_G + GLA_VAL
P_CG = P_CI + CONV_CH
P_Z = P_CG + CONV_CH
D_INP = P_Z + Z_PAD

TM_PROJ = 512
GLA_SUB = 256
GLA_NSUB = 4
TM_GLA = GLA_SUB * GLA_NSUB
TM_MLP = 512
FF_CHUNK = 512
OUT_CHUNK = 256
CONV_HALO = 32
CONV_RB = 32
CONV_SH_ROWS = TM_MLP + CONV_HALO - SUBLANES
GLA_FAST_MIN_B = -60.0

VMEM_LIMIT = 56 * 1024 * 1024

F32 = jnp.float32
BF16 = jnp.bfloat16


def _sigmoid(x):
    return 1.0 / (1.0 + jnp.exp(-x))


def _split_bf16(x):
    hi = x.astype(BF16)
    lo = (x - hi.astype(F32)).astype(BF16)
    return hi, lo


def _dot(a, b):
    return jnp.dot(a, b, preferred_element_type=F32)


def _dot_nt(a, b):
    return lax.dot_general(a, b, (((1,), (1,)), ((), ())), preferred_element_type=F32)


def _dot_tn(a, b):
    return lax.dot_general(a, b, (((0,), (0,)), ((), ())), preferred_element_type=F32)


def _proj_kernel(x_ref, g1_ref, w_ref, wg_ref, bg_ref,
                 q_ref, k_ref, la_ref, v_ref, sg_ref, u_ref):
    x = x_ref[...]
    ms = jnp.mean(x * x, axis=-1, keepdims=True)
    xn = (x * lax.rsqrt(ms + EPS) * g1_ref[...]).astype(BF16)
    proj = _dot(xn, w_ref[...])
    q_ref[...] = proj[:, P_Q:P_K] * (GLA_DK ** -0.5)
    k_ref[...] = proj[:, P_K:P_V]
    v_ref[...] = proj[:, P_V:P_G].astype(BF16)
    g = proj[:, P_G:P_CI]
    sg_ref[...] = (g * _sigmoid(g)).astype(BF16)
    u_ref[...] = proj[:, P_CI:P_CG] * _sigmoid(proj[:, P_CG:P_Z])
    z = proj[:, P_Z:D_INP].astype(BF16)
    a_logit = _dot(z, wg_ref[...]) + bg_ref[...]
    la = jnp.minimum(a_logit, 0.0) - jnp.log(1.0 + jnp.exp(-jnp.abs(a_logit)))
    la_ref[...] = la * (1.0 / GATE_TAU)


def _proj_call(x2, g1, w_p, wg_p, bg):
    s = x2.shape[0]
    row = lambda i: (i, 0)
    const = lambda i: (0, 0)
    return pl.pallas_call(
        _proj_kernel,
        grid=(s // TM_PROJ,),
        in_specs=[
            pl.BlockSpec((TM_PROJ, D_MODEL), row),
            pl.BlockSpec((1, D_MODEL), const),
            pl.BlockSpec((D_MODEL, D_INP), const, pipeline_mode=pl.Buffered(1)),
            pl.BlockSpec((Z_PAD, GLA_KEY), const),
            pl.BlockSpec((1, GLA_KEY), const),
        ],
        out_specs=[
            pl.BlockSpec((TM_PROJ, GLA_KEY), row),
            pl.BlockSpec((TM_PROJ, GLA_KEY), row),
            pl.BlockSpec((TM_PROJ, GLA_KEY), row),
            pl.BlockSpec((TM_PROJ, GLA_VAL), row),
            pl.BlockSpec((TM_PROJ, GLA_VAL), row),
            pl.BlockSpec((TM_PROJ, CONV_CH), row),
        ],
        out_shape=[
            jax.ShapeDtypeStruct((s, GLA_KEY), F32),
            jax.ShapeDtypeStruct((s, GLA_KEY), F32),
            jax.ShapeDtypeStruct((s, GLA_KEY), F32),
            jax.ShapeDtypeStruct((s, GLA_VAL), BF16),
            jax.ShapeDtypeStruct((s, GLA_VAL), BF16),
            jax.ShapeDtypeStruct((s, CONV_CH), F32),
        ],
        compiler_params=pltpu.CompilerParams(
            dimension_semantics=("parallel",), vmem_limit_bytes=VMEM_LIMIT),
        name="proj",
    )(x2, g1, w_p, wg_p, bg)


def _gla_kernel(q_ref, k_ref, la_ref, v_ref, sg_ref, gn_ref, o_ref, st_ref, b_ref):
    c = GLA_SUB

    @pl.when(pl.program_id(0) == 0)
    def _():
        st_ref[...] = jnp.zeros_like(st_ref)

    ri = lax.broadcasted_iota(jnp.int32, (c, c), 0)
    ci = lax.broadcasted_iota(jnp.int32, (c, c), 1)
    first_head = lax.broadcasted_iota(jnp.int32, (c, LANES), 1) < GLA_DK
    st_row = lax.broadcasted_iota(jnp.int32, (2 * GLA_DV, LANES), 0)
    st_lane = lax.broadcasted_iota(jnp.int32, (2 * GLA_DV, LANES), 1)
    same_head = (st_row < GLA_DV) == (st_lane < GLA_DK)

    def masked_sum(m, rows):
        mb = m.astype(BF16)
        la_hi, la_lo = _split_bf16(la_ref[rows, :])
        return _dot(mb, la_hi) + _dot(mb, la_lo)

    b_min = None
    for s in range(GLA_NSUB):
        rows = slice(s * c, (s + 1) * c)
        b = masked_sum(ci <= ri, rows)
        b_ref[rows, :] = b
        b_min = b[c - 1:c, :] if b_min is None else jnp.minimum(b_min, b[c - 1:c, :])
    fast = jnp.min(b_min) > GLA_FAST_MIN_B

    def chunk(rows, states, scores_fn):
        b = b_ref[rows, :]
        b_last = b[c - 1:c, :]
        q = q_ref[rows, :]
        k = k_ref[rows, :]
        q_in = (q * jnp.exp(b)).astype(BF16)
        k_out = (k * jnp.exp(b_last - b)).astype(BF16)
        a_chunk = jnp.exp(b_last)
        new_states = []
        for p in range(GLA_HEADS // 2):
            kc = slice(p * LANES, (p + 1) * LANES)
            v_p = v_ref[rows, 2 * p * GLA_DV:2 * (p + 1) * GLA_DV]
            st = states[p]
            o_inter = _dot_nt(q_in[:, kc], st.astype(BF16))
            for j, s_h in enumerate(scores_fn(p, q, k, b, q_in)):
                h = 2 * p + j
                dv = slice(j * GLA_DV, (j + 1) * GLA_DV)
                o_h = _dot(s_h.astype(BF16), v_p[:, dv]) + o_inter[:, dv]
                ms = jnp.mean(o_h * o_h, axis=-1, keepdims=True)
                y = o_h * lax.rsqrt(ms + EPS) * gn_ref[...]
                cols = slice(h * GLA_DV, (h + 1) * GLA_DV)
                o_ref[rows, cols] = (y * sg_ref[rows, cols].astype(F32)).astype(BF16)
            upd = _dot_tn(v_p, k_out[:, kc])
            new_states.append(a_chunk[:, kc] * st + jnp.where(same_head, upd, 0.0))
        return new_states

    def fast_scores(p, q, k, b, q_in):
        kc = slice(p * LANES, (p + 1) * LANES)
        k_div = k[:, kc] * jnp.exp(-b[:, kc])
        k_blk = jnp.concatenate([jnp.where(first_head, k_div, 0.0),
                                 jnp.where(first_head, 0.0, k_div)], axis=0).astype(BF16)
        s2 = _dot_nt(q_in[:, kc], k_blk)
        return [jnp.where(ri >= ci, s2[:, j * c:(j + 1) * c], 0.0) for j in range(2)]

    def safe_scores(rows):
        q = q_ref[rows, :]
        k = k_ref[rows, :]
        kb = k.astype(BF16)
        qb = q.astype(BF16)
        scores = []
        for h in range(GLA_HEADS):
            dk = slice(h * GLA_DK, (h + 1) * GLA_DK)
            scores.append(jnp.where(ri == ci, _dot_nt(qb[:, dk], kb[:, dk]), 0.0))
        half = c // 2
        while half >= 1:
            ref = (ri & (-2 * half)) + (half - 1)
            lo = jnp.minimum(ri, ref)
            hi = jnp.maximum(ri, ref)
            e = jnp.exp(masked_sum((ci > lo) & (ci <= hi), rows))
            q_l = (q * e).astype(BF16)
            k_l = (k * e).astype(BF16)
            pair = (((ri ^ ci) & (-2 * half)) == 0) & ((ri & half) != 0) & ((ci & half) == 0)
            for h in range(GLA_HEADS):
                dk = slice(h * GLA_DK, (h + 1) * GLA_DK)
                scores[h] = jnp.where(pair, _dot_nt(q_l[:, dk], k_l[:, dk]), scores[h])
            half //= 2
        return scores

    @pl.when(fast)
    def _():
        states = [st_ref[p] for p in range(GLA_HEADS // 2)]
        for s in range(GLA_NSUB):
            states = chunk(slice(s * c, (s + 1) * c), states, fast_scores)
        for p in range(GLA_HEADS // 2):
            st_ref[p] = states[p]

    @pl.when(jnp.logical_not(fast))
    def _():
        def body(s, carry):
            rows = pl.ds(pl.multiple_of(s * c, c), c)
            scores = safe_scores(rows)
            states = chunk(rows, [st_ref[p] for p in range(GLA_HEADS // 2)],
                           lambda p, *_: scores[2 * p:2 * p + 2])
            for p in range(GLA_HEADS // 2):
                st_ref[p] = states[p]
            return carry

        lax.fori_loop(0, GLA_NSUB, body, 0)


def _gla_call(q, k, la, v, sg, gn):
    s = q.shape[0]
    row = lambda i: (i, 0)
    const = lambda i: (0, 0)
    return pl.pallas_call(
        _gla_kernel,
        grid=(s // TM_GLA,),
        in_specs=[
            pl.BlockSpec((TM_GLA, GLA_KEY), row),
            pl.BlockSpec((TM_GLA, GLA_KEY), row),
            pl.BlockSpec((TM_GLA, GLA_KEY), row),
            pl.BlockSpec((TM_GLA, GLA_VAL), row),
            pl.BlockSpec((TM_GLA, GLA_VAL), row),
            pl.BlockSpec((1, GLA_DV), const),
        ],
        out_specs=pl.BlockSpec((TM_GLA, GLA_VAL), row),
        out_shape=jax.ShapeDtypeStruct((s, GLA_VAL), BF16),
        scratch_shapes=[
            pltpu.VMEM((GLA_HEADS // 2, 2 * GLA_DV, LANES), F32),
            pltpu.VMEM((TM_GLA, GLA_KEY), F32),
        ],
        compiler_params=pltpu.CompilerParams(
            dimension_semantics=("arbitrary",), vmem_limit_bytes=VMEM_LIMIT),
        name="gla",
    )(q, k, la, v, sg, gn)


def _group_mean(t, low_half, size):
    lo = jnp.sum(jnp.where(low_half, t, 0.0), axis=-1, keepdims=True)
    hi = jnp.sum(jnp.where(low_half, 0.0, t), axis=-1, keepdims=True)
    return jnp.where(low_half, lo, hi) * (1.0 / size)


def _interleave(mxu_items, vec_items):
    done = 0
    for i, item in enumerate(mxu_items):
        item()
        upto = (i + 1) * len(vec_items) // len(mxu_items)
        for v_item in vec_items[done:upto]:
            v_item()
        done = upto


def _mlp_conv_kernel(x_ref, og_ref, un_ref, u0_ref, cw_ref, cb_ref, gg_ref, gb_ref,
                     wo_ref, g2_ref, w1_ref, w2_ref, gf_ref,
                     o_ref, ext_ref, sh_ref, oc_ref, f_ref):
    tm = TM_MLP
    grp = CONV_CH // CONV_GROUPS
    low_half = lax.broadcasted_iota(jnp.int32, (CONV_RB, LANES), 1) < grp
    first = CONV_HALO - (CONV_W - 1)

    def shifted_copy(p):
        sh_ref[p - 1] = ext_ref[p:p + CONV_SH_ROWS, :]

    def conv_block(base):
        for cg in range(CONV_CH // LANES):
            cols = slice(cg * LANES, (cg + 1) * LANES)
            acc = jnp.broadcast_to(cb_ref[:, cols], (CONV_RB, LANES))
            for t in range(CONV_W):
                shift = first + t
                p = shift % SUBLANES
                rows = pl.ds(base + (shift - p), CONV_RB)
                tap = ext_ref[rows, cols] if p == 0 else sh_ref[p - 1, rows, cols]
                acc = acc + cw_ref[t:t + 1, cols] * tap
            d = acc - _group_mean(acc, low_half, grp)
            var = _group_mean(d * d, low_half, grp)
            yn = d * lax.rsqrt(var + EPS) * gg_ref[:, cols] + gb_ref[:, cols]
            oc_ref[pl.ds(base, CONV_RB), cols] = (yn * _sigmoid(yn)).astype(BF16)

    @pl.when(pl.program_id(0) == 0)
    def _():
        ext_ref[0:CONV_HALO, :] = jnp.zeros((CONV_HALO, CONV_CH), F32)
        ext_ref[CONV_HALO:, :] = u0_ref[...]
        for p in range(1, SUBLANES):
            shifted_copy(p)

        def body(r, carry):
            conv_block(pl.multiple_of(r * CONV_RB, CONV_RB))
            return carry

        lax.fori_loop(0, tm // CONV_RB, body, 0)

    h = x_ref[...] + _dot(og_ref[...], wo_ref[0:GLA_VAL, :]) + _dot(oc_ref[...], wo_ref[GLA_VAL:, :])
    ms = jnp.mean(h * h, axis=-1, keepdims=True)
    hn = (h * lax.rsqrt(ms + EPS) * g2_ref[...]).astype(BF16)

    def roll_window():
        ext_ref[0:CONV_HALO, :] = ext_ref[tm:tm + CONV_HALO, :]
        ext_ref[CONV_HALO:, :] = un_ref[...]

    vec_items = [roll_window] + [functools.partial(shifted_copy, p) for p in range(1, SUBLANES)]
    vec_items += [functools.partial(conv_block, r * CONV_RB) for r in range(tm // CONV_RB)]

    def ff_in(j):
        cols = slice(j * FF_CHUNK, (j + 1) * FF_CHUNK)
        f = jnp.maximum(_dot(hn, w1_ref[:, cols]), 0.0)
        f_ref[:, cols] = (f * f).astype(BF16)

    def ff_out(j):
        cols = slice(j * OUT_CHUNK, (j + 1) * OUT_CHUNK)
        o_ref[:, cols] = h[:, cols] + _dot(f_ref[...], w2_ref[:, cols])

    mxu_items = [functools.partial(ff_in, j) for j in range(D_FF // FF_CHUNK)]
    mxu_items += [functools.partial(ff_out, j) for j in range(D_MODEL // OUT_CHUNK)]
    _interleave(mxu_items, vec_items)

    acc = o_ref[...]
    ms2 = jnp.mean(acc * acc, axis=-1, keepdims=True)
    o_ref[...] = acc * lax.rsqrt(ms2 + EPS) * gf_ref[...]


def _mlp_conv_call(x2, og, u, cw, cb, gg, gb, wo, g2, w1, w2, gf):
    s = x2.shape[0]
    n = s // TM_MLP
    row = lambda i: (i, 0)
    nxt = lambda i: (jnp.minimum(i + 1, n - 1), 0)
    const = lambda i: (0, 0)
    single = pl.Buffered(1)
    return pl.pallas_call(
        _mlp_conv_kernel,
        grid=(n,),
        in_specs=[
            pl.BlockSpec((TM_MLP, D_MODEL), row),
            pl.BlockSpec((TM_MLP, GLA_VAL), row),
            pl.BlockSpec((TM_MLP, CONV_CH), nxt),
            pl.BlockSpec((TM_MLP, CONV_CH), const, pipeline_mode=single),
            pl.BlockSpec((CONV_W, CONV_CH), const),
            pl.BlockSpec((1, CONV_CH), const),
            pl.BlockSpec((1, CONV_CH), const),
            pl.BlockSpec((1, CONV_CH), const),
            pl.BlockSpec((D_MODEL, D_MODEL), const, pipeline_mode=single),
            pl.BlockSpec((1, D_MODEL), const),
            pl.BlockSpec((D_MODEL, D_FF), const, pipeline_mode=single),
            pl.BlockSpec((D_FF, D_MODEL), const, pipeline_mode=single),
            pl.BlockSpec((1, D_MODEL), const),
        ],
        out_specs=pl.BlockSpec((TM_MLP, D_MODEL), row),
        out_shape=jax.ShapeDtypeStruct((s, D_MODEL), F32),
        scratch_shapes=[
            pltpu.VMEM((TM_MLP + CONV_HALO, CONV_CH), F32),
            pltpu.VMEM((SUBLANES - 1, CONV_SH_ROWS, CONV_CH), F32),
            pltpu.VMEM((TM_MLP, CONV_CH), BF16),
            pltpu.VMEM((TM_MLP, D_FF), BF16),
        ],
        compiler_params=pltpu.CompilerParams(
            dimension_semantics=("arbitrary",), vmem_limit_bytes=VMEM_LIMIT),
        name="mlp_conv",
    )(x2, og, u, u, cw, cb, gg, gb, wo, g2, w1, w2, gf)


def kernel(x, norm1_g, w_in, w_gate_up, b_gate, gla_norm_g, conv_w, conv_b, conv_norm_g,
           conv_norm_b, w_out, norm2_g, w_mlp_in, w_mlp_out, final_norm_g):
    bsz, seq, _ = x.shape
    x2 = x.reshape(bsz * seq, D_MODEL)
    assert bsz == 1, "state / halo carry across grid steps assumes one sequence"
    l = 0
    wi = w_in[l]
    w_p = jnp.concatenate(
        [wi[:, OFF_Q:OFF_A], wi[:, OFF_C:D_IN], wi[:, OFF_A:OFF_C],
         jnp.zeros((D_MODEL, Z_PAD - GATE_RANK), wi.dtype)], axis=1).astype(BF16)
    wg_p = jnp.concatenate(
        [w_gate_up[l], jnp.zeros((Z_PAD - GATE_RANK, GLA_KEY), w_gate_up.dtype)], axis=0).astype(BF16)
    q, k, la, v, sg, u = _proj_call(x2, norm1_g[l][None, :], w_p, wg_p, b_gate[l][None, :])
    o_gla = _gla_call(q, k, la, v, sg, gla_norm_g[l][None, :])
    out = _mlp_conv_call(x2, o_gla, u, conv_w[l], conv_b[l][None, :], conv_norm_g[l][None, :],
                         conv_norm_b[l][None, :], w_out[l].astype(BF16), norm2_g[l][None, :],
                         w_mlp_in[l].astype(BF16), w_mlp_out[l].astype(BF16),
                         final_norm_g[None, :])
    return out.reshape(bsz, seq, D_MODEL)
```

```python
import functools

import jax
import jax.numpy as jnp
from jax import lax
from jax.experimental import pallas as pl
from jax.experimental.pallas import tpu as pltpu

D_MODEL = 1024
GLA_HEADS = 4
GLA_DV = 128
GLA_DK = 64
GLA_KEY = GLA_HEADS * GLA_DK
GLA_VAL = GLA_HEADS * GLA_DV
GATE_RANK = 16
GATE_TAU = 16.0
CONV_CH = 512
CONV_GROUPS = 8
CONV_W = 31
D_FF = 4 * D_MODEL
EPS = 1e-6

OFF_Q = 0
OFF_K = OFF_Q + GLA_KEY
OFF_V = OFF_K + GLA_KEY
OFF_G = OFF_V + GLA_VAL
OFF_A = OFF_G + GLA_VAL
OFF_C = OFF_A + GATE_RANK
D_IN = OFF_C + 2 * CONV_CH

LANES = 128
SUBLANES = 8
Z_PAD = LANES
P_Q = 0
P_K = P_Q + GLA_KEY
P_V = P_K + GLA_KEY
P_G = P_V + GLA_VAL
P_CI = P_G + GLA_VAL
P_CG = P_CI + CONV_CH
P_Z = P_CG + CONV_CH
D_INP = P_Z + Z_PAD

TM_PROJ = 512
GLA_SUB = 256
GLA_NSUB = 8
TM_GLA = GLA_SUB * GLA_NSUB
TM_MLP = 512
FF_CHUNK = 512
OUT_CHUNK = 256
CONV_HALO = 32
CONV_RB = 32
CONV_SH_ROWS = TM_MLP + CONV_HALO - SUBLANES
GLA_FAST_MIN_B = -60.0

VMEM_LIMIT = 56 * 1024 * 1024

F32 = jnp.float32
BF16 = jnp.bfloat16


def _sigmoid(x):
    return 1.0 / (1.0 + jnp.exp(-x))


def _split_bf16(x):
    hi = x.astype(BF16)
    lo = (x - hi.astype(F32)).astype(BF16)
    return hi, lo


def _dot(a, b):
    return jnp.dot(a, b, preferred_element_type=F32)


def _dot_nt(a, b):
    return lax.dot_general(a, b, (((1,), (1,)), ((), ())), preferred_element_type=F32)


def _dot_tn(a, b):
    return lax.dot_general(a, b, (((0,), (0,)), ((), ())), preferred_element_type=F32)


def _proj_kernel(x_ref, g1_ref, w_ref, wg_ref, bg_ref,
                 q_ref, k_ref, la_ref, v_ref, sg_ref, u_ref):
    x = x_ref[...]
    ms = jnp.mean(x * x, axis=-1, keepdims=True)
    xn = (x * lax.rsqrt(ms + EPS) * g1_ref[...]).astype(BF16)
    proj = _dot(xn, w_ref[...])
    q_ref[...] = proj[:, P_Q:P_K] * (GLA_DK ** -0.5)
    k_ref[...] = proj[:, P_K:P_V]
    v_ref[...] = proj[:, P_V:P_G].astype(BF16)
    g = proj[:, P_G:P_CI]
    sg_ref[...] = (g * _sigmoid(g)).astype(BF16)
    u_ref[...] = proj[:, P_CI:P_CG] * _sigmoid(proj[:, P_CG:P_Z])
    z = proj[:, P_Z:D_INP].astype(BF16)
    a_logit = _dot(z, wg_ref[...]) + bg_ref[...]
    la = jnp.minimum(a_logit, 0.0) - jnp.log(1.0 + jnp.exp(-jnp.abs(a_logit)))
    la_ref[...] = la * (1.0 / GATE_TAU)


def _proj_call(x2, g1, w_p, wg_p, bg):
    s = x2.shape[0]
    row = lambda i: (i, 0)
    const = lambda i: (0, 0)
    return pl.pallas_call(
        _proj_kernel,
        grid=(s // TM_PROJ,),
        in_specs=[
            pl.BlockSpec((TM_PROJ, D_MODEL), row),
            pl.BlockSpec((1, D_MODEL), const),
            pl.BlockSpec((D_MODEL, D_INP), const, pipeline_mode=pl.Buffered(1)),
            pl.BlockSpec((Z_PAD, GLA_KEY), const),
            pl.BlockSpec((1, GLA_KEY), const),
        ],
        out_specs=[
            pl.BlockSpec((TM_PROJ, GLA_KEY), row),
            pl.BlockSpec((TM_PROJ, GLA_KEY), row),
            pl.BlockSpec((TM_PROJ, GLA_KEY), row),
            pl.BlockSpec((TM_PROJ, GLA_VAL), row),
            pl.BlockSpec((TM_PROJ, GLA_VAL), row),
            pl.BlockSpec((TM_PROJ, CONV_CH), row),
        ],
        out_shape=[
            jax.ShapeDtypeStruct((s, GLA_KEY), F32),
            jax.ShapeDtypeStruct((s, GLA_KEY), F32),
            jax.ShapeDtypeStruct((s, GLA_KEY), F32),
            jax.ShapeDtypeStruct((s, GLA_VAL), BF16),
            jax.ShapeDtypeStruct((s, GLA_VAL), BF16),
            jax.ShapeDtypeStruct((s, CONV_CH), F32),
        ],
        compiler_params=pltpu.CompilerParams(
            dimension_semantics=("parallel",), vmem_limit_bytes=VMEM_LIMIT),
        name="proj",
    )(x2, g1, w_p, wg_p, bg)


def _gla_kernel(q_ref, k_ref, la_ref, v_ref, sg_ref, gn_ref, o_ref, st_ref, b_ref):
    c = GLA_SUB

    @pl.when(pl.program_id(0) == 0)
    def _():
        st_ref[...] = jnp.zeros_like(st_ref)

    ri = lax.broadcasted_iota(jnp.int32, (c, c), 0)
    ci = lax.broadcasted_iota(jnp.int32, (c, c), 1)
    first_head = lax.broadcasted_iota(jnp.int32, (c, LANES), 1) < GLA_DK
    st_row = lax.broadcasted_iota(jnp.int32, (2 * GLA_DV, LANES), 0)
    st_lane = lax.broadcasted_iota(jnp.int32, (2 * GLA_DV, LANES), 1)
    same_head = (st_row < GLA_DV) == (st_lane < GLA_DK)

    def masked_sum(m, rows):
        mb = m.astype(BF16)
        la_hi, la_lo = _split_bf16(la_ref[rows, :])
        return _dot(mb, la_hi) + _dot(mb, la_lo)

    b_min = None
    for s in range(GLA_NSUB):
        rows = slice(s * c, (s + 1) * c)
        b = masked_sum(ci <= ri, rows)
        b_ref[rows, :] = b
        b_min = b[c - 1:c, :] if b_min is None else jnp.minimum(b_min, b[c - 1:c, :])
    fast = jnp.min(b_min) > GLA_FAST_MIN_B

    def chunk(rows, states, scores_fn):
        b = b_ref[rows, :]
        b_last = b[c - 1:c, :]
        q = q_ref[rows, :]
        k = k_ref[rows, :]
        q_in = (q * jnp.exp(b)).astype(BF16)
        k_out = (k * jnp.exp(b_last - b)).astype(BF16)
        a_chunk = jnp.exp(b_last)
        new_states = []
        for p in range(GLA_HEADS // 2):
            kc = slice(p * LANES, (p + 1) * LANES)
            v_p = v_ref[rows, 2 * p * GLA_DV:2 * (p + 1) * GLA_DV]
            st = states[p]
            o_inter = _dot_nt(q_in[:, kc], st.astype(BF16))
            for j, s_h in enumerate(scores_fn(p, q, k, b, q_in)):
                h = 2 * p + j
                dv = slice(j * GLA_DV, (j + 1) * GLA_DV)
                o_h = _dot(s_h.astype(BF16), v_p[:, dv]) + o_inter[:, dv]
                ms = jnp.mean(o_h * o_h, axis=-1, keepdims=True)
                y = o_h * lax.rsqrt(ms + EPS) * gn_ref[...]
                cols = slice(h * GLA_DV, (h + 1) * GLA_DV)
                o_ref[rows, cols] = (y * sg_ref[rows, cols].astype(F32)).astype(BF16)
            upd = _dot_tn(v_p, k_out[:, kc])
            new_states.append(a_chunk[:, kc] * st + jnp.where(same_head, upd, 0.0))
        return new_states

    def fast_scores(p, q, k, b, q_in):
        kc = slice(p * LANES, (p + 1) * LANES)
        k_div = k[:, kc] * jnp.exp(-b[:, kc])
        k_blk = jnp.concatenate([jnp.where(first_head, k_div, 0.0),
                                 jnp.where(first_head, 0.0, k_div)], axis=0).astype(BF16)
        s2 = _dot_nt(q_in[:, kc], k_blk)
        return [jnp.where(ri >= ci, s2[:, j * c:(j + 1) * c], 0.0) for j in range(2)]

    def safe_scores(rows):
        q = q_ref[rows, :]
        k = k_ref[rows, :]
        kb = k.astype(BF16)
        qb = q.astype(BF16)
        scores = []
        for h in range(GLA_HEADS):
            dk = slice(h * GLA_DK, (h + 1) * GLA_DK)
            scores.append(jnp.where(ri == ci, _dot_nt(qb[:, dk], kb[:, dk]), 0.0))
        half = c // 2
        while half >= 1:
            ref = (ri & (-2 * half)) + (half - 1)
            lo = jnp.minimum(ri, ref)
            hi = jnp.maximum(ri, ref)
            e = jnp.exp(masked_sum((ci > lo) & (ci <= hi), rows))
            q_l = (q * e).astype(BF16)
            k_l = (k * e).astype(BF16)
            pair = (((ri ^ ci) & (-2 * half)) == 0) & ((ri & half) != 0) & ((ci & half) == 0)
            for h in range(GLA_HEADS):
                dk = slice(h * GLA_DK, (h + 1) * GLA_DK)
                scores[h] = jnp.where(pair, _dot_nt(q_l[:, dk], k_l[:, dk]), scores[h])
            half //= 2
        return scores

    @pl.when(fast)
    def _():
        states = [st_ref[p] for p in range(GLA_HEADS // 2)]
        for s in range(GLA_NSUB):
            states = chunk(slice(s * c, (s + 1) * c), states, fast_scores)
        for p in range(GLA_HEADS // 2):
            st_ref[p] = states[p]

    @pl.when(jnp.logical_not(fast))
    def _():
        def body(s, carry):
            rows = pl.ds(pl.multiple_of(s * c, c), c)
            scores = safe_scores(rows)
            states = chunk(rows, [st_ref[p] for p in range(GLA_HEADS // 2)],
                           lambda p, *_: scores[2 * p:2 * p + 2])
            for p in range(GLA_HEADS // 2):
                st_ref[p] = states[p]
            return carry

        lax.fori_loop(0, GLA_NSUB, body, 0)


def _gla_call(q, k, la, v, sg, gn):
    s = q.shape[0]
    row = lambda i: (i, 0)
    const = lambda i: (0, 0)
    return pl.pallas_call(
        _gla_kernel,
        grid=(s // TM_GLA,),
        in_specs=[
            pl.BlockSpec((TM_GLA, GLA_KEY), row),
            pl.BlockSpec((TM_GLA, GLA_KEY), row),
            pl.BlockSpec((TM_GLA, GLA_KEY), row),
            pl.BlockSpec((TM_GLA, GLA_VAL), row),
            pl.BlockSpec((TM_GLA, GLA_VAL), row),
            pl.BlockSpec((1, GLA_DV), const),
        ],
        out_specs=pl.BlockSpec((TM_GLA, GLA_VAL), row),
        out_shape=jax.ShapeDtypeStruct((s, GLA_VAL), BF16),
        scratch_shapes=[
            pltpu.VMEM((GLA_HEADS // 2, 2 * GLA_DV, LANES), F32),
            pltpu.VMEM((TM_GLA, GLA_KEY), F32),
        ],
        compiler_params=pltpu.CompilerParams(
            dimension_semantics=("arbitrary",), vmem_limit_bytes=VMEM_LIMIT),
        name="gla",
    )(q, k, la, v, sg, gn)


def _group_mean(t, low_half, size):
    lo = jnp.sum(jnp.where(low_half, t, 0.0), axis=-1, keepdims=True)
    hi = jnp.sum(jnp.where(low_half, 0.0, t), axis=-1, keepdims=True)
    return jnp.where(low_half, lo, hi) * (1.0 / size)


def _interleave(mxu_items, vec_items):
    done = 0
    for i, item in enumerate(mxu_items):
        item()
        upto = (i + 1) * len(vec_items) // len(mxu_items)
        for v_item in vec_items[done:upto]:
            v_item()
        done = upto


def _mlp_conv_kernel(x_ref, og_ref, un_ref, u0_ref, cw_ref, cb_ref, gg_ref, gb_ref,
                     wo_ref, g2_ref, w1_ref, w2_ref, gf_ref,
                     o_ref, ext_ref, sh_ref, oc_ref, f_ref):
    tm = TM_MLP
    grp = CONV_CH // CONV_GROUPS
    low_half = lax.broadcasted_iota(jnp.int32, (CONV_RB, LANES), 1) < grp
    first = CONV_HALO - (CONV_W - 1)

    def shifted_copy(p):
        sh_ref[p - 1] = ext_ref[p:p + CONV_SH_ROWS, :]

    def conv_block(base):
        for cg in range(CONV_CH // LANES):
            cols = slice(cg * LANES, (cg + 1) * LANES)
            acc = jnp.broadcast_to(cb_ref[:, cols], (CONV_RB, LANES))
            for t in range(CONV_W):
                shift = first + t
                p = shift % SUBLANES
                rows = pl.ds(base + (shift - p), CONV_RB)
                tap = ext_ref[rows, cols] if p == 0 else sh_ref[p - 1, rows, cols]
                acc = acc + cw_ref[t:t + 1, cols] * tap
            d = acc - _group_mean(acc, low_half, grp)
            var = _group_mean(d * d, low_half, grp)
            yn = d * lax.rsqrt(var + EPS) * gg_ref[:, cols] + gb_ref[:, cols]
            oc_ref[pl.ds(base, CONV_RB), cols] = (yn * _sigmoid(yn)).astype(BF16)

    @pl.when(pl.program_id(0) == 0)
    def _():
        ext_ref[0:CONV_HALO, :] = jnp.zeros((CONV_HALO, CONV_CH), F32)
        ext_ref[CONV_HALO:, :] = u0_ref[...]
        for p in range(1, SUBLANES):
            shifted_copy(p)

        def body(r, carry):
            conv_block(pl.multiple_of(r * CONV_RB, CONV_RB))
            return carry

        lax.fori_loop(0, tm // CONV_RB, body, 0)

    h = x_ref[...] + _dot(og_ref[...], wo_ref[0:GLA_VAL, :]) + _dot(oc_ref[...], wo_ref[GLA_VAL:, :])
    ms = jnp.mean(h * h, axis=-1, keepdims=True)
    hn = (h * lax.rsqrt(ms + EPS) * g2_ref[...]).astype(BF16)

    def roll_window():
        ext_ref[0:CONV_HALO, :] = ext_ref[tm:tm + CONV_HALO, :]
        ext_ref[CONV_HALO:, :] = un_ref[...]

    vec_items = [roll_window] + [functools.partial(shifted_copy, p) for p in range(1, SUBLANES)]
    vec_items += [functools.partial(conv_block, r * CONV_RB) for r in range(tm // CONV_RB)]

    def ff_in(j):
        cols = slice(j * FF_CHUNK, (j + 1) * FF_CHUNK)
        f = jnp.maximum(_dot(hn, w1_ref[:, cols]), 0.0)
        f_ref[:, cols] = (f * f).astype(BF16)

    def ff_out(j):
        cols = slice(j * OUT_CHUNK, (j + 1) * OUT_CHUNK)
        o_ref[:, cols] = h[:, cols] + _dot(f_ref[...], w2_ref[:, cols])

    mxu_items = [functools.partial(ff_in, j) for j in range(D_FF // FF_CHUNK)]
    mxu_items += [functools.partial(ff_out, j) for j in range(D_MODEL // OUT_CHUNK)]
    _interleave(mxu_items, vec_items)

    acc = o_ref[...]
    ms2 = jnp.mean(acc * acc, axis=-1, keepdims=True)
    o_ref[...] = acc * lax.rsqrt(ms2 + EPS) * gf_ref[...]


def _mlp_conv_call(x2, og, u, cw, cb, gg, gb, wo, g2, w1, w2, gf):
    s = x2.shape[0]
    n = s // TM_MLP
    row = lambda i: (i, 0)
    nxt = lambda i: (jnp.minimum(i + 1, n - 1), 0)
    const = lambda i: (0, 0)
    single = pl.Buffered(1)
    return pl.pallas_call(
        _mlp_conv_kernel,
        grid=(n,),
        in_specs=[
            pl.BlockSpec((TM_MLP, D_MODEL), row),
            pl.BlockSpec((TM_MLP, GLA_VAL), row),
            pl.BlockSpec((TM_MLP, CONV_CH), nxt),
            pl.BlockSpec((TM_MLP, CONV_CH), const, pipeline_mode=single),
            pl.BlockSpec((CONV_W, CONV_CH), const),
            pl.BlockSpec((1, CONV_CH), const),
            pl.BlockSpec((1, CONV_CH), const),
            pl.BlockSpec((1, CONV_CH), const),
            pl.BlockSpec((D_MODEL, D_MODEL), const, pipeline_mode=single),
            pl.BlockSpec((1, D_MODEL), const),
            pl.BlockSpec((D_MODEL, D_FF), const, pipeline_mode=single),
            pl.BlockSpec((D_FF, D_MODEL), const, pipeline_mode=single),
            pl.BlockSpec((1, D_MODEL), const),
        ],
        out_specs=pl.BlockSpec((TM_MLP, D_MODEL), row),
        out_shape=jax.ShapeDtypeStruct((s, D_MODEL), F32),
        scratch_shapes=[
            pltpu.VMEM((TM_MLP + CONV_HALO, CONV_CH), F32),
            pltpu.VMEM((SUBLANES - 1, CONV_SH_ROWS, CONV_CH), F32),
            pltpu.VMEM((TM_MLP, CONV_CH), BF16),
            pltpu.VMEM((TM_MLP, D_FF), BF16),
        ],
        compiler_params=pltpu.CompilerParams(
            dimension_semantics=("arbitrary",), vmem_limit_bytes=VMEM_LIMIT),
        name="mlp_conv",
    )(x2, og, u, u, cw, cb, gg, gb, wo, g2, w1, w2, gf)


def kernel(x, norm1_g, w_in, w_gate_up, b_gate, gla_norm_g, conv_w, conv_b, conv_norm_g,
           conv_norm_b, w_out, norm2_g, w_mlp_in, w_mlp_out, final_norm_g):
    bsz, seq, _ = x.shape
    x2 = x.reshape(bsz * seq, D_MODEL)
    assert bsz == 1, "state / halo carry across grid steps assumes one sequence"
    l = 0
    wi = w_in[l].astype(BF16)
    w_p = jnp.concatenate(
        [wi[:, OFF_Q:OFF_A], wi[:, OFF_C:D_IN], wi[:, OFF_A:OFF_C],
         jnp.zeros((D_MODEL, Z_PAD - GATE_RANK), BF16)], axis=1)
    wg_p = jnp.concatenate(
        [w_gate_up[l].astype(BF16), jnp.zeros((Z_PAD - GATE_RANK, GLA_KEY), BF16)], axis=0)
    q, k, la, v, sg, u = _proj_call(x2, norm1_g[l][None, :], w_p, wg_p, b_gate[l][None, :])
    o_gla = _gla_call(q, k, la, v, sg, gla_norm_g[l][None, :])
    out = _mlp_conv_call(x2, o_gla, u, conv_w[l], conv_b[l][None, :], conv_norm_g[l][None, :],
                         conv_norm_b[l][None, :], w_out[l].astype(BF16), norm2_g[l][None, :],
                         w_mlp_in[l].astype(BF16), w_mlp_out[l].astype(BF16),
                         final_norm_g[None, :])
    return out.reshape(bsz, seq, D_MODEL)
```

```python
import functools

import jax
import jax.numpy as jnp
from jax import lax
from jax.experimental import pallas as pl
from jax.experimental.pallas import tpu as pltpu

D_MODEL = 1024
GLA_HEADS = 4
GLA_DV = 128
GLA_DK = 64
GLA_KEY = GLA_HEADS * GLA_DK
GLA_VAL = GLA_HEADS * GLA_DV
GATE_RANK = 16
GATE_TAU = 16.0
CONV_CH = 512
CONV_GROUPS = 8
CONV_W = 31
D_FF = 4 * D_MODEL
EPS = 1e-6

OFF_Q = 0
OFF_K = OFF_Q + GLA_KEY
OFF_V = OFF_K + GLA_KEY
OFF_G = OFF_V + GLA_VAL
OFF_A = OFF_G + GLA_VAL
OFF_C = OFF_A + GATE_RANK
D_IN = OFF_C + 2 * CONV_CH

LANES = 128
SUBLANES = 8
Z_PAD = LANES
P_Q = 0
P_K = P_Q + GLA_KEY
P_V = P_K + GLA_KEY
P_G = P_V + GLA_VAL
P_CI = P_G + GLA_VAL
P_CG = P_CI + CONV_CH
P_Z = P_CG + CONV_CH
D_INP = P_Z + Z_PAD

TM_PROJ = 512
GLA_SUB = 256
GLA_NSUB = 8
TM_GLA = GLA_SUB * GLA_NSUB
TM_MLP = 512
FF_CHUNK = 512
OUT_CHUNK = 256
CONV_HALO = 32
CONV_RB = 32
CONV_SH_ROWS = TM_MLP + CONV_HALO - SUBLANES
GLA_FAST_MIN_B = -60.0

VMEM_LIMIT = 56 * 1024 * 1024

F32 = jnp.float32
BF16 = jnp.bfloat16


def _sigmoid(x):
    return 1.0 / (1.0 + jnp.exp(-x))


def _split_bf16(x):
    hi = x.astype(BF16)
    lo = (x - hi.astype(F32)).astype(BF16)
    return hi, lo


def _dot(a, b):
    return jnp.dot(a, b, preferred_element_type=F32)


def _dot_nt(a, b):
    return lax.dot_general(a, b, (((1,), (1,)), ((), ())), preferred_element_type=F32)


def _dot_tn(a, b):
    return lax.dot_general(a, b, (((0,), (0,)), ((), ())), preferred_element_type=F32)


def _proj_kernel(x_ref, g1_ref, w_ref, wg_ref, bg_ref,
                 q_ref, k_ref, la_ref, v_ref, sg_ref, u_ref, wb_ref, wgb_ref):
    @pl.when(pl.program_id(0) == 0)
    def _():
        wb_ref[:, P_Q:P_CI] = w_ref[:, OFF_Q:OFF_A].astype(BF16)
        wb_ref[:, P_CI:P_Z] = w_ref[:, OFF_C:D_IN].astype(BF16)
        lane = lax.broadcasted_iota(jnp.int32, (D_MODEL, Z_PAD), 1)
        z_cols = w_ref[:, OFF_A:OFF_A + Z_PAD]
        wb_ref[:, P_Z:D_INP] = jnp.where(lane < GATE_RANK, z_cols, 0.0).astype(BF16)
        wgb_ref[...] = jnp.zeros_like(wgb_ref)
        wgb_ref[0:GATE_RANK, :] = wg_ref[...].astype(BF16)

    x = x_ref[...]
    ms = jnp.mean(x * x, axis=-1, keepdims=True)
    xn = (x * lax.rsqrt(ms + EPS) * g1_ref[...]).astype(BF16)
    proj = _dot(xn, wb_ref[...])
    q_ref[...] = proj[:, P_Q:P_K] * (GLA_DK ** -0.5)
    k_ref[...] = proj[:, P_K:P_V]
    v_ref[...] = proj[:, P_V:P_G].astype(BF16)
    g = proj[:, P_G:P_CI]
    sg_ref[...] = (g * _sigmoid(g)).astype(BF16)
    u_ref[...] = proj[:, P_CI:P_CG] * _sigmoid(proj[:, P_CG:P_Z])
    z = proj[:, P_Z:D_INP].astype(BF16)
    a_logit = _dot(z, wgb_ref[...]) + bg_ref[...]
    la = jnp.minimum(a_logit, 0.0) - jnp.log(1.0 + jnp.exp(-jnp.abs(a_logit)))
    la_ref[...] = la * (1.0 / GATE_TAU)


def _proj_call(x2, g1, w_in, w_gate_up, bg):
    s = x2.shape[0]
    row = lambda i: (i, 0)
    const = lambda i: (0, 0)
    return pl.pallas_call(
        _proj_kernel,
        grid=(s // TM_PROJ,),
        in_specs=[
            pl.BlockSpec((TM_PROJ, D_MODEL), row),
            pl.BlockSpec((1, D_MODEL), const),
            pl.BlockSpec((D_MODEL, D_IN), const, pipeline_mode=pl.Buffered(1)),
            pl.BlockSpec((GATE_RANK, GLA_KEY), const),
            pl.BlockSpec((1, GLA_KEY), const),
        ],
        out_specs=[
            pl.BlockSpec((TM_PROJ, GLA_KEY), row),
            pl.BlockSpec((TM_PROJ, GLA_KEY), row),
            pl.BlockSpec((TM_PROJ, GLA_KEY), row),
            pl.BlockSpec((TM_PROJ, GLA_VAL), row),
            pl.BlockSpec((TM_PROJ, GLA_VAL), row),
            pl.BlockSpec((TM_PROJ, CONV_CH), row),
        ],
        out_shape=[
            jax.ShapeDtypeStruct((s, GLA_KEY), F32),
            jax.ShapeDtypeStruct((s, GLA_KEY), F32),
            jax.ShapeDtypeStruct((s, GLA_KEY), F32),
            jax.ShapeDtypeStruct((s, GLA_VAL), BF16),
            jax.ShapeDtypeStruct((s, GLA_VAL), BF16),
            jax.ShapeDtypeStruct((s, CONV_CH), F32),
        ],
        scratch_shapes=[
            pltpu.VMEM((D_MODEL, D_INP), BF16),
            pltpu.VMEM((Z_PAD, GLA_KEY), BF16),
        ],
        compiler_params=pltpu.CompilerParams(
            dimension_semantics=("arbitrary",), vmem_limit_bytes=VMEM_LIMIT),
        name="proj",
    )(x2, g1, w_in, w_gate_up, bg)


def _gla_kernel(q_ref, k_ref, la_ref, v_ref, sg_ref, gn_ref, wo_ref, w1_ref, w2_ref,
                o_ref, wob_ref, w1b_ref, w2b_ref, st_ref, b_ref):
    c = GLA_SUB

    wob_ref[...] = wo_ref[...].astype(BF16)
    w1b_ref[...] = w1_ref[...].astype(BF16)
    w2b_ref[...] = w2_ref[...].astype(BF16)

    @pl.when(pl.program_id(0) == 0)
    def _():
        st_ref[...] = jnp.zeros_like(st_ref)

    ri = lax.broadcasted_iota(jnp.int32, (c, c), 0)
    ci = lax.broadcasted_iota(jnp.int32, (c, c), 1)
    first_head = lax.broadcasted_iota(jnp.int32, (c, LANES), 1) < GLA_DK
    st_row = lax.broadcasted_iota(jnp.int32, (2 * GLA_DV, LANES), 0)
    st_lane = lax.broadcasted_iota(jnp.int32, (2 * GLA_DV, LANES), 1)
    same_head = (st_row < GLA_DV) == (st_lane < GLA_DK)

    def masked_sum(m, rows):
        mb = m.astype(BF16)
        la_hi, la_lo = _split_bf16(la_ref[rows, :])
        return _dot(mb, la_hi) + _dot(mb, la_lo)

    b_min = None
    for s in range(GLA_NSUB):
        rows = slice(s * c, (s + 1) * c)
        b = masked_sum(ci <= ri, rows)
        b_ref[rows, :] = b
        b_min = b[c - 1:c, :] if b_min is None else jnp.minimum(b_min, b[c - 1:c, :])
    fast = jnp.min(b_min) > GLA_FAST_MIN_B

    def chunk(rows, states, scores_fn):
        b = b_ref[rows, :]
        b_last = b[c - 1:c, :]
        q = q_ref[rows, :]
        k = k_ref[rows, :]
        q_in = (q * jnp.exp(b)).astype(BF16)
        k_out = (k * jnp.exp(b_last - b)).astype(BF16)
        a_chunk = jnp.exp(b_last)
        new_states = []
        for p in range(GLA_HEADS // 2):
            kc = slice(p * LANES, (p + 1) * LANES)
            v_p = v_ref[rows, 2 * p * GLA_DV:2 * (p + 1) * GLA_DV]
            st = states[p]
            o_inter = _dot_nt(q_in[:, kc], st.astype(BF16))
            for j, s_h in enumerate(scores_fn(p, q, k, b, q_in)):
                h = 2 * p + j
                dv = slice(j * GLA_DV, (j + 1) * GLA_DV)
                o_h = _dot(s_h.astype(BF16), v_p[:, dv]) + o_inter[:, dv]
                ms = jnp.mean(o_h * o_h, axis=-1, keepdims=True)
                y = o_h * lax.rsqrt(ms + EPS) * gn_ref[...]
                cols = slice(h * GLA_DV, (h + 1) * GLA_DV)
                o_ref[rows, cols] = (y * sg_ref[rows, cols].astype(F32)).astype(BF16)
            upd = _dot_tn(v_p, k_out[:, kc])
            new_states.append(a_chunk[:, kc] * st + jnp.where(same_head, upd, 0.0))
        return new_states

    def fast_scores(p, q, k, b, q_in):
        kc = slice(p * LANES, (p + 1) * LANES)
        k_div = k[:, kc] * jnp.exp(-b[:, kc])
        k_blk = jnp.concatenate([jnp.where(first_head, k_div, 0.0),
                                 jnp.where(first_head, 0.0, k_div)], axis=0).astype(BF16)
        s2 = _dot_nt(q_in[:, kc], k_blk)
        return [jnp.where(ri >= ci, s2[:, j * c:(j + 1) * c], 0.0) for j in range(2)]

    def safe_scores(rows):
        q = q_ref[rows, :]
        k = k_ref[rows, :]
        kb = k.astype(BF16)
        qb = q.astype(BF16)
        scores = []
        for h in range(GLA_HEADS):
            dk = slice(h * GLA_DK, (h + 1) * GLA_DK)
            scores.append(jnp.where(ri == ci, _dot_nt(qb[:, dk], kb[:, dk]), 0.0))
        half = c // 2
        while half >= 1:
            ref = (ri & (-2 * half)) + (half - 1)
            lo = jnp.minimum(ri, ref)
            hi = jnp.maximum(ri, ref)
            e = jnp.exp(masked_sum((ci > lo) & (ci <= hi), rows))
            q_l = (q * e).astype(BF16)
            k_l = (k * e).astype(BF16)
            pair = (((ri ^ ci) & (-2 * half)) == 0) & ((ri & half) != 0) & ((ci & half) == 0)
            for h in range(GLA_HEADS):
                dk = slice(h * GLA_DK, (h + 1) * GLA_DK)
                scores[h] = jnp.where(pair, _dot_nt(q_l[:, dk], k_l[:, dk]), scores[h])
            half //= 2
        return scores

    @pl.when(fast)
    def _():
        states = [st_ref[p] for p in range(GLA_HEADS // 2)]
        for s in range(GLA_NSUB):
            states = chunk(slice(s * c, (s + 1) * c), states, fast_scores)
        for p in range(GLA_HEADS // 2):
            st_ref[p] = states[p]

    @pl.when(jnp.logical_not(fast))
    def _():
        def body(s, carry):
            rows = pl.ds(pl.multiple_of(s * c, c), c)
            scores = safe_scores(rows)
            states = chunk(rows, [st_ref[p] for p in range(GLA_HEADS // 2)],
                           lambda p, *_: scores[2 * p:2 * p + 2])
            for p in range(GLA_HEADS // 2):
                st_ref[p] = states[p]
            return carry

        lax.fori_loop(0, GLA_NSUB, body, 0)


def _gla_call(q, k, la, v, sg, gn, wo, w1, w2):
    s = q.shape[0]
    n = s // TM_GLA
    row = lambda i: (i, 0)
    col = lambda i: (0, i)
    const = lambda i: (0, 0)
    return pl.pallas_call(
        _gla_kernel,
        grid=(n,),
        in_specs=[
            pl.BlockSpec((TM_GLA, GLA_KEY), row),
            pl.BlockSpec((TM_GLA, GLA_KEY), row),
            pl.BlockSpec((TM_GLA, GLA_KEY), row),
            pl.BlockSpec((TM_GLA, GLA_VAL), row),
            pl.BlockSpec((TM_GLA, GLA_VAL), row),
            pl.BlockSpec((1, GLA_DV), const),
            pl.BlockSpec((D_MODEL // n, D_MODEL), row),
            pl.BlockSpec((D_MODEL, D_FF // n), col),
            pl.BlockSpec((D_FF // n, D_MODEL), row),
        ],
        out_specs=[
            pl.BlockSpec((TM_GLA, GLA_VAL), row),
            pl.BlockSpec((D_MODEL // n, D_MODEL), row),
            pl.BlockSpec((D_MODEL, D_FF // n), col),
            pl.BlockSpec((D_FF // n, D_MODEL), row),
        ],
        out_shape=[
            jax.ShapeDtypeStruct((s, GLA_VAL), BF16),
            jax.ShapeDtypeStruct((D_MODEL, D_MODEL), BF16),
            jax.ShapeDtypeStruct((D_MODEL, D_FF), BF16),
            jax.ShapeDtypeStruct((D_FF, D_MODEL), BF16),
        ],
        scratch_shapes=[
            pltpu.VMEM((GLA_HEADS // 2, 2 * GLA_DV, LANES), F32),
            pltpu.VMEM((TM_GLA, GLA_KEY), F32),
        ],
        compiler_params=pltpu.CompilerParams(
            dimension_semantics=("arbitrary",), vmem_limit_bytes=VMEM_LIMIT),
        name="gla",
    )(q, k, la, v, sg, gn, wo, w1, w2)


def _group_mean(t, low_half, size):
    lo = jnp.sum(jnp.where(low_half, t, 0.0), axis=-1, keepdims=True)
    hi = jnp.sum(jnp.where(low_half, 0.0, t), axis=-1, keepdims=True)
    return jnp.where(low_half, lo, hi) * (1.0 / size)


def _interleave(mxu_items, vec_items):
    done = 0
    for i, item in enumerate(mxu_items):
        item()
        upto = (i + 1) * len(vec_items) // len(mxu_items)
        for v_item in vec_items[done:upto]:
            v_item()
        done = upto


def _mlp_conv_kernel(x_ref, og_ref, un_ref, u0_ref, cw_ref, cb_ref, gg_ref, gb_ref,
                     wo_ref, g2_ref, w1_ref, w2_ref, gf_ref,
                     o_ref, ext_ref, sh_ref, oc_ref, f_ref):
    tm = TM_MLP
    grp = CONV_CH // CONV_GROUPS
    low_half = lax.broadcasted_iota(jnp.int32, (CONV_RB, LANES), 1) < grp
    first = CONV_HALO - (CONV_W - 1)

    def shifted_copy(p):
        sh_ref[p - 1] = ext_ref[p:p + CONV_SH_ROWS, :]

    def conv_block(base):
        for cg in range(CONV_CH // LANES):
            cols = slice(cg * LANES, (cg + 1) * LANES)
            acc = jnp.broadcast_to(cb_ref[:, cols], (CONV_RB, LANES))
            for t in range(CONV_W):
                shift = first + t
                p = shift % SUBLANES
                rows = pl.ds(base + (shift - p), CONV_RB)
                tap = ext_ref[rows, cols] if p == 0 else sh_ref[p - 1, rows, cols]
                acc = acc + cw_ref[t:t + 1, cols] * tap
            d = acc - _group_mean(acc, low_half, grp)
            var = _group_mean(d * d, low_half, grp)
            yn = d * lax.rsqrt(var + EPS) * gg_ref[:, cols] + gb_ref[:, cols]
            oc_ref[pl.ds(base, CONV_RB), cols] = (yn * _sigmoid(yn)).astype(BF16)

    @pl.when(pl.program_id(0) == 0)
    def _():
        ext_ref[0:CONV_HALO, :] = jnp.zeros((CONV_HALO, CONV_CH), F32)
        ext_ref[CONV_HALO:, :] = u0_ref[...]
        for p in range(1, SUBLANES):
            shifted_copy(p)

        def body(r, carry):
            conv_block(pl.multiple_of(r * CONV_RB, CONV_RB))
            return carry

        lax.fori_loop(0, tm // CONV_RB, body, 0)

    h = x_ref[...] + _dot(og_ref[...], wo_ref[0:GLA_VAL, :]) + _dot(oc_ref[...], wo_ref[GLA_VAL:, :])
    ms = jnp.mean(h * h, axis=-1, keepdims=True)
    hn = (h * lax.rsqrt(ms + EPS) * g2_ref[...]).astype(BF16)

    def roll_window():
        ext_ref[0:CONV_HALO, :] = ext_ref[tm:tm + CONV_HALO, :]
        ext_ref[CONV_HALO:, :] = un_ref[...]

    vec_items = [roll_window] + [functools.partial(shifted_copy, p) for p in range(1, SUBLANES)]
    vec_items += [functools.partial(conv_block, r * CONV_RB) for r in range(tm // CONV_RB)]

    def ff_in(j):
        cols = slice(j * FF_CHUNK, (j + 1) * FF_CHUNK)
        f = jnp.maximum(_dot(hn, w1_ref[:, cols]), 0.0)
        f_ref[:, cols] = (f * f).astype(BF16)

    def ff_out(j):
        cols = slice(j * OUT_CHUNK, (j + 1) * OUT_CHUNK)
        o_ref[:, cols] = h[:, cols] + _dot(f_ref[...], w2_ref[:, cols])

    mxu_items = [functools.partial(ff_in, j) for j in range(D_FF // FF_CHUNK)]
    mxu_items += [functools.partial(ff_out, j) for j in range(D_MODEL // OUT_CHUNK)]
    _interleave(mxu_items, vec_items)

    acc = o_ref[...]
    ms2 = jnp.mean(acc * acc, axis=-1, keepdims=True)
    o_ref[...] = acc * lax.rsqrt(ms2 + EPS) * gf_ref[...]


def _mlp_conv_call(x2, og, u, cw, cb, gg, gb, wo, g2, w1, w2, gf):
    s = x2.shape[0]
    n = s // TM_MLP
    row = lambda i: (i, 0)
    nxt = lambda i: (jnp.minimum(i + 1, n - 1), 0)
    const = lambda i: (0, 0)
    single = pl.Buffered(1)
    return pl.pallas_call(
        _mlp_conv_kernel,
        grid=(n,),
        in_specs=[
            pl.BlockSpec((TM_MLP, D_MODEL), row),
            pl.BlockSpec((TM_MLP, GLA_VAL), row),
            pl.BlockSpec((TM_MLP, CONV_CH), nxt),
            pl.BlockSpec((TM_MLP, CONV_CH), const, pipeline_mode=single),
            pl.BlockSpec((CONV_W, CONV_CH), const),
            pl.BlockSpec((1, CONV_CH), const),
            pl.BlockSpec((1, CONV_CH), const),
            pl.BlockSpec((1, CONV_CH), const),
            pl.BlockSpec((D_MODEL, D_MODEL), const, pipeline_mode=single),
            pl.BlockSpec((1, D_MODEL), const),
            pl.BlockSpec((D_MODEL, D_FF), const, pipeline_mode=single),
            pl.BlockSpec((D_FF, D_MODEL), const, pipeline_mode=single),
            pl.BlockSpec((1, D_MODEL), const),
        ],
        out_specs=pl.BlockSpec((TM_MLP, D_MODEL), row),
        out_shape=jax.ShapeDtypeStruct((s, D_MODEL), F32),
        scratch_shapes=[
            pltpu.VMEM((TM_MLP + CONV_HALO, CONV_CH), F32),
            pltpu.VMEM((SUBLANES - 1, CONV_SH_ROWS, CONV_CH), F32),
            pltpu.VMEM((TM_MLP, CONV_CH), BF16),
            pltpu.VMEM((TM_MLP, D_FF), BF16),
        ],
        compiler_params=pltpu.CompilerParams(
            dimension_semantics=("arbitrary",), vmem_limit_bytes=VMEM_LIMIT),
        name="mlp_conv",
    )(x2, og, u, u, cw, cb, gg, gb, wo, g2, w1, w2, gf)


def kernel(x, norm1_g, w_in, w_gate_up, b_gate, gla_norm_g, conv_w, conv_b, conv_norm_g,
           conv_norm_b, w_out, norm2_g, w_mlp_in, w_mlp_out, final_norm_g):
    bsz, seq, _ = x.shape
    x2 = x.reshape(bsz * seq, D_MODEL)
    assert bsz == 1, "state / halo carry across grid steps assumes one sequence"
    l = 0
    q, k, la, v, sg, u = _proj_call(x2, norm1_g[l][None, :], w_in[l], w_gate_up[l],
                                    b_gate[l][None, :])
    o_gla, wo_b, w1_b, w2_b = _gla_call(q, k, la, v, sg, gla_norm_g[l][None, :],
                                        w_out[l], w_mlp_in[l], w_mlp_out[l])
    out = _mlp_conv_call(x2, o_gla, u, conv_w[l], conv_b[l][None, :], conv_norm_g[l][None, :],
                         conv_norm_b[l][None, :], wo_b, norm2_g[l][None, :], w1_b, w2_b,
                         final_norm_g[None, :])
    return out.reshape(bsz, seq, D_MODEL)
```

```python
import functools

import jax
import jax.numpy as jnp
from jax import lax
from jax.experimental import pallas as pl
from jax.experimental.pallas import tpu as pltpu

D_MODEL = 1024
GLA_HEADS = 4
GLA_DV = 128
GLA_DK = 64
GLA_KEY = GLA_HEADS * GLA_DK
GLA_VAL = GLA_HEADS * GLA_DV
GATE_RANK = 16
GATE_TAU = 16.0
CONV_CH = 512
CONV_GROUPS = 8
CONV_W = 31
D_FF = 4 * D_MODEL
EPS = 1e-6

OFF_Q = 0
OFF_K = OFF_Q + GLA_KEY
OFF_V = OFF_K + GLA_KEY
OFF_G = OFF_V + GLA_VAL
OFF_A = OFF_G + GLA_VAL
OFF_C = OFF_A + GATE_RANK
D_IN = OFF_C + 2 * CONV_CH

LANES = 128
SUBLANES = 8
Z_PAD = LANES
P_Q = 0
P_K = P_Q + GLA_KEY
P_V = P_K + GLA_KEY
P_G = P_V + GLA_VAL
P_CI = P_G + GLA_VAL
P_CG = P_CI + CONV_CH
P_Z = P_CG + CONV_CH
D_INP = P_Z + Z_PAD

TM_PROJ = 512
GLA_SUB = 256
GLA_NSUB = 8
TM_GLA = GLA_SUB * GLA_NSUB
TM_MLP = 512
FF_CHUNK = 512
OUT_CHUNK = 256
CONV_HALO = 32
CONV_RB = 32
CONV_SH_ROWS = TM_MLP + CONV_HALO - SUBLANES
GLA_FAST_MIN_B = -60.0

VMEM_LIMIT = 56 * 1024 * 1024

F32 = jnp.float32
BF16 = jnp.bfloat16


def _sigmoid(x):
    return 1.0 / (1.0 + jnp.exp(-x))


def _split_bf16(x):
    hi = x.astype(BF16)
    lo = (x - hi.astype(F32)).astype(BF16)
    return hi, lo


def _dot(a, b):
    return jnp.dot(a, b, preferred_element_type=F32)


def _dot_nt(a, b):
    return lax.dot_general(a, b, (((1,), (1,)), ((), ())), preferred_element_type=F32)


def _dot_tn(a, b):
    return lax.dot_general(a, b, (((0,), (0,)), ((), ())), preferred_element_type=F32)


def _proj_kernel(x_ref, g1_ref, w_ref, wg_ref, bg_ref, wo_ref, w1_ref,
                 q_ref, k_ref, la_ref, v_ref, sg_ref, u_ref, wob_ref, w1b_ref, wb_ref, wgb_ref):
    wob_ref[...] = wo_ref[...].astype(BF16)
    w1b_ref[...] = w1_ref[...].astype(BF16)

    @pl.when(pl.program_id(0) == 0)
    def _():
        wb_ref[:, P_Q:P_CI] = w_ref[:, OFF_Q:OFF_A].astype(BF16)
        wb_ref[:, P_CI:P_Z] = w_ref[:, OFF_C:D_IN].astype(BF16)
        lane = lax.broadcasted_iota(jnp.int32, (D_MODEL, Z_PAD), 1)
        z_cols = w_ref[:, OFF_A:OFF_A + Z_PAD]
        wb_ref[:, P_Z:D_INP] = jnp.where(lane < GATE_RANK, z_cols, 0.0).astype(BF16)
        wgb_ref[...] = jnp.zeros_like(wgb_ref)
        wgb_ref[0:GATE_RANK, :] = wg_ref[...].astype(BF16)

    x = x_ref[...]
    ms = jnp.mean(x * x, axis=-1, keepdims=True)
    xn = (x * lax.rsqrt(ms + EPS) * g1_ref[...]).astype(BF16)
    proj = _dot(xn, wb_ref[...])
    q_ref[...] = proj[:, P_Q:P_K] * (GLA_DK ** -0.5)
    k_ref[...] = proj[:, P_K:P_V]
    v_ref[...] = proj[:, P_V:P_G].astype(BF16)
    g = proj[:, P_G:P_CI]
    sg_ref[...] = (g * _sigmoid(g)).astype(BF16)
    u_ref[...] = proj[:, P_CI:P_CG] * _sigmoid(proj[:, P_CG:P_Z])
    z = proj[:, P_Z:D_INP].astype(BF16)
    a_logit = _dot(z, wgb_ref[...]) + bg_ref[...]
    la = jnp.minimum(a_logit, 0.0) - jnp.log(1.0 + jnp.exp(-jnp.abs(a_logit)))
    la_ref[...] = la * (1.0 / GATE_TAU)


def _proj_call(x2, g1, w_in, w_gate_up, bg, wo, w1):
    s = x2.shape[0]
    n = s // TM_PROJ
    row = lambda i: (i, 0)
    col = lambda i: (0, i)
    const = lambda i: (0, 0)
    return pl.pallas_call(
        _proj_kernel,
        grid=(n,),
        in_specs=[
            pl.BlockSpec((TM_PROJ, D_MODEL), row),
            pl.BlockSpec((1, D_MODEL), const),
            pl.BlockSpec((None, D_MODEL, D_IN), lambda i: (0, 0, 0), pipeline_mode=pl.Buffered(1)),
            pl.BlockSpec((GATE_RANK, GLA_KEY), const),
            pl.BlockSpec((1, GLA_KEY), const),
            pl.BlockSpec((D_MODEL // n, D_MODEL), row),
            pl.BlockSpec((D_MODEL, D_FF // n), col),
        ],
        out_specs=[
            pl.BlockSpec((TM_PROJ, GLA_KEY), row),
            pl.BlockSpec((TM_PROJ, GLA_KEY), row),
            pl.BlockSpec((TM_PROJ, GLA_KEY), row),
            pl.BlockSpec((TM_PROJ, GLA_VAL), row),
            pl.BlockSpec((TM_PROJ, GLA_VAL), row),
            pl.BlockSpec((TM_PROJ, CONV_CH), row),
            pl.BlockSpec((D_MODEL // n, D_MODEL), row),
            pl.BlockSpec((D_MODEL, D_FF // n), col),
        ],
        out_shape=[
            jax.ShapeDtypeStruct((s, GLA_KEY), F32),
            jax.ShapeDtypeStruct((s, GLA_KEY), F32),
            jax.ShapeDtypeStruct((s, GLA_KEY), F32),
            jax.ShapeDtypeStruct((s, GLA_VAL), BF16),
            jax.ShapeDtypeStruct((s, GLA_VAL), BF16),
            jax.ShapeDtypeStruct((s, CONV_CH), F32),
            jax.ShapeDtypeStruct((D_MODEL, D_MODEL), BF16),
            jax.ShapeDtypeStruct((D_MODEL, D_FF), BF16),
        ],
        scratch_shapes=[
            pltpu.VMEM((D_MODEL, D_INP), BF16),
            pltpu.VMEM((Z_PAD, GLA_KEY), BF16),
        ],
        compiler_params=pltpu.CompilerParams(
            dimension_semantics=("arbitrary",), vmem_limit_bytes=VMEM_LIMIT),
        name="proj",
    )(x2, g1, w_in, w_gate_up, bg, wo, w1)


def _gla_kernel(q_ref, k_ref, la_ref, v_ref, sg_ref, gn_ref, w2_ref,
                o_ref, w2b_ref, st_ref, b_ref):
    c = GLA_SUB

    w2b_ref[...] = w2_ref[...].astype(BF16)

    @pl.when(pl.program_id(0) == 0)
    def _():
        st_ref[...] = jnp.zeros_like(st_ref)

    ri = lax.broadcasted_iota(jnp.int32, (c, c), 0)
    ci = lax.broadcasted_iota(jnp.int32, (c, c), 1)
    first_head = lax.broadcasted_iota(jnp.int32, (c, LANES), 1) < GLA_DK
    st_row = lax.broadcasted_iota(jnp.int32, (2 * GLA_DV, LANES), 0)
    st_lane = lax.broadcasted_iota(jnp.int32, (2 * GLA_DV, LANES), 1)
    same_head = (st_row < GLA_DV) == (st_lane < GLA_DK)

    def masked_sum(m, rows):
        mb = m.astype(BF16)
        la_hi, la_lo = _split_bf16(la_ref[rows, :])
        return _dot(mb, la_hi) + _dot(mb, la_lo)

    b_min = None
    for s in range(GLA_NSUB):
        rows = slice(s * c, (s + 1) * c)
        b = masked_sum(ci <= ri, rows)
        b_ref[rows, :] = b
        b_min = b[c - 1:c, :] if b_min is None else jnp.minimum(b_min, b[c - 1:c, :])
    fast = jnp.min(b_min) > GLA_FAST_MIN_B

    def chunk(rows, states, scores_fn):
        b = b_ref[rows, :]
        b_last = b[c - 1:c, :]
        q = q_ref[rows, :]
        k = k_ref[rows, :]
        q_in = (q * jnp.exp(b)).astype(BF16)
        k_out = (k * jnp.exp(b_last - b)).astype(BF16)
        a_chunk = jnp.exp(b_last)
        new_states = []
        for p in range(GLA_HEADS // 2):
            kc = slice(p * LANES, (p + 1) * LANES)
            v_p = v_ref[rows, 2 * p * GLA_DV:2 * (p + 1) * GLA_DV]
            st = states[p]
            o_inter = _dot_nt(q_in[:, kc], st.astype(BF16))
            for j, s_h in enumerate(scores_fn(p, q, k, b, q_in)):
                h = 2 * p + j
                dv = slice(j * GLA_DV, (j + 1) * GLA_DV)
                o_h = _dot(s_h.astype(BF16), v_p[:, dv]) + o_inter[:, dv]
                ms = jnp.mean(o_h * o_h, axis=-1, keepdims=True)
                y = o_h * lax.rsqrt(ms + EPS) * gn_ref[...]
                cols = slice(h * GLA_DV, (h + 1) * GLA_DV)
                o_ref[rows, cols] = (y * sg_ref[rows, cols].astype(F32)).astype(BF16)
            upd = _dot_tn(v_p, k_out[:, kc])
            new_states.append(a_chunk[:, kc] * st + jnp.where(same_head, upd, 0.0))
        return new_states

    def fast_scores(p, q, k, b, q_in):
        kc = slice(p * LANES, (p + 1) * LANES)
        k_div = k[:, kc] * jnp.exp(-b[:, kc])
        k_blk = jnp.concatenate([jnp.where(first_head, k_div, 0.0),
                                 jnp.where(first_head, 0.0, k_div)], axis=0).astype(BF16)
        s2 = _dot_nt(q_in[:, kc], k_blk)
        return [jnp.where(ri >= ci, s2[:, j * c:(j + 1) * c], 0.0) for j in range(2)]

    def safe_scores(rows):
        q = q_ref[rows, :]
        k = k_ref[rows, :]
        kb = k.astype(BF16)
        qb = q.astype(BF16)
        scores = []
        for h in range(GLA_HEADS):
            dk = slice(h * GLA_DK, (h + 1) * GLA_DK)
            scores.append(jnp.where(ri == ci, _dot_nt(qb[:, dk], kb[:, dk]), 0.0))
        half = c // 2
        while half >= 1:
            ref = (ri & (-2 * half)) + (half - 1)
            lo = jnp.minimum(ri, ref)
            hi = jnp.maximum(ri, ref)
            e = jnp.exp(masked_sum((ci > lo) & (ci <= hi), rows))
            q_l = (q * e).astype(BF16)
            k_l = (k * e).astype(BF16)
            pair = (((ri ^ ci) & (-2 * half)) == 0) & ((ri & half) != 0) & ((ci & half) == 0)
            for h in range(GLA_HEADS):
                dk = slice(h * GLA_DK, (h + 1) * GLA_DK)
                scores[h] = jnp.where(pair, _dot_nt(q_l[:, dk], k_l[:, dk]), scores[h])
            half //= 2
        return scores

    @pl.when(fast)
    def _():
        states = [st_ref[p] for p in range(GLA_HEADS // 2)]
        for s in range(GLA_NSUB):
            states = chunk(slice(s * c, (s + 1) * c), states, fast_scores)
        for p in range(GLA_HEADS // 2):
            st_ref[p] = states[p]

    @pl.when(jnp.logical_not(fast))
    def _():
        def body(s, carry):
            rows = pl.ds(pl.multiple_of(s * c, c), c)
            scores = safe_scores(rows)
            states = chunk(rows, [st_ref[p] for p in range(GLA_HEADS // 2)],
                           lambda p, *_: scores[2 * p:2 * p + 2])
            for p in range(GLA_HEADS // 2):
                st_ref[p] = states[p]
            return carry

        lax.fori_loop(0, GLA_NSUB, body, 0)


def _gla_call(q, k, la, v, sg, gn, w2):
    s = q.shape[0]
    n = s // TM_GLA
    row = lambda i: (i, 0)
    const = lambda i: (0, 0)
    return pl.pallas_call(
        _gla_kernel,
        grid=(n,),
        in_specs=[
            pl.BlockSpec((TM_GLA, GLA_KEY), row),
            pl.BlockSpec((TM_GLA, GLA_KEY), row),
            pl.BlockSpec((TM_GLA, GLA_KEY), row),
            pl.BlockSpec((TM_GLA, GLA_VAL), row),
            pl.BlockSpec((TM_GLA, GLA_VAL), row),
            pl.BlockSpec((1, GLA_DV), const),
            pl.BlockSpec((D_FF // n, D_MODEL), row),
        ],
        out_specs=[
            pl.BlockSpec((TM_GLA, GLA_VAL), row),
            pl.BlockSpec((D_FF // n, D_MODEL), row),
        ],
        out_shape=[
            jax.ShapeDtypeStruct((s, GLA_VAL), BF16),
            jax.ShapeDtypeStruct((D_FF, D_MODEL), BF16),
        ],
        scratch_shapes=[
            pltpu.VMEM((GLA_HEADS // 2, 2 * GLA_DV, LANES), F32),
            pltpu.VMEM((TM_GLA, GLA_KEY), F32),
        ],
        compiler_params=pltpu.CompilerParams(
            dimension_semantics=("arbitrary",), vmem_limit_bytes=VMEM_LIMIT),
        name="gla",
    )(q, k, la, v, sg, gn, w2)


def _group_mean(t, low_half, size):
    lo = jnp.sum(jnp.where(low_half, t, 0.0), axis=-1, keepdims=True)
    hi = jnp.sum(jnp.where(low_half, 0.0, t), axis=-1, keepdims=True)
    return jnp.where(low_half, lo, hi) * (1.0 / size)


def _interleave(mxu_items, vec_items):
    done = 0
    for i, item in enumerate(mxu_items):
        item()
        upto = (i + 1) * len(vec_items) // len(mxu_items)
        for v_item in vec_items[done:upto]:
            v_item()
        done = upto


def _mlp_conv_kernel(x_ref, og_ref, un_ref, u0_ref, cw_ref, cb_ref, gg_ref, gb_ref,
                     wo_ref, g2_ref, w1_ref, w2_ref, gf_ref,
                     o_ref, ext_ref, sh_ref, oc_ref, f_ref):
    tm = TM_MLP
    grp = CONV_CH // CONV_GROUPS
    low_half = lax.broadcasted_iota(jnp.int32, (CONV_RB, LANES), 1) < grp
    first = CONV_HALO - (CONV_W - 1)

    def shifted_copy(p):
        sh_ref[p - 1] = ext_ref[p:p + CONV_SH_ROWS, :]

    def conv_block(base):
        for cg in range(CONV_CH // LANES):
            cols = slice(cg * LANES, (cg + 1) * LANES)
            acc = jnp.broadcast_to(cb_ref[:, cols], (CONV_RB, LANES))
            for t in range(CONV_W):
                shift = first + t
                p = shift % SUBLANES
                rows = pl.ds(base + (shift - p), CONV_RB)
                tap = ext_ref[rows, cols] if p == 0 else sh_ref[p - 1, rows, cols]
                acc = acc + cw_ref[t:t + 1, cols] * tap
            d = acc - _group_mean(acc, low_half, grp)
            var = _group_mean(d * d, low_half, grp)
            yn = d * lax.rsqrt(var + EPS) * gg_ref[:, cols] + gb_ref[:, cols]
            oc_ref[pl.ds(base, CONV_RB), cols] = (yn * _sigmoid(yn)).astype(BF16)

    @pl.when(pl.program_id(0) == 0)
    def _():
        ext_ref[0:CONV_HALO, :] = jnp.zeros((CONV_HALO, CONV_CH), F32)
        ext_ref[CONV_HALO:, :] = u0_ref[...]
        for p in range(1, SUBLANES):
            shifted_copy(p)

        def body(r, carry):
            conv_block(pl.multiple_of(r * CONV_RB, CONV_RB))
            return carry

        lax.fori_loop(0, tm // CONV_RB, body, 0)

    h = x_ref[...] + _dot(og_ref[...], wo_ref[0:GLA_VAL, :]) + _dot(oc_ref[...], wo_ref[GLA_VAL:, :])
    ms = jnp.mean(h * h, axis=-1, keepdims=True)
    hn = (h * lax.rsqrt(ms + EPS) * g2_ref[...]).astype(BF16)

    def roll_window():
        ext_ref[0:CONV_HALO, :] = ext_ref[tm:tm + CONV_HALO, :]
        ext_ref[CONV_HALO:, :] = un_ref[...]

    vec_items = [roll_window] + [functools.partial(shifted_copy, p) for p in range(1, SUBLANES)]
    vec_items += [functools.partial(conv_block, r * CONV_RB) for r in range(tm // CONV_RB)]

    def ff_in(j):
        cols = slice(j * FF_CHUNK, (j + 1) * FF_CHUNK)
        f = jnp.maximum(_dot(hn, w1_ref[:, cols]), 0.0)
        f_ref[:, cols] = (f * f).astype(BF16)

    def ff_out(j):
        cols = slice(j * OUT_CHUNK, (j + 1) * OUT_CHUNK)
        o_ref[:, cols] = h[:, cols] + _dot(f_ref[...], w2_ref[:, cols])

    mxu_items = [functools.partial(ff_in, j) for j in range(D_FF // FF_CHUNK)]
    mxu_items += [functools.partial(ff_out, j) for j in range(D_MODEL // OUT_CHUNK)]
    _interleave(mxu_items, vec_items)

    acc = o_ref[...]
    ms2 = jnp.mean(acc * acc, axis=-1, keepdims=True)
    o_ref[...] = acc * lax.rsqrt(ms2 + EPS) * gf_ref[...]


def _mlp_conv_call(x2, og, u, cw, cb, gg, gb, wo, g2, w1, w2, gf):
    s = x2.shape[0]
    n = s // TM_MLP
    row = lambda i: (i, 0)
    nxt = lambda i: (jnp.minimum(i + 1, n - 1), 0)
    const = lambda i: (0, 0)
    single = pl.Buffered(1)
    return pl.pallas_call(
        _mlp_conv_kernel,
        grid=(n,),
        in_specs=[
            pl.BlockSpec((TM_MLP, D_MODEL), row),
            pl.BlockSpec((TM_MLP, GLA_VAL), row),
            pl.BlockSpec((TM_MLP, CONV_CH), nxt),
            pl.BlockSpec((TM_MLP, CONV_CH), const, pipeline_mode=single),
            pl.BlockSpec((CONV_W, CONV_CH), const),
            pl.BlockSpec((1, CONV_CH), const),
            pl.BlockSpec((1, CONV_CH), const),
            pl.BlockSpec((1, CONV_CH), const),
            pl.BlockSpec((D_MODEL, D_MODEL), const, pipeline_mode=single),
            pl.BlockSpec((1, D_MODEL), const),
            pl.BlockSpec((D_MODEL, D_FF), const, pipeline_mode=single),
            pl.BlockSpec((D_FF, D_MODEL), const, pipeline_mode=single),
            pl.BlockSpec((1, D_MODEL), const),
        ],
        out_specs=pl.BlockSpec((TM_MLP, D_MODEL), row),
        out_shape=jax.ShapeDtypeStruct((s, D_MODEL), F32),
        scratch_shapes=[
            pltpu.VMEM((TM_MLP + CONV_HALO, CONV_CH), F32),
            pltpu.VMEM((SUBLANES - 1, CONV_SH_ROWS, CONV_CH), F32),
            pltpu.VMEM((TM_MLP, CONV_CH), BF16),
            pltpu.VMEM((TM_MLP, D_FF), BF16),
        ],
        compiler_params=pltpu.CompilerParams(
            dimension_semantics=("arbitrary",), vmem_limit_bytes=VMEM_LIMIT),
        name="mlp_conv",
    )(x2, og, u, u, cw, cb, gg, gb, wo, g2, w1, w2, gf)


def kernel(x, norm1_g, w_in, w_gate_up, b_gate, gla_norm_g, conv_w, conv_b, conv_norm_g,
           conv_norm_b, w_out, norm2_g, w_mlp_in, w_mlp_out, final_norm_g):
    bsz, seq, _ = x.shape
    x2 = x.reshape(bsz * seq, D_MODEL)
    assert bsz == 1, "state / halo carry across grid steps assumes one sequence"
    l = 0
    assert w_in.shape[0] == 1, "single layer: the projection weight is passed with its layer axis"
    q, k, la, v, sg, u, wo_b, w1_b = _proj_call(
        x2, norm1_g[l][None, :], w_in, w_gate_up[l], b_gate[l][None, :], w_out[l], w_mlp_in[l])
    o_gla, w2_b = _gla_call(q, k, la, v, sg, gla_norm_g[l][None, :], w_mlp_out[l])
    out = _mlp_conv_call(x2, o_gla, u, conv_w[l], conv_b[l][None, :], conv_norm_g[l][None, :],
                         conv_norm_b[l][None, :], wo_b, norm2_g[l][None, :], w1_b, w2_b,
                         final_norm_g[None, :])
    return out.reshape(bsz, seq, D_MODEL)
```

```python
import functools

import jax
import jax.numpy as jnp
from jax import lax
from jax.experimental import pallas as pl
from jax.experimental.pallas import tpu as pltpu

D_MODEL = 1024
GLA_HEADS = 4
GLA_DV = 128
GLA_DK = 64
GLA_KEY = GLA_HEADS * GLA_DK
GLA_VAL = GLA_HEADS * GLA_DV
GATE_RANK = 16
GATE_TAU = 16.0
CONV_CH = 512
CONV_GROUPS = 8
CONV_W = 31
D_FF = 4 * D_MODEL
EPS = 1e-6

OFF_Q = 0
OFF_K = OFF_Q + GLA_KEY
OFF_V = OFF_K + GLA_KEY
OFF_G = OFF_V + GLA_VAL
OFF_A = OFF_G + GLA_VAL
OFF_C = OFF_A + GATE_RANK
D_IN = OFF_C + 2 * CONV_CH

LANES = 128
SUBLANES = 8
Z_PAD = LANES
P_Q = 0
P_K = P_Q + GLA_KEY
P_V = P_K + GLA_KEY
P_G = P_V + GLA_VAL
P_CI = P_G + GLA_VAL
P_CG = P_CI + CONV_CH
P_Z = P_CG + CONV_CH
D_INP = P_Z + Z_PAD

TM_PROJ = 512
GLA_SUB = 256
GLA_NSUB = 8
TM_GLA = GLA_SUB * GLA_NSUB
TM_MLP = 512
FF_CHUNK = 512
OUT_CHUNK = 256
CONV_HALO = 32
CONV_RB = 32
CONV_SH_ROWS = TM_MLP + CONV_HALO - SUBLANES
GLA_FAST_MIN_B = -60.0

VMEM_LIMIT = 56 * 1024 * 1024

F32 = jnp.float32
BF16 = jnp.bfloat16


def _sigmoid(x):
    return 1.0 / (1.0 + jnp.exp(-x))


def _split_bf16(x):
    hi = x.astype(BF16)
    lo = (x - hi.astype(F32)).astype(BF16)
    return hi, lo


def _dot(a, b):
    return jnp.dot(a, b, preferred_element_type=F32)


def _dot_nt(a, b):
    return lax.dot_general(a, b, (((1,), (1,)), ((), ())), preferred_element_type=F32)


def _dot_tn(a, b):
    return lax.dot_general(a, b, (((0,), (0,)), ((), ())), preferred_element_type=F32)


def _proj_kernel(x_ref, g1_ref, w_ref, wg_ref, bg_ref, wo_ref, w1_ref,
                 q_ref, k_ref, la_ref, v_ref, sg_ref, u_ref, wob_ref, w1b_ref, wb_ref, wgb_ref):
    wob_ref[...] = wo_ref[...].astype(BF16)
    w1b_ref[...] = w1_ref[...].astype(BF16)

    @pl.when(pl.program_id(0) == 0)
    def _():
        wb_ref[P_Q:P_CI, :] = w_ref[OFF_Q:OFF_A, :].astype(BF16)
        wb_ref[P_CI:P_Z, :] = w_ref[OFF_C:D_IN, :].astype(BF16)
        wb_ref[P_Z:P_Z + GATE_RANK, :] = w_ref[OFF_A:OFF_C, :].astype(BF16)
        wb_ref[P_Z + GATE_RANK:D_INP, :] = jnp.zeros((Z_PAD - GATE_RANK, D_MODEL), BF16)
        wgb_ref[...] = jnp.zeros_like(wgb_ref)
        wgb_ref[0:GATE_RANK, :] = wg_ref[...].astype(BF16)

    x = x_ref[...]
    ms = jnp.mean(x * x, axis=-1, keepdims=True)
    xn = (x * lax.rsqrt(ms + EPS) * g1_ref[...]).astype(BF16)
    proj = _dot_nt(xn, wb_ref[...])
    q_ref[...] = proj[:, P_Q:P_K] * (GLA_DK ** -0.5)
    k_ref[...] = proj[:, P_K:P_V]
    v_ref[...] = proj[:, P_V:P_G].astype(BF16)
    g = proj[:, P_G:P_CI]
    sg_ref[...] = (g * _sigmoid(g)).astype(BF16)
    u_ref[...] = proj[:, P_CI:P_CG] * _sigmoid(proj[:, P_CG:P_Z])
    z = proj[:, P_Z:D_INP].astype(BF16)
    a_logit = _dot(z, wgb_ref[...]) + bg_ref[...]
    la = jnp.minimum(a_logit, 0.0) - jnp.log(1.0 + jnp.exp(-jnp.abs(a_logit)))
    la_ref[...] = la * (1.0 / GATE_TAU)


def _proj_call(x2, g1, w_in_t, w_gate_up, bg, wo, w1):
    s = x2.shape[0]
    n = s // TM_PROJ
    row = lambda i: (i, 0)
    col = lambda i: (0, i)
    const = lambda i: (0, 0)
    return pl.pallas_call(
        _proj_kernel,
        grid=(n,),
        in_specs=[
            pl.BlockSpec((TM_PROJ, D_MODEL), row),
            pl.BlockSpec((1, D_MODEL), const),
            pl.BlockSpec((None, D_IN, D_MODEL), lambda i: (0, 0, 0), pipeline_mode=pl.Buffered(1)),
            pl.BlockSpec((GATE_RANK, GLA_KEY), const),
            pl.BlockSpec((1, GLA_KEY), const),
            pl.BlockSpec((D_MODEL // n, D_MODEL), row),
            pl.BlockSpec((D_MODEL, D_FF // n), col),
        ],
        out_specs=[
            pl.BlockSpec((TM_PROJ, GLA_KEY), row),
            pl.BlockSpec((TM_PROJ, GLA_KEY), row),
            pl.BlockSpec((TM_PROJ, GLA_KEY), row),
            pl.BlockSpec((TM_PROJ, GLA_VAL), row),
            pl.BlockSpec((TM_PROJ, GLA_VAL), row),
            pl.BlockSpec((TM_PROJ, CONV_CH), row),
            pl.BlockSpec((D_MODEL // n, D_MODEL), row),
            pl.BlockSpec((D_MODEL, D_FF // n), col),
        ],
        out_shape=[
            jax.ShapeDtypeStruct((s, GLA_KEY), F32),
            jax.ShapeDtypeStruct((s, GLA_KEY), F32),
            jax.ShapeDtypeStruct((s, GLA_KEY), F32),
            jax.ShapeDtypeStruct((s, GLA_VAL), BF16),
            jax.ShapeDtypeStruct((s, GLA_VAL), BF16),
            jax.ShapeDtypeStruct((s, CONV_CH), F32),
            jax.ShapeDtypeStruct((D_MODEL, D_MODEL), BF16),
            jax.ShapeDtypeStruct((D_MODEL, D_FF), BF16),
        ],
        scratch_shapes=[
            pltpu.VMEM((D_INP, D_MODEL), BF16),
            pltpu.VMEM((Z_PAD, GLA_KEY), BF16),
        ],
        compiler_params=pltpu.CompilerParams(
            dimension_semantics=("arbitrary",), vmem_limit_bytes=VMEM_LIMIT),
        name="proj",
    )(x2, g1, w_in_t, w_gate_up, bg, wo, w1)


def _gla_kernel(q_ref, k_ref, la_ref, v_ref, sg_ref, gn_ref, w2_ref,
                o_ref, w2b_ref, st_ref, b_ref):
    c = GLA_SUB

    w2b_ref[...] = w2_ref[...].astype(BF16)

    @pl.when(pl.program_id(0) == 0)
    def _():
        st_ref[...] = jnp.zeros_like(st_ref)

    ri = lax.broadcasted_iota(jnp.int32, (c, c), 0)
    ci = lax.broadcasted_iota(jnp.int32, (c, c), 1)
    first_head = lax.broadcasted_iota(jnp.int32, (c, LANES), 1) < GLA_DK
    st_row = lax.broadcasted_iota(jnp.int32, (2 * GLA_DV, LANES), 0)
    st_lane = lax.broadcasted_iota(jnp.int32, (2 * GLA_DV, LANES), 1)
    same_head = (st_row < GLA_DV) == (st_lane < GLA_DK)

    def masked_sum(m, rows):
        mb = m.astype(BF16)
        la_hi, la_lo = _split_bf16(la_ref[rows, :])
        return _dot(mb, la_hi) + _dot(mb, la_lo)

    b_min = None
    for s in range(GLA_NSUB):
        rows = slice(s * c, (s + 1) * c)
        b = masked_sum(ci <= ri, rows)
        b_ref[rows, :] = b
        b_min = b[c - 1:c, :] if b_min is None else jnp.minimum(b_min, b[c - 1:c, :])
    fast = jnp.min(b_min) > GLA_FAST_MIN_B

    def chunk(rows, states, scores_fn):
        b = b_ref[rows, :]
        b_last = b[c - 1:c, :]
        q = q_ref[rows, :]
        k = k_ref[rows, :]
        q_in = (q * jnp.exp(b)).astype(BF16)
        k_out = (k * jnp.exp(b_last - b)).astype(BF16)
        a_chunk = jnp.exp(b_last)
        new_states = []
        for p in range(GLA_HEADS // 2):
            kc = slice(p * LANES, (p + 1) * LANES)
            v_p = v_ref[rows, 2 * p * GLA_DV:2 * (p + 1) * GLA_DV]
            st = states[p]
            o_inter = _dot_nt(q_in[:, kc], st.astype(BF16))
            for j, s_h in enumerate(scores_fn(p, q, k, b, q_in)):
                h = 2 * p + j
                dv = slice(j * GLA_DV, (j + 1) * GLA_DV)
                o_h = _dot(s_h.astype(BF16), v_p[:, dv]) + o_inter[:, dv]
                ms = jnp.mean(o_h * o_h, axis=-1, keepdims=True)
                y = o_h * lax.rsqrt(ms + EPS) * gn_ref[...]
                cols = slice(h * GLA_DV, (h + 1) * GLA_DV)
                o_ref[rows, cols] = (y * sg_ref[rows, cols].astype(F32)).astype(BF16)
            upd = _dot_tn(v_p, k_out[:, kc])
            new_states.append(a_chunk[:, kc] * st + jnp.where(same_head, upd, 0.0))
        return new_states

    def fast_scores(p, q, k, b, q_in):
        kc = slice(p * LANES, (p + 1) * LANES)
        k_div = k[:, kc] * jnp.exp(-b[:, kc])
        k_blk = jnp.concatenate([jnp.where(first_head, k_div, 0.0),
                                 jnp.where(first_head, 0.0, k_div)], axis=0).astype(BF16)
        s2 = _dot_nt(q_in[:, kc], k_blk)
        return [jnp.where(ri >= ci, s2[:, j * c:(j + 1) * c], 0.0) for j in range(2)]

    def safe_scores(rows):
        q = q_ref[rows, :]
        k = k_ref[rows, :]
        kb = k.astype(BF16)
        qb = q.astype(BF16)
        scores = []
        for h in range(GLA_HEADS):
            dk = slice(h * GLA_DK, (h + 1) * GLA_DK)
            scores.append(jnp.where(ri == ci, _dot_nt(qb[:, dk], kb[:, dk]), 0.0))
        half = c // 2
        while half >= 1:
            ref = (ri & (-2 * half)) + (half - 1)
            lo = jnp.minimum(ri, ref)
            hi = jnp.maximum(ri, ref)
            e = jnp.exp(masked_sum((ci > lo) & (ci <= hi), rows))
            q_l = (q * e).astype(BF16)
            k_l = (k * e).astype(BF16)
            pair = (((ri ^ ci) & (-2 * half)) == 0) & ((ri & half) != 0) & ((ci & half) == 0)
            for h in range(GLA_HEADS):
                dk = slice(h * GLA_DK, (h + 1) * GLA_DK)
                scores[h] = jnp.where(pair, _dot_nt(q_l[:, dk], k_l[:, dk]), scores[h])
            half //= 2
        return scores

    @pl.when(fast)
    def _():
        states = [st_ref[p] for p in range(GLA_HEADS // 2)]
        for s in range(GLA_NSUB):
            states = chunk(slice(s * c, (s + 1) * c), states, fast_scores)
        for p in range(GLA_HEADS // 2):
            st_ref[p] = states[p]

    @pl.when(jnp.logical_not(fast))
    def _():
        def body(s, carry):
            rows = pl.ds(pl.multiple_of(s * c, c), c)
            scores = safe_scores(rows)
            states = chunk(rows, [st_ref[p] for p in range(GLA_HEADS // 2)],
                           lambda p, *_: scores[2 * p:2 * p + 2])
            for p in range(GLA_HEADS // 2):
                st_ref[p] = states[p]
            return carry

        lax.fori_loop(0, GLA_NSUB, body, 0)


def _gla_call(q, k, la, v, sg, gn, w2):
    s = q.shape[0]
    n = s // TM_GLA
    row = lambda i: (i, 0)
    const = lambda i: (0, 0)
    return pl.pallas_call(
        _gla_kernel,
        grid=(n,),
        in_specs=[
            pl.BlockSpec((TM_GLA, GLA_KEY), row),
            pl.BlockSpec((TM_GLA, GLA_KEY), row),
            pl.BlockSpec((TM_GLA, GLA_KEY), row),
            pl.BlockSpec((TM_GLA, GLA_VAL), row),
            pl.BlockSpec((TM_GLA, GLA_VAL), row),
            pl.BlockSpec((1, GLA_DV), const),
            pl.BlockSpec((D_FF // n, D_MODEL), row),
        ],
        out_specs=[
            pl.BlockSpec((TM_GLA, GLA_VAL), row),
            pl.BlockSpec((D_FF // n, D_MODEL), row),
        ],
        out_shape=[
            jax.ShapeDtypeStruct((s, GLA_VAL), BF16),
            jax.ShapeDtypeStruct((D_FF, D_MODEL), BF16),
        ],
        scratch_shapes=[
            pltpu.VMEM((GLA_HEADS // 2, 2 * GLA_DV, LANES), F32),
            pltpu.VMEM((TM_GLA, GLA_KEY), F32),
        ],
        compiler_params=pltpu.CompilerParams(
            dimension_semantics=("arbitrary",), vmem_limit_bytes=VMEM_LIMIT),
        name="gla",
    )(q, k, la, v, sg, gn, w2)


def _group_mean(t, low_half, size):
    lo = jnp.sum(jnp.where(low_half, t, 0.0), axis=-1, keepdims=True)
    hi = jnp.sum(jnp.where(low_half, 0.0, t), axis=-1, keepdims=True)
    return jnp.where(low_half, lo, hi) * (1.0 / size)


def _interleave(mxu_items, vec_items):
    done = 0
    for i, item in enumerate(mxu_items):
        item()
        upto = (i + 1) * len(vec_items) // len(mxu_items)
        for v_item in vec_items[done:upto]:
            v_item()
        done = upto


def _mlp_conv_kernel(x_ref, og_ref, un_ref, u0_ref, cw_ref, cb_ref, gg_ref, gb_ref,
                     wo_ref, g2_ref, w1_ref, w2_ref, gf_ref,
                     o_ref, ext_ref, sh_ref, oc_ref, f_ref):
    tm = TM_MLP
    grp = CONV_CH // CONV_GROUPS
    low_half = lax.broadcasted_iota(jnp.int32, (CONV_RB, LANES), 1) < grp
    first = CONV_HALO - (CONV_W - 1)

    def shifted_copy(p):
        sh_ref[p - 1] = ext_ref[p:p + CONV_SH_ROWS, :]

    def conv_block(base):
        for cg in range(CONV_CH // LANES):
            cols = slice(cg * LANES, (cg + 1) * LANES)
            acc = jnp.broadcast_to(cb_ref[:, cols], (CONV_RB, LANES))
            for t in range(CONV_W):
                shift = first + t
                p = shift % SUBLANES
                rows = pl.ds(base + (shift - p), CONV_RB)
                tap = ext_ref[rows, cols] if p == 0 else sh_ref[p - 1, rows, cols]
                acc = acc + cw_ref[t:t + 1, cols] * tap
            d = acc - _group_mean(acc, low_half, grp)
            var = _group_mean(d * d, low_half, grp)
            yn = d * lax.rsqrt(var + EPS) * gg_ref[:, cols] + gb_ref[:, cols]
            oc_ref[pl.ds(base, CONV_RB), cols] = (yn * _sigmoid(yn)).astype(BF16)

    @pl.when(pl.program_id(0) == 0)
    def _():
        ext_ref[0:CONV_HALO, :] = jnp.zeros((CONV_HALO, CONV_CH), F32)
        ext_ref[CONV_HALO:, :] = u0_ref[...]
        for p in range(1, SUBLANES):
            shifted_copy(p)

        def body(r, carry):
            conv_block(pl.multiple_of(r * CONV_RB, CONV_RB))
            return carry

        lax.fori_loop(0, tm // CONV_RB, body, 0)

    h = x_ref[...] + _dot(og_ref[...], wo_ref[0:GLA_VAL, :]) + _dot(oc_ref[...], wo_ref[GLA_VAL:, :])
    ms = jnp.mean(h * h, axis=-1, keepdims=True)
    hn = (h * lax.rsqrt(ms + EPS) * g2_ref[...]).astype(BF16)

    def roll_window():
        ext_ref[0:CONV_HALO, :] = ext_ref[tm:tm + CONV_HALO, :]
        ext_ref[CONV_HALO:, :] = un_ref[...]

    vec_items = [roll_window] + [functools.partial(shifted_copy, p) for p in range(1, SUBLANES)]
    vec_items += [functools.partial(conv_block, r * CONV_RB) for r in range(tm // CONV_RB)]

    def ff_in(j):
        cols = slice(j * FF_CHUNK, (j + 1) * FF_CHUNK)
        f = jnp.maximum(_dot(hn, w1_ref[:, cols]), 0.0)
        f_ref[:, cols] = (f * f).astype(BF16)

    def ff_out(j):
        cols = slice(j * OUT_CHUNK, (j + 1) * OUT_CHUNK)
        o_ref[:, cols] = h[:, cols] + _dot(f_ref[...], w2_ref[:, cols])

    mxu_items = [functools.partial(ff_in, j) for j in range(D_FF // FF_CHUNK)]
    mxu_items += [functools.partial(ff_out, j) for j in range(D_MODEL // OUT_CHUNK)]
    _interleave(mxu_items, vec_items)

    acc = o_ref[...]
    ms2 = jnp.mean(acc * acc, axis=-1, keepdims=True)
    o_ref[...] = acc * lax.rsqrt(ms2 + EPS) * gf_ref[...]


def _mlp_conv_call(x2, og, u, cw, cb, gg, gb, wo, g2, w1, w2, gf):
    s = x2.shape[0]
    n = s // TM_MLP
    row = lambda i: (i, 0)
    nxt = lambda i: (jnp.minimum(i + 1, n - 1), 0)
    const = lambda i: (0, 0)
    single = pl.Buffered(1)
    return pl.pallas_call(
        _mlp_conv_kernel,
        grid=(n,),
        in_specs=[
            pl.BlockSpec((TM_MLP, D_MODEL), row),
            pl.BlockSpec((TM_MLP, GLA_VAL), row),
            pl.BlockSpec((TM_MLP, CONV_CH), nxt),
            pl.BlockSpec((TM_MLP, CONV_CH), const, pipeline_mode=single),
            pl.BlockSpec((CONV_W, CONV_CH), const),
            pl.BlockSpec((1, CONV_CH), const),
            pl.BlockSpec((1, CONV_CH), const),
            pl.BlockSpec((1, CONV_CH), const),
            pl.BlockSpec((D_MODEL, D_MODEL), const, pipeline_mode=single),
            pl.BlockSpec((1, D_MODEL), const),
            pl.BlockSpec((D_MODEL, D_FF), const, pipeline_mode=single),
            pl.BlockSpec((D_FF, D_MODEL), const, pipeline_mode=single),
            pl.BlockSpec((1, D_MODEL), const),
        ],
        out_specs=pl.BlockSpec((TM_MLP, D_MODEL), row),
        out_shape=jax.ShapeDtypeStruct((s, D_MODEL), F32),
        scratch_shapes=[
            pltpu.VMEM((TM_MLP + CONV_HALO, CONV_CH), F32),
            pltpu.VMEM((SUBLANES - 1, CONV_SH_ROWS, CONV_CH), F32),
            pltpu.VMEM((TM_MLP, CONV_CH), BF16),
            pltpu.VMEM((TM_MLP, D_FF), BF16),
        ],
        compiler_params=pltpu.CompilerParams(
            dimension_semantics=("arbitrary",), vmem_limit_bytes=VMEM_LIMIT),
        name="mlp_conv",
    )(x2, og, u, u, cw, cb, gg, gb, wo, g2, w1, w2, gf)


def kernel(x, norm1_g, w_in, w_gate_up, b_gate, gla_norm_g, conv_w, conv_b, conv_norm_g,
           conv_norm_b, w_out, norm2_g, w_mlp_in, w_mlp_out, final_norm_g):
    bsz, seq, _ = x.shape
    x2 = x.reshape(bsz * seq, D_MODEL)
    assert bsz == 1, "state / halo carry across grid steps assumes one sequence"
    l = 0
    assert w_in.shape[0] == 1, "single layer: the projection weight is passed with its layer axis"
    w_in_t = jnp.swapaxes(w_in, 1, 2)
    q, k, la, v, sg, u, wo_b, w1_b = _proj_call(
        x2, norm1_g[l][None, :], w_in_t, w_gate_up[l], b_gate[l][None, :], w_out[l], w_mlp_in[l])
    o_gla, w2_b = _gla_call(q, k, la, v, sg, gla_norm_g[l][None, :], w_mlp_out[l])
    out = _mlp_conv_call(x2, o_gla, u, conv_w[l], conv_b[l][None, :], conv_norm_g[l][None, :],
                         conv_norm_b[l][None, :], wo_b, norm2_g[l][None, :], w1_b, w2_b,
                         final_norm_g[None, :])
    return out.reshape(bsz, seq, D_MODEL)
```

```python
import functools

import jax
import jax.numpy as jnp
from jax import lax
from jax.experimental import pallas as pl
from jax.experimental.pallas import tpu as pltpu

D_MODEL = 1024
GLA_HEADS = 4
GLA_DV = 128
GLA_DK = 64
GLA_KEY = GLA_HEADS * GLA_DK
GLA_VAL = GLA_HEADS * GLA_DV
GATE_RANK = 16
GATE_TAU = 16.0
CONV_CH = 512
CONV_GROUPS = 8
CONV_W = 31
D_FF = 4 * D_MODEL
EPS = 1e-6

OFF_Q = 0
OFF_K = OFF_Q + GLA_KEY
OFF_V = OFF_K + GLA_KEY
OFF_G = OFF_V + GLA_VAL
OFF_A = OFF_G + GLA_VAL
OFF_C = OFF_A + GATE_RANK
D_IN = OFF_C + 2 * CONV_CH

LANES = 128
SUBLANES = 8
Z_PAD = LANES
P_Q = 0
P_K = P_Q + GLA_KEY
P_V = P_K + GLA_KEY
P_G = P_V + GLA_VAL
P_CI = P_G + GLA_VAL
P_CG = P_CI + CONV_CH
P_Z = P_CG + CONV_CH
D_INP = P_Z + Z_PAD

TM_PROJ = 1024
GLA_SUB = 256
GLA_NSUB = 8
TM_GLA = GLA_SUB * GLA_NSUB
TM_MLP = 512
FF_CHUNK = 512
OUT_CHUNK = 256
CONV_HALO = 32
CONV_RB = 32
CONV_SH_ROWS = TM_MLP + CONV_HALO - SUBLANES
GLA_FAST_MIN_B = -60.0

VMEM_LIMIT = 56 * 1024 * 1024

F32 = jnp.float32
BF16 = jnp.bfloat16


def _sigmoid(x):
    return 1.0 / (1.0 + jnp.exp(-x))


def _split_bf16(x):
    hi = x.astype(BF16)
    lo = (x - hi.astype(F32)).astype(BF16)
    return hi, lo


def _dot(a, b):
    return jnp.dot(a, b, preferred_element_type=F32)


def _dot_nt(a, b):
    return lax.dot_general(a, b, (((1,), (1,)), ((), ())), preferred_element_type=F32)


def _dot_tn(a, b):
    return lax.dot_general(a, b, (((0,), (0,)), ((), ())), preferred_element_type=F32)


def _proj_kernel(x_ref, g1_ref, w_ref, wg_ref, bg_ref, wo_ref, w1_ref,
                 q_ref, k_ref, la_ref, v_ref, sg_ref, u_ref, wob_ref, w1b_ref, wb_ref, wgb_ref):
    wob_ref[...] = wo_ref[...].astype(BF16)
    w1b_ref[...] = w1_ref[...].astype(BF16)

    @pl.when(pl.program_id(0) == 0)
    def _():
        wb_ref[P_Q:P_CI, :] = w_ref[OFF_Q:OFF_A, :].astype(BF16)
        wb_ref[P_CI:P_Z, :] = w_ref[OFF_C:D_IN, :].astype(BF16)
        wb_ref[P_Z:P_Z + GATE_RANK, :] = w_ref[OFF_A:OFF_C, :].astype(BF16)
        wb_ref[P_Z + GATE_RANK:D_INP, :] = jnp.zeros((Z_PAD - GATE_RANK, D_MODEL), BF16)
        wgb_ref[...] = jnp.zeros_like(wgb_ref)
        wgb_ref[0:GATE_RANK, :] = wg_ref[...].astype(BF16)

    x = x_ref[...]
    ms = jnp.mean(x * x, axis=-1, keepdims=True)
    xn = (x * lax.rsqrt(ms + EPS) * g1_ref[...]).astype(BF16)
    proj = _dot_nt(xn, wb_ref[...])
    q_ref[...] = proj[:, P_Q:P_K] * (GLA_DK ** -0.5)
    k_ref[...] = proj[:, P_K:P_V]
    v_ref[...] = proj[:, P_V:P_G].astype(BF16)
    g = proj[:, P_G:P_CI]
    sg_ref[...] = (g * _sigmoid(g)).astype(BF16)
    u_ref[...] = proj[:, P_CI:P_CG] * _sigmoid(proj[:, P_CG:P_Z])
    z = proj[:, P_Z:D_INP].astype(BF16)
    a_logit = _dot(z, wgb_ref[...]) + bg_ref[...]
    la = jnp.minimum(a_logit, 0.0) - jnp.log(1.0 + jnp.exp(-jnp.abs(a_logit)))
    la_ref[...] = la * (1.0 / GATE_TAU)


def _proj_call(x2, g1, w_in_t, w_gate_up, bg, wo, w1):
    s = x2.shape[0]
    n = s // TM_PROJ
    row = lambda i: (i, 0)
    col = lambda i: (0, i)
    const = lambda i: (0, 0)
    return pl.pallas_call(
        _proj_kernel,
        grid=(n,),
        in_specs=[
            pl.BlockSpec((TM_PROJ, D_MODEL), row),
            pl.BlockSpec((1, D_MODEL), const),
            pl.BlockSpec((None, D_IN, D_MODEL), lambda i: (0, 0, 0), pipeline_mode=pl.Buffered(1)),
            pl.BlockSpec((GATE_RANK, GLA_KEY), const),
            pl.BlockSpec((1, GLA_KEY), const),
            pl.BlockSpec((D_MODEL // n, D_MODEL), row),
            pl.BlockSpec((D_MODEL, D_FF // n), col),
        ],
        out_specs=[
            pl.BlockSpec((TM_PROJ, GLA_KEY), row),
            pl.BlockSpec((TM_PROJ, GLA_KEY), row),
            pl.BlockSpec((TM_PROJ, GLA_KEY), row),
            pl.BlockSpec((TM_PROJ, GLA_VAL), row),
            pl.BlockSpec((TM_PROJ, GLA_VAL), row),
            pl.BlockSpec((TM_PROJ, CONV_CH), row),
            pl.BlockSpec((D_MODEL // n, D_MODEL), row),
            pl.BlockSpec((D_MODEL, D_FF // n), col),
        ],
        out_shape=[
            jax.ShapeDtypeStruct((s, GLA_KEY), F32),
            jax.ShapeDtypeStruct((s, GLA_KEY), F32),
            jax.ShapeDtypeStruct((s, GLA_KEY), F32),
            jax.ShapeDtypeStruct((s, GLA_VAL), BF16),
            jax.ShapeDtypeStruct((s, GLA_VAL), BF16),
            jax.ShapeDtypeStruct((s, CONV_CH), F32),
            jax.ShapeDtypeStruct((D_MODEL, D_MODEL), BF16),
            jax.ShapeDtypeStruct((D_MODEL, D_FF), BF16),
        ],
        scratch_shapes=[
            pltpu.VMEM((D_INP, D_MODEL), BF16),
            pltpu.VMEM((Z_PAD, GLA_KEY), BF16),
        ],
        compiler_params=pltpu.CompilerParams(
            dimension_semantics=("arbitrary",), vmem_limit_bytes=VMEM_LIMIT),
        name="proj",
    )(x2, g1, w_in_t, w_gate_up, bg, wo, w1)


def _gla_kernel(q_ref, k_ref, la_ref, v_ref, sg_ref, gn_ref, w2_ref,
                o_ref, w2b_ref, st_ref, b_ref):
    c = GLA_SUB

    w2b_ref[...] = w2_ref[...].astype(BF16)

    @pl.when(pl.program_id(0) == 0)
    def _():
        st_ref[...] = jnp.zeros_like(st_ref)

    ri = lax.broadcasted_iota(jnp.int32, (c, c), 0)
    ci = lax.broadcasted_iota(jnp.int32, (c, c), 1)
    first_head = lax.broadcasted_iota(jnp.int32, (c, LANES), 1) < GLA_DK
    st_row = lax.broadcasted_iota(jnp.int32, (2 * GLA_DV, LANES), 0)
    st_lane = lax.broadcasted_iota(jnp.int32, (2 * GLA_DV, LANES), 1)
    same_head = (st_row < GLA_DV) == (st_lane < GLA_DK)

    def masked_sum(m, rows):
        mb = m.astype(BF16)
        la_hi, la_lo = _split_bf16(la_ref[rows, :])
        return _dot(mb, la_hi) + _dot(mb, la_lo)

    b_min = None
    for s in range(GLA_NSUB):
        rows = slice(s * c, (s + 1) * c)
        b = masked_sum(ci <= ri, rows)
        b_ref[rows, :] = b
        b_min = b[c - 1:c, :] if b_min is None else jnp.minimum(b_min, b[c - 1:c, :])
    fast = jnp.min(b_min) > GLA_FAST_MIN_B

    def chunk(rows, states, scores_fn):
        b = b_ref[rows, :]
        b_last = b[c - 1:c, :]
        q = q_ref[rows, :]
        k = k_ref[rows, :]
        q_in = (q * jnp.exp(b)).astype(BF16)
        k_out = (k * jnp.exp(b_last - b)).astype(BF16)
        a_chunk = jnp.exp(b_last)
        new_states = []
        for p in range(GLA_HEADS // 2):
            kc = slice(p * LANES, (p + 1) * LANES)
            v_p = v_ref[rows, 2 * p * GLA_DV:2 * (p + 1) * GLA_DV]
            st = states[p]
            o_inter = _dot_nt(q_in[:, kc], st.astype(BF16))
            for j, s_h in enumerate(scores_fn(p, q, k, b, q_in)):
                h = 2 * p + j
                dv = slice(j * GLA_DV, (j + 1) * GLA_DV)
                o_h = _dot(s_h.astype(BF16), v_p[:, dv]) + o_inter[:, dv]
                ms = jnp.mean(o_h * o_h, axis=-1, keepdims=True)
                y = o_h * lax.rsqrt(ms + EPS) * gn_ref[...]
                cols = slice(h * GLA_DV, (h + 1) * GLA_DV)
                o_ref[rows, cols] = (y * sg_ref[rows, cols].astype(F32)).astype(BF16)
            upd = _dot_tn(v_p, k_out[:, kc])
            new_states.append(a_chunk[:, kc] * st + jnp.where(same_head, upd, 0.0))
        return new_states

    def fast_scores(p, q, k, b, q_in):
        kc = slice(p * LANES, (p + 1) * LANES)
        k_div = k[:, kc] * jnp.exp(-b[:, kc])
        k_blk = jnp.concatenate([jnp.where(first_head, k_div, 0.0),
                                 jnp.where(first_head, 0.0, k_div)], axis=0).astype(BF16)
        s2 = _dot_nt(q_in[:, kc], k_blk)
        return [jnp.where(ri >= ci, s2[:, j * c:(j + 1) * c], 0.0) for j in range(2)]

    def safe_scores(rows):
        q = q_ref[rows, :]
        k = k_ref[rows, :]
        kb = k.astype(BF16)
        qb = q.astype(BF16)
        scores = []
        for h in range(GLA_HEADS):
            dk = slice(h * GLA_DK, (h + 1) * GLA_DK)
            scores.append(jnp.where(ri == ci, _dot_nt(qb[:, dk], kb[:, dk]), 0.0))
        half = c // 2
        while half >= 1:
            ref = (ri & (-2 * half)) + (half - 1)
            lo = jnp.minimum(ri, ref)
            hi = jnp.maximum(ri, ref)
            e = jnp.exp(masked_sum((ci > lo) & (ci <= hi), rows))
            q_l = (q * e).astype(BF16)
            k_l = (k * e).astype(BF16)
            pair = (((ri ^ ci) & (-2 * half)) == 0) & ((ri & half) != 0) & ((ci & half) == 0)
            for h in range(GLA_HEADS):
                dk = slice(h * GLA_DK, (h + 1) * GLA_DK)
                scores[h] = jnp.where(pair, _dot_nt(q_l[:, dk], k_l[:, dk]), scores[h])
            half //= 2
        return scores

    @pl.when(fast)
    def _():
        states = [st_ref[p] for p in range(GLA_HEADS // 2)]
        for s in range(GLA_NSUB):
            states = chunk(slice(s * c, (s + 1) * c), states, fast_scores)
        for p in range(GLA_HEADS // 2):
            st_ref[p] = states[p]

    @pl.when(jnp.logical_not(fast))
    def _():
        def body(s, carry):
            rows = pl.ds(pl.multiple_of(s * c, c), c)
            scores = safe_scores(rows)
            states = chunk(rows, [st_ref[p] for p in range(GLA_HEADS // 2)],
                           lambda p, *_: scores[2 * p:2 * p + 2])
            for p in range(GLA_HEADS // 2):
                st_ref[p] = states[p]
            return carry

        lax.fori_loop(0, GLA_NSUB, body, 0)


def _gla_call(q, k, la, v, sg, gn, w2):
    s = q.shape[0]
    n = s // TM_GLA
    row = lambda i: (i, 0)
    const = lambda i: (0, 0)
    return pl.pallas_call(
        _gla_kernel,
        grid=(n,),
        in_specs=[
            pl.BlockSpec((TM_GLA, GLA_KEY), row),
            pl.BlockSpec((TM_GLA, GLA_KEY), row),
            pl.BlockSpec((TM_GLA, GLA_KEY), row),
            pl.BlockSpec((TM_GLA, GLA_VAL), row),
            pl.BlockSpec((TM_GLA, GLA_VAL), row),
            pl.BlockSpec((1, GLA_DV), const),
            pl.BlockSpec((D_FF // n, D_MODEL), row),
        ],
        out_specs=[
            pl.BlockSpec((TM_GLA, GLA_VAL), row),
            pl.BlockSpec((D_FF // n, D_MODEL), row),
        ],
        out_shape=[
            jax.ShapeDtypeStruct((s, GLA_VAL), BF16),
            jax.ShapeDtypeStruct((D_FF, D_MODEL), BF16),
        ],
        scratch_shapes=[
            pltpu.VMEM((GLA_HEADS // 2, 2 * GLA_DV, LANES), F32),
            pltpu.VMEM((TM_GLA, GLA_KEY), F32),
        ],
        compiler_params=pltpu.CompilerParams(
            dimension_semantics=("arbitrary",), vmem_limit_bytes=VMEM_LIMIT),
        name="gla",
    )(q, k, la, v, sg, gn, w2)


def _group_mean(t, low_half, size):
    lo = jnp.sum(jnp.where(low_half, t, 0.0), axis=-1, keepdims=True)
    hi = jnp.sum(jnp.where(low_half, 0.0, t), axis=-1, keepdims=True)
    return jnp.where(low_half, lo, hi) * (1.0 / size)


def _interleave(mxu_items, vec_items):
    done = 0
    for i, item in enumerate(mxu_items):
        item()
        upto = (i + 1) * len(vec_items) // len(mxu_items)
        for v_item in vec_items[done:upto]:
            v_item()
        done = upto


def _mlp_conv_kernel(x_ref, og_ref, un_ref, u0_ref, cw_ref, cb_ref, gg_ref, gb_ref,
                     wo_ref, g2_ref, w1_ref, w2_ref, gf_ref,
                     o_ref, ext_ref, sh_ref, oc_ref, f_ref):
    tm = TM_MLP
    grp = CONV_CH // CONV_GROUPS
    low_half = lax.broadcasted_iota(jnp.int32, (CONV_RB, LANES), 1) < grp
    first = CONV_HALO - (CONV_W - 1)

    def shifted_copy(p):
        sh_ref[p - 1] = ext_ref[p:p + CONV_SH_ROWS, :]

    def conv_block(base):
        for cg in range(CONV_CH // LANES):
            cols = slice(cg * LANES, (cg + 1) * LANES)
            acc = jnp.broadcast_to(cb_ref[:, cols], (CONV_RB, LANES))
            for t in range(CONV_W):
                shift = first + t
                p = shift % SUBLANES
                rows = pl.ds(base + (shift - p), CONV_RB)
                tap = ext_ref[rows, cols] if p == 0 else sh_ref[p - 1, rows, cols]
                acc = acc + cw_ref[t:t + 1, cols] * tap
            d = acc - _group_mean(acc, low_half, grp)
            var = _group_mean(d * d, low_half, grp)
            yn = d * lax.rsqrt(var + EPS) * gg_ref[:, cols] + gb_ref[:, cols]
            oc_ref[pl.ds(base, CONV_RB), cols] = (yn * _sigmoid(yn)).astype(BF16)

    @pl.when(pl.program_id(0) == 0)
    def _():
        ext_ref[0:CONV_HALO, :] = jnp.zeros((CONV_HALO, CONV_CH), F32)
        ext_ref[CONV_HALO:, :] = u0_ref[...]
        for p in range(1, SUBLANES):
            shifted_copy(p)

        def body(r, carry):
            conv_block(pl.multiple_of(r * CONV_RB, CONV_RB))
            return carry

        lax.fori_loop(0, tm // CONV_RB, body, 0)

    h = x_ref[...] + _dot(og_ref[...], wo_ref[0:GLA_VAL, :]) + _dot(oc_ref[...], wo_ref[GLA_VAL:, :])
    ms = jnp.mean(h * h, axis=-1, keepdims=True)
    hn = (h * lax.rsqrt(ms + EPS) * g2_ref[...]).astype(BF16)

    def roll_window():
        ext_ref[0:CONV_HALO, :] = ext_ref[tm:tm + CONV_HALO, :]
        ext_ref[CONV_HALO:, :] = un_ref[...]

    vec_items = [roll_window] + [functools.partial(shifted_copy, p) for p in range(1, SUBLANES)]
    vec_items += [functools.partial(conv_block, r * CONV_RB) for r in range(tm // CONV_RB)]

    def ff_in(j):
        cols = slice(j * FF_CHUNK, (j + 1) * FF_CHUNK)
        f = jnp.maximum(_dot(hn, w1_ref[:, cols]), 0.0)
        f_ref[:, cols] = (f * f).astype(BF16)

    def ff_out(j):
        cols = slice(j * OUT_CHUNK, (j + 1) * OUT_CHUNK)
        o_ref[:, cols] = h[:, cols] + _dot(f_ref[...], w2_ref[:, cols])

    mxu_items = [functools.partial(ff_in, j) for j in range(D_FF // FF_CHUNK)]
    mxu_items += [functools.partial(ff_out, j) for j in range(D_MODEL // OUT_CHUNK)]
    _interleave(mxu_items, vec_items)

    acc = o_ref[...]
    ms2 = jnp.mean(acc * acc, axis=-1, keepdims=True)
    o_ref[...] = acc * lax.rsqrt(ms2 + EPS) * gf_ref[...]


def _mlp_conv_call(x2, og, u, cw, cb, gg, gb, wo, g2, w1, w2, gf):
    s = x2.shape[0]
    n = s // TM_MLP
    row = lambda i: (i, 0)
    nxt = lambda i: (jnp.minimum(i + 1, n - 1), 0)
    const = lambda i: (0, 0)
    single = pl.Buffered(1)
    return pl.pallas_call(
        _mlp_conv_kernel,
        grid=(n,),
        in_specs=[
            pl.BlockSpec((TM_MLP, D_MODEL), row),
            pl.BlockSpec((TM_MLP, GLA_VAL), row),
            pl.BlockSpec((TM_MLP, CONV_CH), nxt),
            pl.BlockSpec((TM_MLP, CONV_CH), const, pipeline_mode=single),
            pl.BlockSpec((CONV_W, CONV_CH), const),
            pl.BlockSpec((1, CONV_CH), const),
            pl.BlockSpec((1, CONV_CH), const),
            pl.BlockSpec((1, CONV_CH), const),
            pl.BlockSpec((D_MODEL, D_MODEL), const, pipeline_mode=single),
            pl.BlockSpec((1, D_MODEL), const),
            pl.BlockSpec((D_MODEL, D_FF), const, pipeline_mode=single),
            pl.BlockSpec((D_FF, D_MODEL), const, pipeline_mode=single),
            pl.BlockSpec((1, D_MODEL), const),
        ],
        out_specs=pl.BlockSpec((TM_MLP, D_MODEL), row),
        out_shape=jax.ShapeDtypeStruct((s, D_MODEL), F32),
        scratch_shapes=[
            pltpu.VMEM((TM_MLP + CONV_HALO, CONV_CH), F32),
            pltpu.VMEM((SUBLANES - 1, CONV_SH_ROWS, CONV_CH), F32),
            pltpu.VMEM((TM_MLP, CONV_CH), BF16),
            pltpu.VMEM((TM_MLP, D_FF), BF16),
        ],
        compiler_params=pltpu.CompilerParams(
            dimension_semantics=("arbitrary",), vmem_limit_bytes=VMEM_LIMIT),
        name="mlp_conv",
    )(x2, og, u, u, cw, cb, gg, gb, wo, g2, w1, w2, gf)


def kernel(x, norm1_g, w_in, w_gate_up, b_gate, gla_norm_g, conv_w, conv_b, conv_norm_g,
           conv_norm_b, w_out, norm2_g, w_mlp_in, w_mlp_out, final_norm_g):
    bsz, seq, _ = x.shape
    x2 = x.reshape(bsz * seq, D_MODEL)
    assert bsz == 1, "state / halo carry across grid steps assumes one sequence"
    l = 0
    assert w_in.shape[0] == 1, "single layer: the projection weight is passed with its layer axis"
    w_in_t = jnp.swapaxes(w_in, 1, 2)
    q, k, la, v, sg, u, wo_b, w1_b = _proj_call(
        x2, norm1_g[l][None, :], w_in_t, w_gate_up[l], b_gate[l][None, :], w_out[l], w_mlp_in[l])
    o_gla, w2_b = _gla_call(q, k, la, v, sg, gla_norm_g[l][None, :], w_mlp_out[l])
    out = _mlp_conv_call(x2, o_gla, u, conv_w[l], conv_b[l][None, :], conv_norm_g[l][None, :],
                         conv_norm_b[l][None, :], wo_b, norm2_g[l][None, :], w1_b, w2_b,
                         final_norm_g[None, :])
    return out.reshape(bsz, seq, D_MODEL)
```

```python
import functools

import jax
import jax.numpy as jnp
from jax import lax
from jax.experimental import pallas as pl
from jax.experimental.pallas import tpu as pltpu

D_MODEL = 1024
GLA_HEADS = 4
GLA_DV = 128
GLA_DK = 64
GLA_KEY = GLA_HEADS * GLA_DK
GLA_VAL = GLA_HEADS * GLA_DV
GATE_RANK = 16
GATE_TAU = 16.0
CONV_CH = 512
CONV_GROUPS = 8
CONV_W = 31
D_FF = 4 * D_MODEL
EPS = 1e-6

OFF_Q = 0
OFF_K = OFF_Q + GLA_KEY
OFF_V = OFF_K + GLA_KEY
OFF_G = OFF_V + GLA_VAL
OFF_A = OFF_G + GLA_VAL
OFF_C = OFF_A + GATE_RANK
D_IN = OFF_C + 2 * CONV_CH

LANES = 128
SUBLANES = 8
Z_PAD = LANES
P_Q = 0
P_K = P_Q + GLA_KEY
P_V = P_K + GLA_KEY
P_G = P_V + GLA_VAL
P_CI = P_G + GLA_VAL
P_CG = P_CI + CONV_CH
P_Z = P_CG + CONV_CH
D_INP = P_Z + Z_PAD

TM_PROJ = 1024
GLA_SUB = 256
GLA_NSUB = 8
TM_GLA = GLA_SUB * GLA_NSUB
TM_MLP = 512
FF_CHUNK = 512
OUT_CHUNK = 256
CONV_HALO = 32
CONV_RB = 32
CONV_SH_ROWS = TM_MLP + CONV_HALO - SUBLANES
GLA_FAST_MIN_B = -60.0

VMEM_LIMIT = 56 * 1024 * 1024

F32 = jnp.float32
BF16 = jnp.bfloat16


def _sigmoid(x):
    return 1.0 / (1.0 + jnp.exp(-x))


def _split_bf16(x):
    hi = x.astype(BF16)
    lo = (x - hi.astype(F32)).astype(BF16)
    return hi, lo


def _dot(a, b):
    return jnp.dot(a, b, preferred_element_type=F32)


def _dot_nt(a, b):
    return lax.dot_general(a, b, (((1,), (1,)), ((), ())), preferred_element_type=F32)


def _dot_tn(a, b):
    return lax.dot_general(a, b, (((0,), (0,)), ((), ())), preferred_element_type=F32)


def _proj_kernel(x_ref, g1_ref, w_ref, wg_ref, bg_ref, wo_ref, w1_ref,
                 q_ref, k_ref, la_ref, v_ref, sg_ref, u_ref, wob_ref, w1b_ref, wb_ref, wgb_ref):
    wob_ref[...] = wo_ref[...].astype(BF16)
    w1b_ref[...] = w1_ref[...].astype(BF16)

    @pl.when(pl.program_id(0) == 0)
    def _():
        wb_ref[P_Q:P_CI, :] = w_ref[OFF_Q:OFF_A, :].astype(BF16)
        wb_ref[P_CI:P_Z, :] = w_ref[OFF_C:D_IN, :].astype(BF16)
        wb_ref[P_Z:P_Z + GATE_RANK, :] = w_ref[OFF_A:OFF_C, :].astype(BF16)
        wb_ref[P_Z + GATE_RANK:D_INP, :] = jnp.zeros((Z_PAD - GATE_RANK, D_MODEL), BF16)
        wgb_ref[...] = jnp.zeros_like(wgb_ref)
        wgb_ref[0:GATE_RANK, :] = wg_ref[...].astype(BF16)

    x = x_ref[...]
    ms = jnp.mean(x * x, axis=-1, keepdims=True)
    xn = (x * lax.rsqrt(ms + EPS) * g1_ref[...]).astype(BF16)
    proj = _dot_nt(xn, wb_ref[...])
    q_ref[...] = proj[:, P_Q:P_K] * (GLA_DK ** -0.5)
    k_ref[...] = proj[:, P_K:P_V]
    v_ref[...] = proj[:, P_V:P_G].astype(BF16)
    g = proj[:, P_G:P_CI]
    sg_ref[...] = (g * _sigmoid(g)).astype(BF16)
    u_ref[...] = proj[:, P_CI:P_CG] * _sigmoid(proj[:, P_CG:P_Z])
    z = proj[:, P_Z:D_INP].astype(BF16)
    a_logit = _dot(z, wgb_ref[...]) + bg_ref[...]
    la = jnp.minimum(a_logit, 0.0) - jnp.log(1.0 + jnp.exp(-jnp.abs(a_logit)))
    la_ref[...] = la * (1.0 / GATE_TAU)


def _proj_call(x2, g1, w_in_t, w_gate_up, bg, wo, w1):
    s = x2.shape[0]
    n = s // TM_PROJ
    row = lambda i: (i, 0)
    col = lambda i: (0, i)
    const = lambda i: (0, 0)
    return pl.pallas_call(
        _proj_kernel,
        grid=(n,),
        in_specs=[
            pl.BlockSpec((TM_PROJ, D_MODEL), row),
            pl.BlockSpec((1, D_MODEL), const),
            pl.BlockSpec((None, D_IN, D_MODEL), lambda i: (0, 0, 0), pipeline_mode=pl.Buffered(1)),
            pl.BlockSpec((GATE_RANK, GLA_KEY), const),
            pl.BlockSpec((1, GLA_KEY), const),
            pl.BlockSpec((D_MODEL // n, D_MODEL), row),
            pl.BlockSpec((D_MODEL, D_FF // n), col),
        ],
        out_specs=[
            pl.BlockSpec((TM_PROJ, GLA_KEY), row),
            pl.BlockSpec((TM_PROJ, GLA_KEY), row),
            pl.BlockSpec((TM_PROJ, GLA_KEY), row),
            pl.BlockSpec((TM_PROJ, GLA_VAL), row),
            pl.BlockSpec((TM_PROJ, GLA_VAL), row),
            pl.BlockSpec((TM_PROJ, CONV_CH), row),
            pl.BlockSpec((D_MODEL // n, D_MODEL), row),
            pl.BlockSpec((D_MODEL, D_FF // n), col),
        ],
        out_shape=[
            jax.ShapeDtypeStruct((s, GLA_KEY), F32),
            jax.ShapeDtypeStruct((s, GLA_KEY), F32),
            jax.ShapeDtypeStruct((s, GLA_KEY), F32),
            jax.ShapeDtypeStruct((s, GLA_VAL), BF16),
            jax.ShapeDtypeStruct((s, GLA_VAL), BF16),
            jax.ShapeDtypeStruct((s, CONV_CH), F32),
            jax.ShapeDtypeStruct((D_MODEL, D_MODEL), BF16),
            jax.ShapeDtypeStruct((D_MODEL, D_FF), BF16),
        ],
        scratch_shapes=[
            pltpu.VMEM((D_INP, D_MODEL), BF16),
            pltpu.VMEM((Z_PAD, GLA_KEY), BF16),
        ],
        compiler_params=pltpu.CompilerParams(
            dimension_semantics=("arbitrary",), vmem_limit_bytes=VMEM_LIMIT),
        name="proj",
    )(x2, g1, w_in_t, w_gate_up, bg, wo, w1)


def _gla_kernel(q_ref, k_ref, la_ref, v_ref, sg_ref, gn_ref, w2_ref,
                o_ref, w2b_ref, st_ref, b_ref):
    c = GLA_SUB

    w2b_ref[...] = w2_ref[...].astype(BF16)

    @pl.when(pl.program_id(0) == 0)
    def _():
        st_ref[...] = jnp.zeros_like(st_ref)

    ri = lax.broadcasted_iota(jnp.int32, (c, c), 0)
    ci = lax.broadcasted_iota(jnp.int32, (c, c), 1)
    first_head = lax.broadcasted_iota(jnp.int32, (c, LANES), 1) < GLA_DK
    st_row = lax.broadcasted_iota(jnp.int32, (2 * GLA_DV, LANES), 0)
    st_lane = lax.broadcasted_iota(jnp.int32, (2 * GLA_DV, LANES), 1)
    same_head = (st_row < GLA_DV) == (st_lane < GLA_DK)

    def masked_sum(m, rows):
        mb = m.astype(BF16)
        la_hi, la_lo = _split_bf16(la_ref[rows, :])
        return _dot(mb, la_hi) + _dot(mb, la_lo)

    b_min = None
    for s in range(GLA_NSUB):
        rows = slice(s * c, (s + 1) * c)
        b = masked_sum(ci <= ri, rows)
        b_ref[rows, :] = b
        b_min = b[c - 1:c, :] if b_min is None else jnp.minimum(b_min, b[c - 1:c, :])
    fast = jnp.min(b_min) > GLA_FAST_MIN_B

    def chunk(rows, states, scores_fn):
        b = b_ref[rows, :]
        b_last = b[c - 1:c, :]
        q = q_ref[rows, :]
        k = k_ref[rows, :]
        q_in = (q * jnp.exp(b)).astype(BF16)
        k_out = (k * jnp.exp(b_last - b)).astype(BF16)
        a_chunk = jnp.exp(b_last)
        new_states = []
        for p in range(GLA_HEADS // 2):
            kc = slice(p * LANES, (p + 1) * LANES)
            v_p = v_ref[rows, 2 * p * GLA_DV:2 * (p + 1) * GLA_DV]
            st = states[p]
            o_inter = _dot_nt(q_in[:, kc], st.astype(BF16))
            for j, s_h in enumerate(scores_fn(p, q, k, b, q_in)):
                h = 2 * p + j
                dv = slice(j * GLA_DV, (j + 1) * GLA_DV)
                o_h = _dot(s_h.astype(BF16), v_p[:, dv]) + o_inter[:, dv]
                ms = jnp.mean(o_h * o_h, axis=-1, keepdims=True)
                y = o_h * lax.rsqrt(ms + EPS) * gn_ref[...]
                cols = slice(h * GLA_DV, (h + 1) * GLA_DV)
                o_ref[rows, cols] = (y * sg_ref[rows, cols].astype(F32)).astype(BF16)
            upd = _dot_tn(v_p, k_out[:, kc])
            new_states.append(a_chunk[:, kc] * st + jnp.where(same_head, upd, 0.0))
        return new_states

    def fast_scores(p, q, k, b, q_in):
        kc = slice(p * LANES, (p + 1) * LANES)
        k_div = k[:, kc] * jnp.exp(-b[:, kc])
        k_blk = jnp.concatenate([jnp.where(first_head, k_div, 0.0),
                                 jnp.where(first_head, 0.0, k_div)], axis=0).astype(BF16)
        s2 = _dot_nt(q_in[:, kc], k_blk)
        return [jnp.where(ri >= ci, s2[:, j * c:(j + 1) * c], 0.0) for j in range(2)]

    def safe_scores(rows):
        q = q_ref[rows, :]
        k = k_ref[rows, :]
        kb = k.astype(BF16)
        qb = q.astype(BF16)
        scores = []
        for h in range(GLA_HEADS):
            dk = slice(h * GLA_DK, (h + 1) * GLA_DK)
            scores.append(jnp.where(ri == ci, _dot_nt(qb[:, dk], kb[:, dk]), 0.0))
        half = c // 2
        while half >= 1:
            ref = (ri & (-2 * half)) + (half - 1)
            lo = jnp.minimum(ri, ref)
            hi = jnp.maximum(ri, ref)
            e = jnp.exp(masked_sum((ci > lo) & (ci <= hi), rows))
            q_l = (q * e).astype(BF16)
            k_l = (k * e).astype(BF16)
            pair = (((ri ^ ci) & (-2 * half)) == 0) & ((ri & half) != 0) & ((ci & half) == 0)
            for h in range(GLA_HEADS):
                dk = slice(h * GLA_DK, (h + 1) * GLA_DK)
                scores[h] = jnp.where(pair, _dot_nt(q_l[:, dk], k_l[:, dk]), scores[h])
            half //= 2
        return scores

    @pl.when(fast)
    def _():
        states = [st_ref[p] for p in range(GLA_HEADS // 2)]
        for s in range(GLA_NSUB):
            states = chunk(slice(s * c, (s + 1) * c), states, fast_scores)
        for p in range(GLA_HEADS // 2):
            st_ref[p] = states[p]

    @pl.when(jnp.logical_not(fast))
    def _():
        def body(s, carry):
            rows = pl.ds(pl.multiple_of(s * c, c), c)
            scores = safe_scores(rows)
            states = chunk(rows, [st_ref[p] for p in range(GLA_HEADS // 2)],
                           lambda p, *_: scores[2 * p:2 * p + 2])
            for p in range(GLA_HEADS // 2):
                st_ref[p] = states[p]
            return carry

        lax.fori_loop(0, GLA_NSUB, body, 0)


def _gla_call(q, k, la, v, sg, gn, w2):
    s = q.shape[0]
    n = s // TM_GLA
    row = lambda i: (i, 0)
    const = lambda i: (0, 0)
    return pl.pallas_call(
        _gla_kernel,
        grid=(n,),
        in_specs=[
            pl.BlockSpec((TM_GLA, GLA_KEY), row),
            pl.BlockSpec((TM_GLA, GLA_KEY), row),
            pl.BlockSpec((TM_GLA, GLA_KEY), row),
            pl.BlockSpec((TM_GLA, GLA_VAL), row),
            pl.BlockSpec((TM_GLA, GLA_VAL), row),
            pl.BlockSpec((1, GLA_DV), const),
            pl.BlockSpec((D_FF // n, D_MODEL), row),
        ],
        out_specs=[
            pl.BlockSpec((TM_GLA, GLA_VAL), row),
            pl.BlockSpec((D_FF // n, D_MODEL), row),
        ],
        out_shape=[
            jax.ShapeDtypeStruct((s, GLA_VAL), BF16),
            jax.ShapeDtypeStruct((D_FF, D_MODEL), BF16),
        ],
        scratch_shapes=[
            pltpu.VMEM((GLA_HEADS // 2, 2 * GLA_DV, LANES), F32),
            pltpu.VMEM((TM_GLA, GLA_KEY), F32),
        ],
        compiler_params=pltpu.CompilerParams(
            dimension_semantics=("arbitrary",), vmem_limit_bytes=VMEM_LIMIT),
        name="gla",
    )(q, k, la, v, sg, gn, w2)


def _group_mean(t, low_half, size):
    lo = jnp.sum(jnp.where(low_half, t, 0.0), axis=-1, keepdims=True)
    hi = jnp.sum(jnp.where(low_half, 0.0, t), axis=-1, keepdims=True)
    return jnp.where(low_half, lo, hi) * (1.0 / size)


def _interleave(mxu_items, vec_items):
    done = 0
    for i, item in enumerate(mxu_items):
        item()
        upto = (i + 1) * len(vec_items) // len(mxu_items)
        for v_item in vec_items[done:upto]:
            v_item()
        done = upto


def _mlp_conv_kernel(x_ref, og_ref, un_ref, u0_ref, cw_ref, cb_ref, gg_ref, gb_ref,
                     wo_hbm, g2_ref, w1_hbm, w2_hbm, gf_ref,
                     o_ref, ext_ref, sh_ref, oc_ref, f_ref, wo_ref, w1_ref, w2_ref, w_sem):
    tm = TM_MLP

    def weight_copies():
        return [pltpu.make_async_copy(src, dst, w_sem.at[j])
                for j, (src, dst) in enumerate(((wo_hbm, wo_ref), (w1_hbm, w1_ref), (w2_hbm, w2_ref)))]

    grp = CONV_CH // CONV_GROUPS
    low_half = lax.broadcasted_iota(jnp.int32, (CONV_RB, LANES), 1) < grp
    first = CONV_HALO - (CONV_W - 1)

    def shifted_copy(p):
        sh_ref[p - 1] = ext_ref[p:p + CONV_SH_ROWS, :]

    def conv_block(base):
        for cg in range(CONV_CH // LANES):
            cols = slice(cg * LANES, (cg + 1) * LANES)
            acc = jnp.broadcast_to(cb_ref[:, cols], (CONV_RB, LANES))
            for t in range(CONV_W):
                shift = first + t
                p = shift % SUBLANES
                rows = pl.ds(base + (shift - p), CONV_RB)
                tap = ext_ref[rows, cols] if p == 0 else sh_ref[p - 1, rows, cols]
                acc = acc + cw_ref[t:t + 1, cols] * tap
            d = acc - _group_mean(acc, low_half, grp)
            var = _group_mean(d * d, low_half, grp)
            yn = d * lax.rsqrt(var + EPS) * gg_ref[:, cols] + gb_ref[:, cols]
            oc_ref[pl.ds(base, CONV_RB), cols] = (yn * _sigmoid(yn)).astype(BF16)

    @pl.when(pl.program_id(0) == 0)
    def _():
        for cp in weight_copies():
            cp.start()
        ext_ref[0:CONV_HALO, :] = jnp.zeros((CONV_HALO, CONV_CH), F32)
        ext_ref[CONV_HALO:, :] = u0_ref[...]
        for p in range(1, SUBLANES):
            shifted_copy(p)

        def body(r, carry):
            conv_block(pl.multiple_of(r * CONV_RB, CONV_RB))
            return carry

        lax.fori_loop(0, tm // CONV_RB, body, 0)
        for cp in weight_copies():
            cp.wait()

    h = x_ref[...] + _dot(og_ref[...], wo_ref[0:GLA_VAL, :]) + _dot(oc_ref[...], wo_ref[GLA_VAL:, :])
    ms = jnp.mean(h * h, axis=-1, keepdims=True)
    hn = (h * lax.rsqrt(ms + EPS) * g2_ref[...]).astype(BF16)

    def roll_window():
        ext_ref[0:CONV_HALO, :] = ext_ref[tm:tm + CONV_HALO, :]
        ext_ref[CONV_HALO:, :] = un_ref[...]

    vec_items = [roll_window] + [functools.partial(shifted_copy, p) for p in range(1, SUBLANES)]
    vec_items += [functools.partial(conv_block, r * CONV_RB) for r in range(tm // CONV_RB)]

    def ff_in(j):
        cols = slice(j * FF_CHUNK, (j + 1) * FF_CHUNK)
        f = jnp.maximum(_dot(hn, w1_ref[:, cols]), 0.0)
        f_ref[:, cols] = (f * f).astype(BF16)

    def ff_out(j):
        cols = slice(j * OUT_CHUNK, (j + 1) * OUT_CHUNK)
        o_ref[:, cols] = h[:, cols] + _dot(f_ref[...], w2_ref[:, cols])

    mxu_items = [functools.partial(ff_in, j) for j in range(D_FF // FF_CHUNK)]
    mxu_items += [functools.partial(ff_out, j) for j in range(D_MODEL // OUT_CHUNK)]
    _interleave(mxu_items, vec_items)

    acc = o_ref[...]
    ms2 = jnp.mean(acc * acc, axis=-1, keepdims=True)
    o_ref[...] = acc * lax.rsqrt(ms2 + EPS) * gf_ref[...]


def _mlp_conv_call(x2, og, u, cw, cb, gg, gb, wo, g2, w1, w2, gf):
    s = x2.shape[0]
    n = s // TM_MLP
    row = lambda i: (i, 0)
    nxt = lambda i: (jnp.minimum(i + 1, n - 1), 0)
    const = lambda i: (0, 0)
    single = pl.Buffered(1)
    return pl.pallas_call(
        _mlp_conv_kernel,
        grid=(n,),
        in_specs=[
            pl.BlockSpec((TM_MLP, D_MODEL), row),
            pl.BlockSpec((TM_MLP, GLA_VAL), row),
            pl.BlockSpec((TM_MLP, CONV_CH), nxt),
            pl.BlockSpec((TM_MLP, CONV_CH), const, pipeline_mode=single),
            pl.BlockSpec((CONV_W, CONV_CH), const),
            pl.BlockSpec((1, CONV_CH), const),
            pl.BlockSpec((1, CONV_CH), const),
            pl.BlockSpec((1, CONV_CH), const),
            pl.BlockSpec(memory_space=pl.ANY),
            pl.BlockSpec((1, D_MODEL), const),
            pl.BlockSpec(memory_space=pl.ANY),
            pl.BlockSpec(memory_space=pl.ANY),
            pl.BlockSpec((1, D_MODEL), const),
        ],
        out_specs=pl.BlockSpec((TM_MLP, D_MODEL), row),
        out_shape=jax.ShapeDtypeStruct((s, D_MODEL), F32),
        scratch_shapes=[
            pltpu.VMEM((TM_MLP + CONV_HALO, CONV_CH), F32),
            pltpu.VMEM((SUBLANES - 1, CONV_SH_ROWS, CONV_CH), F32),
            pltpu.VMEM((TM_MLP, CONV_CH), BF16),
            pltpu.VMEM((TM_MLP, D_FF), BF16),
            pltpu.VMEM((D_MODEL, D_MODEL), BF16),
            pltpu.VMEM((D_MODEL, D_FF), BF16),
            pltpu.VMEM((D_FF, D_MODEL), BF16),
            pltpu.SemaphoreType.DMA((3,)),
        ],
        compiler_params=pltpu.CompilerParams(
            dimension_semantics=("arbitrary",), vmem_limit_bytes=VMEM_LIMIT),
        name="mlp_conv",
    )(x2, og, u, u, cw, cb, gg, gb, wo, g2, w1, w2, gf)


def kernel(x, norm1_g, w_in, w_gate_up, b_gate, gla_norm_g, conv_w, conv_b, conv_norm_g,
           conv_norm_b, w_out, norm2_g, w_mlp_in, w_mlp_out, final_norm_g):
    bsz, seq, _ = x.shape
    x2 = x.reshape(bsz * seq, D_MODEL)
    assert bsz == 1, "state / halo carry across grid steps assumes one sequence"
    l = 0
    assert w_in.shape[0] == 1, "single layer: the projection weight is passed with its layer axis"
    w_in_t = jnp.swapaxes(w_in, 1, 2)
    q, k, la, v, sg, u, wo_b, w1_b = _proj_call(
        x2, norm1_g[l][None, :], w_in_t, w_gate_up[l], b_gate[l][None, :], w_out[l], w_mlp_in[l])
    o_gla, w2_b = _gla_call(q, k, la, v, sg, gla_norm_g[l][None, :], w_mlp_out[l])
    out = _mlp_conv_call(x2, o_gla, u, conv_w[l], conv_b[l][None, :], conv_norm_g[l][None, :],
                         conv_norm_b[l][None, :], wo_b, norm2_g[l][None, :], w1_b, w2_b,
                         final_norm_g[None, :])
    return out.reshape(bsz, seq, D_MODEL)
```

```python
import functools

import jax
import jax.numpy as jnp
from jax import lax
from jax.experimental import pallas as pl
from jax.experimental.pallas import tpu as pltpu

D_MODEL = 1024
GLA_HEADS = 4
GLA_DV = 128
GLA_DK = 64
GLA_KEY = GLA_HEADS * GLA_DK
GLA_VAL = GLA_HEADS * GLA_DV
GATE_RANK = 16
GATE_TAU = 16.0
CONV_CH = 512
CONV_GROUPS = 8
CONV_W = 31
D_FF = 4 * D_MODEL
EPS = 1e-6

OFF_Q = 0
OFF_K = OFF_Q + GLA_KEY
OFF_V = OFF_K + GLA_KEY
OFF_G = OFF_V + GLA_VAL
OFF_A = OFF_G + GLA_VAL
OFF_C = OFF_A + GATE_RANK
D_IN = OFF_C + 2 * CONV_CH

LANES = 128
SUBLANES = 8
Z_PAD = LANES
P_Q = 0
P_K = P_Q + GLA_KEY
P_V = P_K + GLA_KEY
P_G = P_V + GLA_VAL
P_CI = P_G + GLA_VAL
P_CG = P_CI + CONV_CH
P_Z = P_CG + CONV_CH
D_INP = P_Z + Z_PAD

TM_PROJ = 1024
GLA_SUB = 256
GLA_NSUB = 8
TM_GLA = GLA_SUB * GLA_NSUB
TM_MLP = 512
FF_CHUNK = 512
OUT_CHUNK = 256
CONV_HALO = 32
CONV_RB = 32
CONV_SH_ROWS = TM_MLP + CONV_HALO - SUBLANES
GLA_FAST_MIN_B = -60.0

VMEM_LIMIT = 56 * 1024 * 1024

F32 = jnp.float32
BF16 = jnp.bfloat16


def _sigmoid(x):
    return 1.0 / (1.0 + jnp.exp(-x))


def _split_bf16(x):
    hi = x.astype(BF16)
    lo = (x - hi.astype(F32)).astype(BF16)
    return hi, lo


def _dot(a, b):
    return jnp.dot(a, b, preferred_element_type=F32)


def _dot_nt(a, b):
    return lax.dot_general(a, b, (((1,), (1,)), ((), ())), preferred_element_type=F32)


def _dot_tn(a, b):
    return lax.dot_general(a, b, (((0,), (0,)), ((), ())), preferred_element_type=F32)


def _proj_kernel(x_ref, g1_ref, w_ref, wg_ref, bg_ref, wo_ref, w1_ref,
                 q_ref, k_ref, la_ref, v_ref, sg_ref, u_ref, wob_ref, w1b_ref, wb_ref, wgb_ref):
    wob_ref[...] = wo_ref[...].astype(BF16)
    w1b_ref[...] = w1_ref[...].astype(BF16)

    @pl.when(pl.program_id(0) == 0)
    def _():
        wb_ref[P_Q:P_CI, :] = w_ref[OFF_Q:OFF_A, :].astype(BF16)
        wb_ref[P_CI:P_Z, :] = w_ref[OFF_C:D_IN, :].astype(BF16)
        wb_ref[P_Z:P_Z + GATE_RANK, :] = w_ref[OFF_A:OFF_C, :].astype(BF16)
        wb_ref[P_Z + GATE_RANK:D_INP, :] = jnp.zeros((Z_PAD - GATE_RANK, D_MODEL), BF16)
        wgb_ref[...] = jnp.zeros_like(wgb_ref)
        wgb_ref[0:GATE_RANK, :] = wg_ref[...].astype(BF16)

    x = x_ref[...]
    ms = jnp.mean(x * x, axis=-1, keepdims=True)
    xn = (x * lax.rsqrt(ms + EPS) * g1_ref[...]).astype(BF16)
    proj = _dot_nt(xn, wb_ref[...])
    q_ref[...] = proj[:, P_Q:P_K] * (GLA_DK ** -0.5)
    k_ref[...] = proj[:, P_K:P_V]
    v_ref[...] = proj[:, P_V:P_G].astype(BF16)
    g = proj[:, P_G:P_CI]
    sg_ref[...] = (g * _sigmoid(g)).astype(BF16)
    u_ref[...] = proj[:, P_CI:P_CG] * _sigmoid(proj[:, P_CG:P_Z])
    z = proj[:, P_Z:D_INP].astype(BF16)
    a_logit = _dot(z, wgb_ref[...]) + bg_ref[...]
    la = jnp.minimum(a_logit, 0.0) - jnp.log(1.0 + jnp.exp(-jnp.abs(a_logit)))
    la_ref[...] = la * (1.0 / GATE_TAU)


def _proj_call(x2, g1, w_in_t, w_gate_up, bg, wo, w1):
    s = x2.shape[0]
    n = s // TM_PROJ
    row = lambda i: (i, 0)
    col = lambda i: (0, i)
    const = lambda i: (0, 0)
    return pl.pallas_call(
        _proj_kernel,
        grid=(n,),
        in_specs=[
            pl.BlockSpec((TM_PROJ, D_MODEL), row),
            pl.BlockSpec((1, D_MODEL), const),
            pl.BlockSpec((None, D_IN, D_MODEL), lambda i: (0, 0, 0), pipeline_mode=pl.Buffered(1)),
            pl.BlockSpec((GATE_RANK, GLA_KEY), const),
            pl.BlockSpec((1, GLA_KEY), const),
            pl.BlockSpec((D_MODEL // n, D_MODEL), row),
            pl.BlockSpec((D_MODEL, D_FF // n), col),
        ],
        out_specs=[
            pl.BlockSpec((TM_PROJ, GLA_KEY), row),
            pl.BlockSpec((TM_PROJ, GLA_KEY), row),
            pl.BlockSpec((TM_PROJ, GLA_KEY), row),
            pl.BlockSpec((TM_PROJ, GLA_VAL), row),
            pl.BlockSpec((TM_PROJ, GLA_VAL), row),
            pl.BlockSpec((TM_PROJ, CONV_CH), row),
            pl.BlockSpec((D_MODEL // n, D_MODEL), row),
            pl.BlockSpec((D_MODEL, D_FF // n), col),
        ],
        out_shape=[
            jax.ShapeDtypeStruct((s, GLA_KEY), F32),
            jax.ShapeDtypeStruct((s, GLA_KEY), F32),
            jax.ShapeDtypeStruct((s, GLA_KEY), F32),
            jax.ShapeDtypeStruct((s, GLA_VAL), BF16),
            jax.ShapeDtypeStruct((s, GLA_VAL), BF16),
            jax.ShapeDtypeStruct((s, CONV_CH), F32),
            jax.ShapeDtypeStruct((D_MODEL, D_MODEL), BF16),
            jax.ShapeDtypeStruct((D_MODEL, D_FF), BF16),
        ],
        scratch_shapes=[
            pltpu.VMEM((D_INP, D_MODEL), BF16),
            pltpu.VMEM((Z_PAD, GLA_KEY), BF16),
        ],
        compiler_params=pltpu.CompilerParams(
            dimension_semantics=("arbitrary",), vmem_limit_bytes=VMEM_LIMIT),
        name="proj",
    )(x2, g1, w_in_t, w_gate_up, bg, wo, w1)


def _gla_kernel(q_ref, k_ref, la_ref, v_ref, sg_ref, gn_ref, w2_ref,
                o_ref, w2b_ref, st_ref, b_ref):
    c = GLA_SUB

    w2b_ref[...] = w2_ref[...].astype(BF16)

    @pl.when(pl.program_id(0) == 0)
    def _():
        st_ref[...] = jnp.zeros_like(st_ref)

    ri = lax.broadcasted_iota(jnp.int32, (c, c), 0)
    ci = lax.broadcasted_iota(jnp.int32, (c, c), 1)
    first_head = lax.broadcasted_iota(jnp.int32, (c, LANES), 1) < GLA_DK
    st_row = lax.broadcasted_iota(jnp.int32, (2 * GLA_DV, LANES), 0)
    st_lane = lax.broadcasted_iota(jnp.int32, (2 * GLA_DV, LANES), 1)
    same_head = (st_row < GLA_DV) == (st_lane < GLA_DK)

    def masked_sum(m, rows):
        mb = m.astype(BF16)
        la_hi, la_lo = _split_bf16(la_ref[rows, :])
        return _dot(mb, la_hi) + _dot(mb, la_lo)

    b_min = None
    for s in range(GLA_NSUB):
        rows = slice(s * c, (s + 1) * c)
        b = masked_sum(ci <= ri, rows)
        b_ref[rows, :] = b
        b_min = b[c - 1:c, :] if b_min is None else jnp.minimum(b_min, b[c - 1:c, :])
    fast = jnp.min(b_min) > GLA_FAST_MIN_B

    def chunk(rows, states, scores_fn):
        b = b_ref[rows, :]
        b_last = b[c - 1:c, :]
        q = q_ref[rows, :]
        k = k_ref[rows, :]
        q_in = (q * jnp.exp(b)).astype(BF16)
        k_out = (k * jnp.exp(b_last - b)).astype(BF16)
        a_chunk = jnp.exp(b_last)
        new_states = []
        for p in range(GLA_HEADS // 2):
            kc = slice(p * LANES, (p + 1) * LANES)
            v_p = v_ref[rows, 2 * p * GLA_DV:2 * (p + 1) * GLA_DV]
            st = states[p]
            o_inter = _dot_nt(q_in[:, kc], st.astype(BF16))
            for j, s_h in enumerate(scores_fn(p, q, k, b, q_in)):
                h = 2 * p + j
                dv = slice(j * GLA_DV, (j + 1) * GLA_DV)
                o_h = _dot(s_h.astype(BF16), v_p[:, dv]) + o_inter[:, dv]
                ms = jnp.mean(o_h * o_h, axis=-1, keepdims=True)
                y = o_h * lax.rsqrt(ms + EPS) * gn_ref[...]
                cols = slice(h * GLA_DV, (h + 1) * GLA_DV)
                o_ref[rows, cols] = (y * sg_ref[rows, cols].astype(F32)).astype(BF16)
            upd = _dot_tn(v_p, k_out[:, kc])
            new_states.append(a_chunk[:, kc] * st + jnp.where(same_head, upd, 0.0))
        return new_states

    def fast_scores(p, q, k, b, q_in):
        kc = slice(p * LANES, (p + 1) * LANES)
        k_div = k[:, kc] * jnp.exp(-b[:, kc])
        k_blk = jnp.concatenate([jnp.where(first_head, k_div, 0.0),
                                 jnp.where(first_head, 0.0, k_div)], axis=0).astype(BF16)
        s2 = _dot_nt(q_in[:, kc], k_blk)
        return [jnp.where(ri >= ci, s2[:, j * c:(j + 1) * c], 0.0) for j in range(2)]

    def safe_scores(rows):
        q = q_ref[rows, :]
        k = k_ref[rows, :]
        kb = k.astype(BF16)
        qb = q.astype(BF16)
        scores = []
        for h in range(GLA_HEADS):
            dk = slice(h * GLA_DK, (h + 1) * GLA_DK)
            scores.append(jnp.where(ri == ci, _dot_nt(qb[:, dk], kb[:, dk]), 0.0))
        half = c // 2
        while half >= 1:
            ref = (ri & (-2 * half)) + (half - 1)
            lo = jnp.minimum(ri, ref)
            hi = jnp.maximum(ri, ref)
            e = jnp.exp(masked_sum((ci > lo) & (ci <= hi), rows))
            q_l = (q * e).astype(BF16)
            k_l = (k * e).astype(BF16)
            pair = (((ri ^ ci) & (-2 * half)) == 0) & ((ri & half) != 0) & ((ci & half) == 0)
            for h in range(GLA_HEADS):
                dk = slice(h * GLA_DK, (h + 1) * GLA_DK)
                scores[h] = jnp.where(pair, _dot_nt(q_l[:, dk], k_l[:, dk]), scores[h])
            half //= 2
        return scores

    @pl.when(fast)
    def _():
        states = [st_ref[p] for p in range(GLA_HEADS // 2)]
        for s in range(GLA_NSUB):
            states = chunk(slice(s * c, (s + 1) * c), states, fast_scores)
        for p in range(GLA_HEADS // 2):
            st_ref[p] = states[p]

    @pl.when(jnp.logical_not(fast))
    def _():
        def body(s, carry):
            rows = pl.ds(pl.multiple_of(s * c, c), c)
            scores = safe_scores(rows)
            states = chunk(rows, [st_ref[p] for p in range(GLA_HEADS // 2)],
                           lambda p, *_: scores[2 * p:2 * p + 2])
            for p in range(GLA_HEADS // 2):
                st_ref[p] = states[p]
            return carry

        lax.fori_loop(0, GLA_NSUB, body, 0)


def _gla_call(q, k, la, v, sg, gn, w2):
    s = q.shape[0]
    n = s // TM_GLA
    row = lambda i: (i, 0)
    const = lambda i: (0, 0)
    return pl.pallas_call(
        _gla_kernel,
        grid=(n,),
        in_specs=[
            pl.BlockSpec((TM_GLA, GLA_KEY), row),
            pl.BlockSpec((TM_GLA, GLA_KEY), row),
            pl.BlockSpec((TM_GLA, GLA_KEY), row),
            pl.BlockSpec((TM_GLA, GLA_VAL), row),
            pl.BlockSpec((TM_GLA, GLA_VAL), row),
            pl.BlockSpec((1, GLA_DV), const),
            pl.BlockSpec((D_FF // n, D_MODEL), row),
        ],
        out_specs=[
            pl.BlockSpec((TM_GLA, GLA_VAL), row),
            pl.BlockSpec((D_FF // n, D_MODEL), row),
        ],
        out_shape=[
            jax.ShapeDtypeStruct((s, GLA_VAL), BF16),
            jax.ShapeDtypeStruct((D_FF, D_MODEL), BF16),
        ],
        scratch_shapes=[
            pltpu.VMEM((GLA_HEADS // 2, 2 * GLA_DV, LANES), F32),
            pltpu.VMEM((TM_GLA, GLA_KEY), F32),
        ],
        compiler_params=pltpu.CompilerParams(
            dimension_semantics=("arbitrary",), vmem_limit_bytes=VMEM_LIMIT),
        name="gla",
    )(q, k, la, v, sg, gn, w2)


def _group_mean(t, low_half, size):
    lo = jnp.sum(jnp.where(low_half, t, 0.0), axis=-1, keepdims=True)
    hi = jnp.sum(jnp.where(low_half, 0.0, t), axis=-1, keepdims=True)
    return jnp.where(low_half, lo, hi) * (1.0 / size)


def _interleave(mxu_items, vec_items):
    done = 0
    for i, item in enumerate(mxu_items):
        item()
        upto = (i + 1) * len(vec_items) // len(mxu_items)
        for v_item in vec_items[done:upto]:
            v_item()
        done = upto


def _mlp_conv_kernel(x_ref, og_ref, un_ref, u0_ref, cw_ref, cb_ref, gg_ref, gb_ref,
                     wo_hbm, g2_ref, w1_hbm, w2_hbm, gf_ref,
                     o_ref, ext_ref, sh_ref, oc_ref, f_ref, wo_ref, w1_ref, w2_ref, w_sem):
    tm = TM_MLP

    def weight_copies():
        return [pltpu.make_async_copy(src, dst, w_sem.at[j])
                for j, (src, dst) in enumerate(((wo_hbm, wo_ref), (w1_hbm, w1_ref), (w2_hbm, w2_ref)))]

    grp = CONV_CH // CONV_GROUPS
    low_half = lax.broadcasted_iota(jnp.int32, (CONV_RB, LANES), 1) < grp
    first = CONV_HALO - (CONV_W - 1)

    def shifted_copy(p):
        sh_ref[p - 1] = ext_ref[p:p + CONV_SH_ROWS, :]

    def conv_block(base):
        for cg in range(CONV_CH // LANES):
            cols = slice(cg * LANES, (cg + 1) * LANES)
            acc = jnp.broadcast_to(cb_ref[:, cols], (CONV_RB, LANES))
            for t in range(CONV_W):
                shift = first + t
                p = shift % SUBLANES
                rows = pl.ds(base + (shift - p), CONV_RB)
                tap = ext_ref[rows, cols] if p == 0 else sh_ref[p - 1, rows, cols]
                acc = acc + cw_ref[t:t + 1, cols] * tap
            d = acc - _group_mean(acc, low_half, grp)
            var = _group_mean(d * d, low_half, grp)
            yn = d * lax.rsqrt(var + EPS) * gg_ref[:, cols] + gb_ref[:, cols]
            oc_ref[pl.ds(base, CONV_RB), cols] = (yn * _sigmoid(yn)).astype(BF16)

    @pl.when(pl.program_id(0) == 0)
    def _():
        for cp in weight_copies():
            cp.start()
        ext_ref[0:CONV_HALO, :] = jnp.zeros((CONV_HALO, CONV_CH), F32)
        ext_ref[CONV_HALO:, :] = u0_ref[...]
        for p in range(1, SUBLANES):
            shifted_copy(p)

        def body(r, carry):
            conv_block(pl.multiple_of(r * CONV_RB, CONV_RB))
            return carry

        lax.fori_loop(0, tm // CONV_RB, body, 0)
        for cp in weight_copies():
            cp.wait()

    h = x_ref[...] + _dot(og_ref[...], wo_ref[0:GLA_VAL, :]) + _dot(oc_ref[...], wo_ref[GLA_VAL:, :])
    hg = (h * g2_ref[...]).astype(BF16)
    r = lax.rsqrt(jnp.mean(h * h, axis=-1, keepdims=True) + EPS)
    r2 = r * r

    def roll_window():
        ext_ref[0:CONV_HALO, :] = ext_ref[tm:tm + CONV_HALO, :]
        ext_ref[CONV_HALO:, :] = un_ref[...]

    vec_items = [roll_window] + [functools.partial(shifted_copy, p) for p in range(1, SUBLANES)]
    vec_items += [functools.partial(conv_block, r * CONV_RB) for r in range(tm // CONV_RB)]

    def ff_in(j):
        cols = slice(j * FF_CHUNK, (j + 1) * FF_CHUNK)
        f = jnp.maximum(_dot(hg, w1_ref[:, cols]), 0.0)
        f_ref[:, cols] = (f * f * r2).astype(BF16)

    def ff_out(j):
        cols = slice(j * OUT_CHUNK, (j + 1) * OUT_CHUNK)
        o_ref[:, cols] = h[:, cols] + _dot(f_ref[...], w2_ref[:, cols])

    mxu_items = [functools.partial(ff_in, j) for j in range(D_FF // FF_CHUNK)]
    mxu_items += [functools.partial(ff_out, j) for j in range(D_MODEL // OUT_CHUNK)]
    _interleave(mxu_items, vec_items)

    acc = o_ref[...]
    ms2 = jnp.mean(acc * acc, axis=-1, keepdims=True)
    o_ref[...] = acc * lax.rsqrt(ms2 + EPS) * gf_ref[...]


def _mlp_conv_call(x2, og, u, cw, cb, gg, gb, wo, g2, w1, w2, gf):
    s = x2.shape[0]
    n = s // TM_MLP
    row = lambda i: (i, 0)
    nxt = lambda i: (jnp.minimum(i + 1, n - 1), 0)
    const = lambda i: (0, 0)
    single = pl.Buffered(1)
    return pl.pallas_call(
        _mlp_conv_kernel,
        grid=(n,),
        in_specs=[
            pl.BlockSpec((TM_MLP, D_MODEL), row),
            pl.BlockSpec((TM_MLP, GLA_VAL), row),
            pl.BlockSpec((TM_MLP, CONV_CH), nxt),
            pl.BlockSpec((TM_MLP, CONV_CH), const, pipeline_mode=single),
            pl.BlockSpec((CONV_W, CONV_CH), const),
            pl.BlockSpec((1, CONV_CH), const),
            pl.BlockSpec((1, CONV_CH), const),
            pl.BlockSpec((1, CONV_CH), const),
            pl.BlockSpec(memory_space=pl.ANY),
            pl.BlockSpec((1, D_MODEL), const),
            pl.BlockSpec(memory_space=pl.ANY),
            pl.BlockSpec(memory_space=pl.ANY),
            pl.BlockSpec((1, D_MODEL), const),
        ],
        out_specs=pl.BlockSpec((TM_MLP, D_MODEL), row),
        out_shape=jax.ShapeDtypeStruct((s, D_MODEL), F32),
        scratch_shapes=[
            pltpu.VMEM((TM_MLP + CONV_HALO, CONV_CH), F32),
            pltpu.VMEM((SUBLANES - 1, CONV_SH_ROWS, CONV_CH), F32),
            pltpu.VMEM((TM_MLP, CONV_CH), BF16),
            pltpu.VMEM((TM_MLP, D_FF), BF16),
            pltpu.VMEM((D_MODEL, D_MODEL), BF16),
            pltpu.VMEM((D_MODEL, D_FF), BF16),
            pltpu.VMEM((D_FF, D_MODEL), BF16),
            pltpu.SemaphoreType.DMA((3,)),
        ],
        compiler_params=pltpu.CompilerParams(
            dimension_semantics=("arbitrary",), vmem_limit_bytes=VMEM_LIMIT),
        name="mlp_conv",
    )(x2, og, u, u, cw, cb, gg, gb, wo, g2, w1, w2, gf)


def kernel(x, norm1_g, w_in, w_gate_up, b_gate, gla_norm_g, conv_w, conv_b, conv_norm_g,
           conv_norm_b, w_out, norm2_g, w_mlp_in, w_mlp_out, final_norm_g):
    bsz, seq, _ = x.shape
    x2 = x.reshape(bsz * seq, D_MODEL)
    assert bsz == 1, "state / halo carry across grid steps assumes one sequence"
    l = 0
    assert w_in.shape[0] == 1, "single layer: the projection weight is passed with its layer axis"
    w_in_t = jnp.swapaxes(w_in, 1, 2)
    q, k, la, v, sg, u, wo_b, w1_b = _proj_call(
        x2, norm1_g[l][None, :], w_in_t, w_gate_up[l], b_gate[l][None, :], w_out[l], w_mlp_in[l])
    o_gla, w2_b = _gla_call(q, k, la, v, sg, gla_norm_g[l][None, :], w_mlp_out[l])
    out = _mlp_conv_call(x2, o_gla, u, conv_w[l], conv_b[l][None, :], conv_norm_g[l][None, :],
                         conv_norm_b[l][None, :], wo_b, norm2_g[l][None, :], w1_b, w2_b,
                         final_norm_g[None, :])
    return out.reshape(bsz, seq, D_MODEL)
```

```python
import functools

import jax
import jax.numpy as jnp
from jax import lax
from jax.experimental import pallas as pl
from jax.experimental.pallas import tpu as pltpu

D_MODEL = 1024
GLA_HEADS = 4
GLA_DV = 128
GLA_DK = 64
GLA_KEY = GLA_HEADS * GLA_DK
GLA_VAL = GLA_HEADS * GLA_DV
GATE_RANK = 16
GATE_TAU = 16.0
CONV_CH = 512
CONV_GROUPS = 8
CONV_W = 31
D_FF = 4 * D_MODEL
EPS = 1e-6

OFF_Q = 0
OFF_K = OFF_Q + GLA_KEY
OFF_V = OFF_K + GLA_KEY
OFF_G = OFF_V + GLA_VAL
OFF_A = OFF_G + GLA_VAL
OFF_C = OFF_A + GATE_RANK
D_IN = OFF_C + 2 * CONV_CH

LANES = 128
SUBLANES = 8
Z_PAD = LANES
P_Q = 0
P_K = P_Q + GLA_KEY
P_V = P_K + GLA_KEY
P_G = P_V + GLA_VAL
P_CI = P_G + GLA_VAL
P_CG = P_CI + CONV_CH
P_Z = P_CG + CONV_CH
D_INP = P_Z + Z_PAD

TM_PROJ = 1024
GLA_SUB = 256
GLA_NSUB = 8
TM_GLA = GLA_SUB * GLA_NSUB
TM_MLP = 512
FF_CHUNK = 512
OUT_CHUNK = 256
CONV_HALO = 32
CONV_RB = 32
CONV_SH_ROWS = TM_MLP + CONV_HALO - SUBLANES
GLA_FAST_MIN_B = -60.0

VMEM_LIMIT = 56 * 1024 * 1024

F32 = jnp.float32
BF16 = jnp.bfloat16


def _sigmoid(x):
    return 1.0 / (1.0 + jnp.exp(-x))


def _split_bf16(x):
    hi = x.astype(BF16)
    lo = (x - hi.astype(F32)).astype(BF16)
    return hi, lo


def _dot(a, b):
    return jnp.dot(a, b, preferred_element_type=F32)


def _dot_nt(a, b):
    return lax.dot_general(a, b, (((1,), (1,)), ((), ())), preferred_element_type=F32)


def _dot_tn(a, b):
    return lax.dot_general(a, b, (((0,), (0,)), ((), ())), preferred_element_type=F32)


def _proj_kernel(x_ref, g1_ref, w_ref, wg_ref, bg_ref, wo_ref, w1_ref,
                 q_ref, k_ref, la_ref, v_ref, sg_ref, u_ref, wob_ref, w1b_ref, wb_ref, wgb_ref):
    wob_ref[...] = wo_ref[...].astype(BF16)
    w1b_ref[...] = w1_ref[...].astype(BF16)

    @pl.when(pl.program_id(0) == 0)
    def _():
        wb_ref[P_Q:P_CI, :] = w_ref[OFF_Q:OFF_A, :].astype(BF16)
        wb_ref[P_CI:P_Z, :] = w_ref[OFF_C:D_IN, :].astype(BF16)
        wb_ref[P_Z:P_Z + GATE_RANK, :] = w_ref[OFF_A:OFF_C, :].astype(BF16)
        wb_ref[P_Z + GATE_RANK:D_INP, :] = jnp.zeros((Z_PAD - GATE_RANK, D_MODEL), BF16)
        wgb_ref[...] = jnp.zeros_like(wgb_ref)
        wgb_ref[0:GATE_RANK, :] = wg_ref[...].astype(BF16)

    x = x_ref[...]
    r = lax.rsqrt(jnp.mean(x * x, axis=-1, keepdims=True) + EPS)
    proj = _dot_nt((x * g1_ref[...]).astype(BF16), wb_ref[...])
    q_ref[...] = proj[:, P_Q:P_K] * (r * (GLA_DK ** -0.5))
    k_ref[...] = proj[:, P_K:P_V] * r
    v_ref[...] = (proj[:, P_V:P_G] * r).astype(BF16)
    g = proj[:, P_G:P_CI] * r
    sg_ref[...] = (g * _sigmoid(g)).astype(BF16)
    u_ref[...] = (proj[:, P_CI:P_CG] * r) * _sigmoid(proj[:, P_CG:P_Z] * r)
    z = (proj[:, P_Z:D_INP] * r).astype(BF16)
    a_logit = _dot(z, wgb_ref[...]) + bg_ref[...]
    la = jnp.minimum(a_logit, 0.0) - jnp.log(1.0 + jnp.exp(-jnp.abs(a_logit)))
    la_ref[...] = la * (1.0 / GATE_TAU)


def _proj_call(x2, g1, w_in_t, w_gate_up, bg, wo, w1):
    s = x2.shape[0]
    n = s // TM_PROJ
    row = lambda i: (i, 0)
    col = lambda i: (0, i)
    const = lambda i: (0, 0)
    return pl.pallas_call(
        _proj_kernel,
        grid=(n,),
        in_specs=[
            pl.BlockSpec((TM_PROJ, D_MODEL), row),
            pl.BlockSpec((1, D_MODEL), const),
            pl.BlockSpec((None, D_IN, D_MODEL), lambda i: (0, 0, 0), pipeline_mode=pl.Buffered(1)),
            pl.BlockSpec((GATE_RANK, GLA_KEY), const),
            pl.BlockSpec((1, GLA_KEY), const),
            pl.BlockSpec((D_MODEL // n, D_MODEL), row),
            pl.BlockSpec((D_MODEL, D_FF // n), col),
        ],
        out_specs=[
            pl.BlockSpec((TM_PROJ, GLA_KEY), row),
            pl.BlockSpec((TM_PROJ, GLA_KEY), row),
            pl.BlockSpec((TM_PROJ, GLA_KEY), row),
            pl.BlockSpec((TM_PROJ, GLA_VAL), row),
            pl.BlockSpec((TM_PROJ, GLA_VAL), row),
            pl.BlockSpec((TM_PROJ, CONV_CH), row),
            pl.BlockSpec((D_MODEL // n, D_MODEL), row),
            pl.BlockSpec((D_MODEL, D_FF // n), col),
        ],
        out_shape=[
            jax.ShapeDtypeStruct((s, GLA_KEY), F32),
            jax.ShapeDtypeStruct((s, GLA_KEY), F32),
            jax.ShapeDtypeStruct((s, GLA_KEY), F32),
            jax.ShapeDtypeStruct((s, GLA_VAL), BF16),
            jax.ShapeDtypeStruct((s, GLA_VAL), BF16),
            jax.ShapeDtypeStruct((s, CONV_CH), F32),
            jax.ShapeDtypeStruct((D_MODEL, D_MODEL), BF16),
            jax.ShapeDtypeStruct((D_MODEL, D_FF), BF16),
        ],
        scratch_shapes=[
            pltpu.VMEM((D_INP, D_MODEL), BF16),
            pltpu.VMEM((Z_PAD, GLA_KEY), BF16),
        ],
        compiler_params=pltpu.CompilerParams(
            dimension_semantics=("arbitrary",), vmem_limit_bytes=VMEM_LIMIT),
        name="proj",
    )(x2, g1, w_in_t, w_gate_up, bg, wo, w1)


def _gla_kernel(q_ref, k_ref, la_ref, v_ref, sg_ref, gn_ref, w2_ref,
                o_ref, w2b_ref, st_ref, b_ref):
    c = GLA_SUB

    w2b_ref[...] = w2_ref[...].astype(BF16)

    @pl.when(pl.program_id(0) == 0)
    def _():
        st_ref[...] = jnp.zeros_like(st_ref)

    ri = lax.broadcasted_iota(jnp.int32, (c, c), 0)
    ci = lax.broadcasted_iota(jnp.int32, (c, c), 1)
    first_head = lax.broadcasted_iota(jnp.int32, (c, LANES), 1) < GLA_DK
    st_row = lax.broadcasted_iota(jnp.int32, (2 * GLA_DV, LANES), 0)
    st_lane = lax.broadcasted_iota(jnp.int32, (2 * GLA_DV, LANES), 1)
    same_head = (st_row < GLA_DV) == (st_lane < GLA_DK)

    def masked_sum(m, rows):
        mb = m.astype(BF16)
        la_hi, la_lo = _split_bf16(la_ref[rows, :])
        return _dot(mb, la_hi) + _dot(mb, la_lo)

    b_min = None
    for s in range(GLA_NSUB):
        rows = slice(s * c, (s + 1) * c)
        b = masked_sum(ci <= ri, rows)
        b_ref[rows, :] = b
        b_min = b[c - 1:c, :] if b_min is None else jnp.minimum(b_min, b[c - 1:c, :])
    fast = jnp.min(b_min) > GLA_FAST_MIN_B

    def chunk(rows, states, scores_fn):
        b = b_ref[rows, :]
        b_last = b[c - 1:c, :]
        q = q_ref[rows, :]
        k = k_ref[rows, :]
        q_in = (q * jnp.exp(b)).astype(BF16)
        k_out = (k * jnp.exp(b_last - b)).astype(BF16)
        a_chunk = jnp.exp(b_last)
        new_states = []
        for p in range(GLA_HEADS // 2):
            kc = slice(p * LANES, (p + 1) * LANES)
            v_p = v_ref[rows, 2 * p * GLA_DV:2 * (p + 1) * GLA_DV]
            st = states[p]
            o_inter = _dot_nt(q_in[:, kc], st.astype(BF16))
            for j, s_h in enumerate(scores_fn(p, q, k, b, q_in)):
                h = 2 * p + j
                dv = slice(j * GLA_DV, (j + 1) * GLA_DV)
                o_h = _dot(s_h.astype(BF16), v_p[:, dv]) + o_inter[:, dv]
                ms = jnp.mean(o_h * o_h, axis=-1, keepdims=True)
                y = o_h * lax.rsqrt(ms + EPS) * gn_ref[...]
                cols = slice(h * GLA_DV, (h + 1) * GLA_DV)
                o_ref[rows, cols] = (y * sg_ref[rows, cols].astype(F32)).astype(BF16)
            upd = _dot_tn(v_p, k_out[:, kc])
            new_states.append(a_chunk[:, kc] * st + jnp.where(same_head, upd, 0.0))
        return new_states

    def fast_scores(p, q, k, b, q_in):
        kc = slice(p * LANES, (p + 1) * LANES)
        k_div = k[:, kc] * jnp.exp(-b[:, kc])
        k_blk = jnp.concatenate([jnp.where(first_head, k_div, 0.0),
                                 jnp.where(first_head, 0.0, k_div)], axis=0).astype(BF16)
        s2 = _dot_nt(q_in[:, kc], k_blk)
        return [jnp.where(ri >= ci, s2[:, j * c:(j + 1) * c], 0.0) for j in range(2)]

    def safe_scores(rows):
        q = q_ref[rows, :]
        k = k_ref[rows, :]
        kb = k.astype(BF16)
        qb = q.astype(BF16)
        scores = []
        for h in range(GLA_HEADS):
            dk = slice(h * GLA_DK, (h + 1) * GLA_DK)
            scores.append(jnp.where(ri == ci, _dot_nt(qb[:, dk], kb[:, dk]), 0.0))
        half = c // 2
        while half >= 1:
            ref = (ri & (-2 * half)) + (half - 1)
            lo = jnp.minimum(ri, ref)
            hi = jnp.maximum(ri, ref)
            e = jnp.exp(masked_sum((ci > lo) & (ci <= hi), rows))
            q_l = (q * e).astype(BF16)
            k_l = (k * e).astype(BF16)
            pair = (((ri ^ ci) & (-2 * half)) == 0) & ((ri & half) != 0) & ((ci & half) == 0)
            for h in range(GLA_HEADS):
                dk = slice(h * GLA_DK, (h + 1) * GLA_DK)
                scores[h] = jnp.where(pair, _dot_nt(q_l[:, dk], k_l[:, dk]), scores[h])
            half //= 2
        return scores

    @pl.when(fast)
    def _():
        states = [st_ref[p] for p in range(GLA_HEADS // 2)]
        for s in range(GLA_NSUB):
            states = chunk(slice(s * c, (s + 1) * c), states, fast_scores)
        for p in range(GLA_HEADS // 2):
            st_ref[p] = states[p]

    @pl.when(jnp.logical_not(fast))
    def _():
        def body(s, carry):
            rows = pl.ds(pl.multiple_of(s * c, c), c)
            scores = safe_scores(rows)
            states = chunk(rows, [st_ref[p] for p in range(GLA_HEADS // 2)],
                           lambda p, *_: scores[2 * p:2 * p + 2])
            for p in range(GLA_HEADS // 2):
                st_ref[p] = states[p]
            return carry

        lax.fori_loop(0, GLA_NSUB, body, 0)


def _gla_call(q, k, la, v, sg, gn, w2):
    s = q.shape[0]
    n = s // TM_GLA
    row = lambda i: (i, 0)
    const = lambda i: (0, 0)
    return pl.pallas_call(
        _gla_kernel,
        grid=(n,),
        in_specs=[
            pl.BlockSpec((TM_GLA, GLA_KEY), row),
            pl.BlockSpec((TM_GLA, GLA_KEY), row),
            pl.BlockSpec((TM_GLA, GLA_KEY), row),
            pl.BlockSpec((TM_GLA, GLA_VAL), row),
            pl.BlockSpec((TM_GLA, GLA_VAL), row),
            pl.BlockSpec((1, GLA_DV), const),
            pl.BlockSpec((D_FF // n, D_MODEL), row),
        ],
        out_specs=[
            pl.BlockSpec((TM_GLA, GLA_VAL), row),
            pl.BlockSpec((D_FF // n, D_MODEL), row),
        ],
        out_shape=[
            jax.ShapeDtypeStruct((s, GLA_VAL), BF16),
            jax.ShapeDtypeStruct((D_FF, D_MODEL), BF16),
        ],
        scratch_shapes=[
            pltpu.VMEM((GLA_HEADS // 2, 2 * GLA_DV, LANES), F32),
            pltpu.VMEM((TM_GLA, GLA_KEY), F32),
        ],
        compiler_params=pltpu.CompilerParams(
            dimension_semantics=("arbitrary",), vmem_limit_bytes=VMEM_LIMIT),
        name="gla",
    )(q, k, la, v, sg, gn, w2)


def _group_mean(t, low_half, size):
    lo = jnp.sum(jnp.where(low_half, t, 0.0), axis=-1, keepdims=True)
    hi = jnp.sum(jnp.where(low_half, 0.0, t), axis=-1, keepdims=True)
    return jnp.where(low_half, lo, hi) * (1.0 / size)


def _interleave(mxu_items, vec_items):
    done = 0
    for i, item in enumerate(mxu_items):
        item()
        upto = (i + 1) * len(vec_items) // len(mxu_items)
        for v_item in vec_items[done:upto]:
            v_item()
        done = upto


def _mlp_conv_kernel(x_ref, og_ref, un_ref, u0_ref, cw_ref, cb_ref, gg_ref, gb_ref,
                     wo_hbm, g2_ref, w1_hbm, w2_hbm, gf_ref,
                     o_ref, ext_ref, sh_ref, oc_ref, f_ref, wo_ref, w1_ref, w2_ref, w_sem):
    tm = TM_MLP

    def weight_copies():
        return [pltpu.make_async_copy(src, dst, w_sem.at[j])
                for j, (src, dst) in enumerate(((wo_hbm, wo_ref), (w1_hbm, w1_ref), (w2_hbm, w2_ref)))]

    grp = CONV_CH // CONV_GROUPS
    low_half = lax.broadcasted_iota(jnp.int32, (CONV_RB, LANES), 1) < grp
    first = CONV_HALO - (CONV_W - 1)

    def shifted_copy(p):
        sh_ref[p - 1] = ext_ref[p:p + CONV_SH_ROWS, :]

    def conv_block(base):
        for cg in range(CONV_CH // LANES):
            cols = slice(cg * LANES, (cg + 1) * LANES)
            acc = jnp.broadcast_to(cb_ref[:, cols], (CONV_RB, LANES))
            for t in range(CONV_W):
                shift = first + t
                p = shift % SUBLANES
                rows = pl.ds(base + (shift - p), CONV_RB)
                tap = ext_ref[rows, cols] if p == 0 else sh_ref[p - 1, rows, cols]
                acc = acc + cw_ref[t:t + 1, cols] * tap
            d = acc - _group_mean(acc, low_half, grp)
            var = _group_mean(d * d, low_half, grp)
            yn = d * lax.rsqrt(var + EPS) * gg_ref[:, cols] + gb_ref[:, cols]
            oc_ref[pl.ds(base, CONV_RB), cols] = (yn * _sigmoid(yn)).astype(BF16)

    @pl.when(pl.program_id(0) == 0)
    def _():
        for cp in weight_copies():
            cp.start()
        ext_ref[0:CONV_HALO, :] = jnp.zeros((CONV_HALO, CONV_CH), F32)
        ext_ref[CONV_HALO:, :] = u0_ref[...]
        for p in range(1, SUBLANES):
            shifted_copy(p)

        def body(r, carry):
            conv_block(pl.multiple_of(r * CONV_RB, CONV_RB))
            return carry

        lax.fori_loop(0, tm // CONV_RB, body, 0)
        for cp in weight_copies():
            cp.wait()

    h = x_ref[...] + _dot(og_ref[...], wo_ref[0:GLA_VAL, :]) + _dot(oc_ref[...], wo_ref[GLA_VAL:, :])
    hg = (h * g2_ref[...]).astype(BF16)
    r = lax.rsqrt(jnp.mean(h * h, axis=-1, keepdims=True) + EPS)
    r2 = r * r

    def roll_window():
        ext_ref[0:CONV_HALO, :] = ext_ref[tm:tm + CONV_HALO, :]
        ext_ref[CONV_HALO:, :] = un_ref[...]

    vec_items = [roll_window] + [functools.partial(shifted_copy, p) for p in range(1, SUBLANES)]
    vec_items += [functools.partial(conv_block, r * CONV_RB) for r in range(tm // CONV_RB)]

    def ff_in(j):
        cols = slice(j * FF_CHUNK, (j + 1) * FF_CHUNK)
        f = jnp.maximum(_dot(hg, w1_ref[:, cols]), 0.0)
        f_ref[:, cols] = (f * f * r2).astype(BF16)

    def ff_out(j):
        cols = slice(j * OUT_CHUNK, (j + 1) * OUT_CHUNK)
        o_ref[:, cols] = h[:, cols] + _dot(f_ref[...], w2_ref[:, cols])

    mxu_items = [functools.partial(ff_in, j) for j in range(D_FF // FF_CHUNK)]
    mxu_items += [functools.partial(ff_out, j) for j in range(D_MODEL // OUT_CHUNK)]
    _interleave(mxu_items, vec_items)

    acc = o_ref[...]
    ms2 = jnp.mean(acc * acc, axis=-1, keepdims=True)
    o_ref[...] = acc * lax.rsqrt(ms2 + EPS) * gf_ref[...]


def _mlp_conv_call(x2, og, u, cw, cb, gg, gb, wo, g2, w1, w2, gf):
    s = x2.shape[0]
    n = s // TM_MLP
    row = lambda i: (i, 0)
    nxt = lambda i: (jnp.minimum(i + 1, n - 1), 0)
    const = lambda i: (0, 0)
    single = pl.Buffered(1)
    return pl.pallas_call(
        _mlp_conv_kernel,
        grid=(n,),
        in_specs=[
            pl.BlockSpec((TM_MLP, D_MODEL), row),
            pl.BlockSpec((TM_MLP, GLA_VAL), row),
            pl.BlockSpec((TM_MLP, CONV_CH), nxt),
            pl.BlockSpec((TM_MLP, CONV_CH), const, pipeline_mode=single),
            pl.BlockSpec((CONV_W, CONV_CH), const),
            pl.BlockSpec((1, CONV_CH), const),
            pl.BlockSpec((1, CONV_CH), const),
            pl.BlockSpec((1, CONV_CH), const),
            pl.BlockSpec(memory_space=pl.ANY),
            pl.BlockSpec((1, D_MODEL), const),
            pl.BlockSpec(memory_space=pl.ANY),
            pl.BlockSpec(memory_space=pl.ANY),
            pl.BlockSpec((1, D_MODEL), const),
        ],
        out_specs=pl.BlockSpec((TM_MLP, D_MODEL), row),
        out_shape=jax.ShapeDtypeStruct((s, D_MODEL), F32),
        scratch_shapes=[
            pltpu.VMEM((TM_MLP + CONV_HALO, CONV_CH), F32),
            pltpu.VMEM((SUBLANES - 1, CONV_SH_ROWS, CONV_CH), F32),
            pltpu.VMEM((TM_MLP, CONV_CH), BF16),
            pltpu.VMEM((TM_MLP, D_FF), BF16),
            pltpu.VMEM((D_MODEL, D_MODEL), BF16),
            pltpu.VMEM((D_MODEL, D_FF), BF16),
            pltpu.VMEM((D_FF, D_MODEL), BF16),
            pltpu.SemaphoreType.DMA((3,)),
        ],
        compiler_params=pltpu.CompilerParams(
            dimension_semantics=("arbitrary",), vmem_limit_bytes=VMEM_LIMIT),
        name="mlp_conv",
    )(x2, og, u, u, cw, cb, gg, gb, wo, g2, w1, w2, gf)


def kernel(x, norm1_g, w_in, w_gate_up, b_gate, gla_norm_g, conv_w, conv_b, conv_norm_g,
           conv_norm_b, w_out, norm2_g, w_mlp_in, w_mlp_out, final_norm_g):
    bsz, seq, _ = x.shape
    x2 = x.reshape(bsz * seq, D_MODEL)
    assert bsz == 1, "state / halo carry across grid steps assumes one sequence"
    l = 0
    assert w_in.shape[0] == 1, "single layer: the projection weight is passed with its layer axis"
    w_in_t = jnp.swapaxes(w_in, 1, 2)
    q, k, la, v, sg, u, wo_b, w1_b = _proj_call(
        x2, norm1_g[l][None, :], w_in_t, w_gate_up[l], b_gate[l][None, :], w_out[l], w_mlp_in[l])
    o_gla, w2_b = _gla_call(q, k, la, v, sg, gla_norm_g[l][None, :], w_mlp_out[l])
    out = _mlp_conv_call(x2, o_gla, u, conv_w[l], conv_b[l][None, :], conv_norm_g[l][None, :],
                         conv_norm_b[l][None, :], wo_b, norm2_g[l][None, :], w1_b, w2_b,
                         final_norm_g[None, :])
    return out.reshape(bsz, seq, D_MODEL)
```

```python
import functools

import jax
import jax.numpy as jnp
from jax import lax
from jax.experimental import pallas as pl
from jax.experimental.pallas import tpu as pltpu

D_MODEL = 1024
GLA_HEADS = 4
GLA_DV = 128
GLA_DK = 64
GLA_KEY = GLA_HEADS * GLA_DK
GLA_VAL = GLA_HEADS * GLA_DV
GATE_RANK = 16
GATE_TAU = 16.0
CONV_CH = 512
CONV_GROUPS = 8
CONV_W = 31
D_FF = 4 * D_MODEL
EPS = 1e-6

OFF_Q = 0
OFF_K = OFF_Q + GLA_KEY
OFF_V = OFF_K + GLA_KEY
OFF_G = OFF_V + GLA_VAL
OFF_A = OFF_G + GLA_VAL
OFF_C = OFF_A + GATE_RANK
D_IN = OFF_C + 2 * CONV_CH

LANES = 128
SUBLANES = 8
Z_PAD = LANES
P_Q = 0
P_K = P_Q + GLA_KEY
P_V = P_K + GLA_KEY
P_G = P_V + GLA_VAL
P_CI = P_G + GLA_VAL
P_CG = P_CI + CONV_CH
P_Z = P_CG + CONV_CH
D_INP = P_Z + Z_PAD

TM_PROJ = 1024
GLA_SUB = 256
GLA_NSUB = 8
TM_GLA = GLA_SUB * GLA_NSUB
TM_MLP = 512
FF_CHUNK = 512
OUT_CHUNK = 256
CONV_HALO = 32
CONV_RB = 32
CONV_SH_ROWS = TM_MLP + CONV_HALO - SUBLANES
GLA_FAST_MIN_B = -60.0

VMEM_LIMIT = 56 * 1024 * 1024

F32 = jnp.float32
BF16 = jnp.bfloat16


def _sigmoid(x):
    return 1.0 / (1.0 + jnp.exp(-x))


def _split_bf16(x):
    hi = x.astype(BF16)
    lo = (x - hi.astype(F32)).astype(BF16)
    return hi, lo


def _dot(a, b):
    return jnp.dot(a, b, preferred_element_type=F32)


def _dot_nt(a, b):
    return lax.dot_general(a, b, (((1,), (1,)), ((), ())), preferred_element_type=F32)


def _dot_tn(a, b):
    return lax.dot_general(a, b, (((0,), (0,)), ((), ())), preferred_element_type=F32)


def _proj_kernel(x_ref, g1_ref, w_ref, wg_ref, bg_ref, wo_ref, w1_ref,
                 q_ref, k_ref, la_ref, v_ref, sg_ref, u_ref, wob_ref, w1b_ref, wb_ref, wgb_ref):
    wob_ref[...] = wo_ref[...].astype(BF16)
    w1b_ref[...] = w1_ref[...].astype(BF16)

    @pl.when(pl.program_id(0) == 0)
    def _():
        wb_ref[P_Q:P_CI, :] = w_ref[OFF_Q:OFF_A, :].astype(BF16)
        wb_ref[P_CI:P_Z, :] = w_ref[OFF_C:D_IN, :].astype(BF16)
        wb_ref[P_Z:P_Z + GATE_RANK, :] = w_ref[OFF_A:OFF_C, :].astype(BF16)
        wb_ref[P_Z + GATE_RANK:D_INP, :] = jnp.zeros((Z_PAD - GATE_RANK, D_MODEL), BF16)
        wgb_ref[...] = jnp.zeros_like(wgb_ref)
        wgb_ref[0:GATE_RANK, :] = wg_ref[...].astype(BF16)

    x = x_ref[...]
    ms = jnp.mean(x * x, axis=-1, keepdims=True)
    xn = (x * lax.rsqrt(ms + EPS) * g1_ref[...]).astype(BF16)
    proj = _dot_nt(xn, wb_ref[...])
    q_ref[...] = proj[:, P_Q:P_K] * (GLA_DK ** -0.5)
    k_ref[...] = proj[:, P_K:P_V]
    v_ref[...] = proj[:, P_V:P_G].astype(BF16)
    g = proj[:, P_G:P_CI]
    sg_ref[...] = (g * _sigmoid(g)).astype(BF16)
    u_ref[...] = proj[:, P_CI:P_CG] * _sigmoid(proj[:, P_CG:P_Z])
    z = proj[:, P_Z:D_INP].astype(BF16)
    a_logit = _dot(z, wgb_ref[...]) + bg_ref[...]
    la = jnp.minimum(a_logit, 0.0) - jnp.log(1.0 + jnp.exp(-jnp.abs(a_logit)))
    la_ref[...] = la * (1.0 / GATE_TAU)


def _proj_call(x2, g1, w_in_t, w_gate_up, bg, wo, w1):
    s = x2.shape[0]
    n = s // TM_PROJ
    row = lambda i: (i, 0)
    col = lambda i: (0, i)
    const = lambda i: (0, 0)
    return pl.pallas_call(
        _proj_kernel,
        grid=(n,),
        in_specs=[
            pl.BlockSpec((TM_PROJ, D_MODEL), row),
            pl.BlockSpec((1, D_MODEL), const),
            pl.BlockSpec((None, D_IN, D_MODEL), lambda i: (0, 0, 0), pipeline_mode=pl.Buffered(1)),
            pl.BlockSpec((GATE_RANK, GLA_KEY), const),
            pl.BlockSpec((1, GLA_KEY), const),
            pl.BlockSpec((D_MODEL // n, D_MODEL), row),
            pl.BlockSpec((D_MODEL, D_FF // n), col),
        ],
        out_specs=[
            pl.BlockSpec((TM_PROJ, GLA_KEY), row),
            pl.BlockSpec((TM_PROJ, GLA_KEY), row),
            pl.BlockSpec((TM_PROJ, GLA_KEY), row),
            pl.BlockSpec((TM_PROJ, GLA_VAL), row),
            pl.BlockSpec((TM_PROJ, GLA_VAL), row),
            pl.BlockSpec((TM_PROJ, CONV_CH), row),
            pl.BlockSpec((D_MODEL // n, D_MODEL), row),
            pl.BlockSpec((D_MODEL, D_FF // n), col),
        ],
        out_shape=[
            jax.ShapeDtypeStruct((s, GLA_KEY), F32),
            jax.ShapeDtypeStruct((s, GLA_KEY), F32),
            jax.ShapeDtypeStruct((s, GLA_KEY), F32),
            jax.ShapeDtypeStruct((s, GLA_VAL), BF16),
            jax.ShapeDtypeStruct((s, GLA_VAL), BF16),
            jax.ShapeDtypeStruct((s, CONV_CH), F32),
            jax.ShapeDtypeStruct((D_MODEL, D_MODEL), BF16),
            jax.ShapeDtypeStruct((D_MODEL, D_FF), BF16),
        ],
        scratch_shapes=[
            pltpu.VMEM((D_INP, D_MODEL), BF16),
            pltpu.VMEM((Z_PAD, GLA_KEY), BF16),
        ],
        compiler_params=pltpu.CompilerParams(
            dimension_semantics=("arbitrary",), vmem_limit_bytes=VMEM_LIMIT),
        name="proj",
    )(x2, g1, w_in_t, w_gate_up, bg, wo, w1)


def _gla_kernel(q_ref, k_ref, la_ref, v_ref, sg_ref, gn_ref, w2_ref,
                o_ref, w2b_ref, st_ref, b_ref):
    c = GLA_SUB

    w2b_ref[...] = w2_ref[...].astype(BF16)

    @pl.when(pl.program_id(0) == 0)
    def _():
        st_ref[...] = jnp.zeros_like(st_ref)

    ri = lax.broadcasted_iota(jnp.int32, (c, c), 0)
    ci = lax.broadcasted_iota(jnp.int32, (c, c), 1)
    first_head = lax.broadcasted_iota(jnp.int32, (c, LANES), 1) < GLA_DK
    st_row = lax.broadcasted_iota(jnp.int32, (2 * GLA_DV, LANES), 0)
    st_lane = lax.broadcasted_iota(jnp.int32, (2 * GLA_DV, LANES), 1)
    same_head = (st_row < GLA_DV) == (st_lane < GLA_DK)

    def masked_sum(m, rows):
        mb = m.astype(BF16)
        la_hi, la_lo = _split_bf16(la_ref[rows, :])
        return _dot(mb, la_hi) + _dot(mb, la_lo)

    b_min = None
    for s in range(GLA_NSUB):
        rows = slice(s * c, (s + 1) * c)
        b = masked_sum(ci <= ri, rows)
        b_ref[rows, :] = b
        b_min = b[c - 1:c, :] if b_min is None else jnp.minimum(b_min, b[c - 1:c, :])
    fast = jnp.min(b_min) > GLA_FAST_MIN_B

    def chunk(rows, states, scores_fn):
        b = b_ref[rows, :]
        b_last = b[c - 1:c, :]
        q = q_ref[rows, :]
        k = k_ref[rows, :]
        q_in = (q * jnp.exp(b)).astype(BF16)
        k_out = (k * jnp.exp(b_last - b)).astype(BF16)
        a_chunk = jnp.exp(b_last)
        new_states = []
        for p in range(GLA_HEADS // 2):
            kc = slice(p * LANES, (p + 1) * LANES)
            v_p = v_ref[rows, 2 * p * GLA_DV:2 * (p + 1) * GLA_DV]
            st = states[p]
            o_inter = _dot_nt(q_in[:, kc], st.astype(BF16))
            for j, s_h in enumerate(scores_fn(p, q, k, b, q_in)):
                h = 2 * p + j
                dv = slice(j * GLA_DV, (j + 1) * GLA_DV)
                o_h = _dot(s_h.astype(BF16), v_p[:, dv]) + o_inter[:, dv]
                ms = jnp.mean(o_h * o_h, axis=-1, keepdims=True)
                y = o_h * lax.rsqrt(ms + EPS) * gn_ref[...]
                cols = slice(h * GLA_DV, (h + 1) * GLA_DV)
                o_ref[rows, cols] = (y * sg_ref[rows, cols].astype(F32)).astype(BF16)
            upd = _dot_tn(v_p, k_out[:, kc])
            new_states.append(a_chunk[:, kc] * st + jnp.where(same_head, upd, 0.0))
        return new_states

    def fast_scores(p, q, k, b, q_in):
        kc = slice(p * LANES, (p + 1) * LANES)
        k_div = k[:, kc] * jnp.exp(-b[:, kc])
        k_blk = jnp.concatenate([jnp.where(first_head, k_div, 0.0),
                                 jnp.where(first_head, 0.0, k_div)], axis=0).astype(BF16)
        s2 = _dot_nt(q_in[:, kc], k_blk)
        return [jnp.where(ri >= ci, s2[:, j * c:(j + 1) * c], 0.0) for j in range(2)]

    def safe_scores(rows):
        q = q_ref[rows, :]
        k = k_ref[rows, :]
        qk = q * k
        scores = []
        for h in range(GLA_HEADS):
            dk = slice(h * GLA_DK, (h + 1) * GLA_DK)
            scores.append(jnp.where(ri == ci, jnp.sum(qk[:, dk], axis=-1, keepdims=True), 0.0))
        half = c // 2
        while half >= 1:
            ref = (ri & (-2 * half)) + (half - 1)
            lo = jnp.minimum(ri, ref)
            hi = jnp.maximum(ri, ref)
            e = jnp.exp(masked_sum((ci > lo) & (ci <= hi), rows))
            q_l = (q * e).astype(BF16)
            k_l = (k * e).astype(BF16)
            pair = (((ri ^ ci) & (-2 * half)) == 0) & ((ri & half) != 0) & ((ci & half) == 0)
            for h in range(GLA_HEADS):
                dk = slice(h * GLA_DK, (h + 1) * GLA_DK)
                scores[h] = jnp.where(pair, _dot_nt(q_l[:, dk], k_l[:, dk]), scores[h])
            half //= 2
        return scores

    @pl.when(fast)
    def _():
        states = [st_ref[p] for p in range(GLA_HEADS // 2)]
        for s in range(GLA_NSUB):
            states = chunk(slice(s * c, (s + 1) * c), states, fast_scores)
        for p in range(GLA_HEADS // 2):
            st_ref[p] = states[p]

    @pl.when(jnp.logical_not(fast))
    def _():
        def body(s, carry):
            rows = pl.ds(pl.multiple_of(s * c, c), c)
            scores = safe_scores(rows)
            states = chunk(rows, [st_ref[p] for p in range(GLA_HEADS // 2)],
                           lambda p, *_: scores[2 * p:2 * p + 2])
            for p in range(GLA_HEADS // 2):
                st_ref[p] = states[p]
            return carry

        lax.fori_loop(0, GLA_NSUB, body, 0)


def _gla_call(q, k, la, v, sg, gn, w2):
    s = q.shape[0]
    n = s // TM_GLA
    row = lambda i: (i, 0)
    const = lambda i: (0, 0)
    return pl.pallas_call(
        _gla_kernel,
        grid=(n,),
        in_specs=[
            pl.BlockSpec((TM_GLA, GLA_KEY), row),
            pl.BlockSpec((TM_GLA, GLA_KEY), row),
            pl.BlockSpec((TM_GLA, GLA_KEY), row),
            pl.BlockSpec((TM_GLA, GLA_VAL), row),
            pl.BlockSpec((TM_GLA, GLA_VAL), row),
            pl.BlockSpec((1, GLA_DV), const),
            pl.BlockSpec((D_FF // n, D_MODEL), row),
        ],
        out_specs=[
            pl.BlockSpec((TM_GLA, GLA_VAL), row),
            pl.BlockSpec((D_FF // n, D_MODEL), row),
        ],
        out_shape=[
            jax.ShapeDtypeStruct((s, GLA_VAL), BF16),
            jax.ShapeDtypeStruct((D_FF, D_MODEL), BF16),
        ],
        scratch_shapes=[
            pltpu.VMEM((GLA_HEADS // 2, 2 * GLA_DV, LANES), F32),
            pltpu.VMEM((TM_GLA, GLA_KEY), F32),
        ],
        compiler_params=pltpu.CompilerParams(
            dimension_semantics=("arbitrary",), vmem_limit_bytes=VMEM_LIMIT),
        name="gla",
    )(q, k, la, v, sg, gn, w2)


def _group_mean(t, low_half, size):
    lo = jnp.sum(jnp.where(low_half, t, 0.0), axis=-1, keepdims=True)
    hi = jnp.sum(jnp.where(low_half, 0.0, t), axis=-1, keepdims=True)
    return jnp.where(low_half, lo, hi) * (1.0 / size)


def _interleave(mxu_items, vec_items):
    done = 0
    for i, item in enumerate(mxu_items):
        item()
        upto = (i + 1) * len(vec_items) // len(mxu_items)
        for v_item in vec_items[done:upto]:
            v_item()
        done = upto


def _mlp_conv_kernel(x_ref, og_ref, un_ref, u0_ref, cw_ref, cb_ref, gg_ref, gb_ref,
                     wo_hbm, g2_ref, w1_hbm, w2_hbm, gf_ref,
                     o_ref, ext_ref, sh_ref, oc_ref, f_ref, wo_ref, w1_ref, w2_ref, w_sem):
    tm = TM_MLP

    def weight_copies():
        return [pltpu.make_async_copy(src, dst, w_sem.at[j])
                for j, (src, dst) in enumerate(((wo_hbm, wo_ref), (w1_hbm, w1_ref), (w2_hbm, w2_ref)))]

    grp = CONV_CH // CONV_GROUPS
    low_half = lax.broadcasted_iota(jnp.int32, (CONV_RB, LANES), 1) < grp
    first = CONV_HALO - (CONV_W - 1)

    def shifted_copy(p):
        sh_ref[p - 1] = ext_ref[p:p + CONV_SH_ROWS, :]

    def conv_block(base):
        for cg in range(CONV_CH // LANES):
            cols = slice(cg * LANES, (cg + 1) * LANES)
            acc = jnp.broadcast_to(cb_ref[:, cols], (CONV_RB, LANES))
            for t in range(CONV_W):
                shift = first + t
                p = shift % SUBLANES
                rows = pl.ds(base + (shift - p), CONV_RB)
                tap = ext_ref[rows, cols] if p == 0 else sh_ref[p - 1, rows, cols]
                acc = acc + cw_ref[t:t + 1, cols] * tap
            d = acc - _group_mean(acc, low_half, grp)
            var = _group_mean(d * d, low_half, grp)
            yn = d * lax.rsqrt(var + EPS) * gg_ref[:, cols] + gb_ref[:, cols]
            oc_ref[pl.ds(base, CONV_RB), cols] = (yn * _sigmoid(yn)).astype(BF16)

    @pl.when(pl.program_id(0) == 0)
    def _():
        for cp in weight_copies():
            cp.start()
        ext_ref[0:CONV_HALO, :] = jnp.zeros((CONV_HALO, CONV_CH), F32)
        ext_ref[CONV_HALO:, :] = u0_ref[...]
        for p in range(1, SUBLANES):
            shifted_copy(p)

        def body(r, carry):
            conv_block(pl.multiple_of(r * CONV_RB, CONV_RB))
            return carry

        lax.fori_loop(0, tm // CONV_RB, body, 0)
        for cp in weight_copies():
            cp.wait()

    h = x_ref[...] + _dot(og_ref[...], wo_ref[0:GLA_VAL, :]) + _dot(oc_ref[...], wo_ref[GLA_VAL:, :])
    hg = (h * g2_ref[...]).astype(BF16)
    r = lax.rsqrt(jnp.mean(h * h, axis=-1, keepdims=True) + EPS)
    r2 = r * r

    def roll_window():
        ext_ref[0:CONV_HALO, :] = ext_ref[tm:tm + CONV_HALO, :]
        ext_ref[CONV_HALO:, :] = un_ref[...]

    vec_items = [roll_window] + [functools.partial(shifted_copy, p) for p in range(1, SUBLANES)]
    vec_items += [functools.partial(conv_block, r * CONV_RB) for r in range(tm // CONV_RB)]

    def ff_in(j):
        cols = slice(j * FF_CHUNK, (j + 1) * FF_CHUNK)
        f = jnp.maximum(_dot(hg, w1_ref[:, cols]), 0.0)
        f_ref[:, cols] = (f * f * r2).astype(BF16)

    def ff_out(j):
        cols = slice(j * OUT_CHUNK, (j + 1) * OUT_CHUNK)
        o_ref[:, cols] = h[:, cols] + _dot(f_ref[...], w2_ref[:, cols])

    mxu_items = [functools.partial(ff_in, j) for j in range(D_FF // FF_CHUNK)]
    mxu_items += [functools.partial(ff_out, j) for j in range(D_MODEL // OUT_CHUNK)]
    _interleave(mxu_items, vec_items)

    acc = o_ref[...]
    ms2 = jnp.mean(acc * acc, axis=-1, keepdims=True)
    o_ref[...] = acc * lax.rsqrt(ms2 + EPS) * gf_ref[...]


def _mlp_conv_call(x2, og, u, cw, cb, gg, gb, wo, g2, w1, w2, gf):
    s = x2.shape[0]
    n = s // TM_MLP
    row = lambda i: (i, 0)
    nxt = lambda i: (jnp.minimum(i + 1, n - 1), 0)
    const = lambda i: (0, 0)
    single = pl.Buffered(1)
    return pl.pallas_call(
        _mlp_conv_kernel,
        grid=(n,),
        in_specs=[
            pl.BlockSpec((TM_MLP, D_MODEL), row),
            pl.BlockSpec((TM_MLP, GLA_VAL), row),
            pl.BlockSpec((TM_MLP, CONV_CH), nxt),
            pl.BlockSpec((TM_MLP, CONV_CH), const, pipeline_mode=single),
            pl.BlockSpec((CONV_W, CONV_CH), const),
            pl.BlockSpec((1, CONV_CH), const),
            pl.BlockSpec((1, CONV_CH), const),
            pl.BlockSpec((1, CONV_CH), const),
            pl.BlockSpec(memory_space=pl.ANY),
            pl.BlockSpec((1, D_MODEL), const),
            pl.BlockSpec(memory_space=pl.ANY),
            pl.BlockSpec(memory_space=pl.ANY),
            pl.BlockSpec((1, D_MODEL), const),
        ],
        out_specs=pl.BlockSpec((TM_MLP, D_MODEL), row),
        out_shape=jax.ShapeDtypeStruct((s, D_MODEL), F32),
        scratch_shapes=[
            pltpu.VMEM((TM_MLP + CONV_HALO, CONV_CH), F32),
            pltpu.VMEM((SUBLANES - 1, CONV_SH_ROWS, CONV_CH), F32),
            pltpu.VMEM((TM_MLP, CONV_CH), BF16),
            pltpu.VMEM((TM_MLP, D_FF), BF16),
            pltpu.VMEM((D_MODEL, D_MODEL), BF16),
            pltpu.VMEM((D_MODEL, D_FF), BF16),
            pltpu.VMEM((D_FF, D_MODEL), BF16),
            pltpu.SemaphoreType.DMA((3,)),
        ],
        compiler_params=pltpu.CompilerParams(
            dimension_semantics=("arbitrary",), vmem_limit_bytes=VMEM_LIMIT),
        name="mlp_conv",
    )(x2, og, u, u, cw, cb, gg, gb, wo, g2, w1, w2, gf)


def kernel(x, norm1_g, w_in, w_gate_up, b_gate, gla_norm_g, conv_w, conv_b, conv_norm_g,
           conv_norm_b, w_out, norm2_g, w_mlp_in, w_mlp_out, final_norm_g):
    bsz, seq, _ = x.shape
    x2 = x.reshape(bsz * seq, D_MODEL)
    assert bsz == 1, "state / halo carry across grid steps assumes one sequence"
    l = 0
    assert w_in.shape[0] == 1, "single layer: the projection weight is passed with its layer axis"
    w_in_t = jnp.swapaxes(w_in, 1, 2)
    q, k, la, v, sg, u, wo_b, w1_b = _proj_call(
        x2, norm1_g[l][None, :], w_in_t, w_gate_up[l], b_gate[l][None, :], w_out[l], w_mlp_in[l])
    o_gla, w2_b = _gla_call(q, k, la, v, sg, gla_norm_g[l][None, :], w_mlp_out[l])
    out = _mlp_conv_call(x2, o_gla, u, conv_w[l], conv_b[l][None, :], conv_norm_g[l][None, :],
                         conv_norm_b[l][None, :], wo_b, norm2_g[l][None, :], w1_b, w2_b,
                         final_norm_g[None, :])
    return out.reshape(bsz, seq, D_MODEL)
```

```python
import functools

import jax
import jax.numpy as jnp
from jax import lax
from jax.experimental import pallas as pl
from jax.experimental.pallas import tpu as pltpu

D_MODEL = 1024
GLA_HEADS = 4
GLA_DV = 128
GLA_DK = 64
GLA_KEY = GLA_HEADS * GLA_DK
GLA_VAL = GLA_HEADS * GLA_DV
GATE_RANK = 16
GATE_TAU = 16.0
CONV_CH = 512
CONV_GROUPS = 8
CONV_W = 31
D_FF = 4 * D_MODEL
EPS = 1e-6

OFF_Q = 0
OFF_K = OFF_Q + GLA_KEY
OFF_V = OFF_K + GLA_KEY
OFF_G = OFF_V + GLA_VAL
OFF_A = OFF_G + GLA_VAL
OFF_C = OFF_A + GATE_RANK
D_IN = OFF_C + 2 * CONV_CH

LANES = 128
SUBLANES = 8
Z_PAD = LANES
P_Q = 0
P_K = P_Q + GLA_KEY
P_V = P_K + GLA_KEY
P_G = P_V + GLA_VAL
P_CI = P_G + GLA_VAL
P_CG = P_CI + CONV_CH
P_Z = P_CG + CONV_CH
D_INP = P_Z + Z_PAD

TM_PROJ = 1024
GLA_SUB = 256
GLA_NSUB = 8
TM_GLA = GLA_SUB * GLA_NSUB
TM_MLP = 512
FF_CHUNK = 512
OUT_CHUNK = 256
CONV_HALO = 32
CONV_RB = 32
CONV_SH_ROWS = TM_MLP + CONV_HALO - SUBLANES
GLA_FAST_MIN_B = -60.0

V7X_VMEM_BYTES = 64 * 1024 * 1024
VMEM_LIMIT = V7X_VMEM_BYTES - 8 * 1024 * 1024

F32 = jnp.float32
BF16 = jnp.bfloat16


def _sigmoid(x):
    return 1.0 / (1.0 + jnp.exp(-x))


def _split_bf16(x):
    hi = x.astype(BF16)
    lo = (x - hi.astype(F32)).astype(BF16)
    return hi, lo


def _dot(a, b):
    return jnp.dot(a, b, preferred_element_type=F32)


def _dot_nt(a, b):
    return lax.dot_general(a, b, (((1,), (1,)), ((), ())), preferred_element_type=F32)


def _dot_tn(a, b):
    return lax.dot_general(a, b, (((0,), (0,)), ((), ())), preferred_element_type=F32)


def _proj_kernel(x_ref, g1_ref, w_ref, wg_ref, bg_ref, wo_ref, w1_ref,
                 q_ref, k_ref, la_ref, v_ref, sg_ref, u_ref, wob_ref, w1b_ref, wb_ref, wgb_ref):
    wob_ref[...] = wo_ref[...].astype(BF16)
    w1b_ref[...] = w1_ref[...].astype(BF16)

    @pl.when(pl.program_id(0) == 0)
    def _():
        wb_ref[P_Q:P_CI, :] = w_ref[OFF_Q:OFF_A, :].astype(BF16)
        wb_ref[P_CI:P_Z, :] = w_ref[OFF_C:D_IN, :].astype(BF16)
        wb_ref[P_Z:P_Z + GATE_RANK, :] = w_ref[OFF_A:OFF_C, :].astype(BF16)
        wb_ref[P_Z + GATE_RANK:D_INP, :] = jnp.zeros((Z_PAD - GATE_RANK, D_MODEL), BF16)
        wgb_ref[...] = jnp.zeros_like(wgb_ref)
        wgb_ref[0:GATE_RANK, :] = wg_ref[...].astype(BF16)

    x = x_ref[...]
    ms = jnp.mean(x * x, axis=-1, keepdims=True)
    xn = (x * lax.rsqrt(ms + EPS) * g1_ref[...]).astype(BF16)
    proj = _dot_nt(xn, wb_ref[...])
    q_ref[...] = proj[:, P_Q:P_K] * (GLA_DK ** -0.5)
    k_ref[...] = proj[:, P_K:P_V]
    v_ref[...] = proj[:, P_V:P_G].astype(BF16)
    g = proj[:, P_G:P_CI]
    sg_ref[...] = (g * _sigmoid(g)).astype(BF16)
    u_ref[...] = proj[:, P_CI:P_CG] * _sigmoid(proj[:, P_CG:P_Z])
    z = proj[:, P_Z:D_INP].astype(BF16)
    a_logit = _dot(z, wgb_ref[...]) + bg_ref[...]
    la = jnp.minimum(a_logit, 0.0) - jnp.log(1.0 + jnp.exp(-jnp.abs(a_logit)))
    la_ref[...] = la * (1.0 / GATE_TAU)


def _proj_call(x2, g1, w_in_t, w_gate_up, bg, wo, w1):
    s = x2.shape[0]
    n = s // TM_PROJ
    row = lambda i: (i, 0)
    col = lambda i: (0, i)
    const = lambda i: (0, 0)
    return pl.pallas_call(
        _proj_kernel,
        grid=(n,),
        in_specs=[
            pl.BlockSpec((TM_PROJ, D_MODEL), row),
            pl.BlockSpec((1, D_MODEL), const),
            pl.BlockSpec((None, D_IN, D_MODEL), lambda i: (0, 0, 0), pipeline_mode=pl.Buffered(1)),
            pl.BlockSpec((GATE_RANK, GLA_KEY), const),
            pl.BlockSpec((1, GLA_KEY), const),
            pl.BlockSpec((D_MODEL // n, D_MODEL), row),
            pl.BlockSpec((D_MODEL, D_FF // n), col),
        ],
        out_specs=[
            pl.BlockSpec((TM_PROJ, GLA_KEY), row),
            pl.BlockSpec((TM_PROJ, GLA_KEY), row),
            pl.BlockSpec((TM_PROJ, GLA_KEY), row),
            pl.BlockSpec((TM_PROJ, GLA_VAL), row),
            pl.BlockSpec((TM_PROJ, GLA_VAL), row),
            pl.BlockSpec((TM_PROJ, CONV_CH), row),
            pl.BlockSpec((D_MODEL // n, D_MODEL), row),
            pl.BlockSpec((D_MODEL, D_FF // n), col),
        ],
        out_shape=[
            jax.ShapeDtypeStruct((s, GLA_KEY), F32),
            jax.ShapeDtypeStruct((s, GLA_KEY), F32),
            jax.ShapeDtypeStruct((s, GLA_KEY), F32),
            jax.ShapeDtypeStruct((s, GLA_VAL), BF16),
            jax.ShapeDtypeStruct((s, GLA_VAL), BF16),
            jax.ShapeDtypeStruct((s, CONV_CH), F32),
            jax.ShapeDtypeStruct((D_MODEL, D_MODEL), BF16),
            jax.ShapeDtypeStruct((D_MODEL, D_FF), BF16),
        ],
        scratch_shapes=[
            pltpu.VMEM((D_INP, D_MODEL), BF16),
            pltpu.VMEM((Z_PAD, GLA_KEY), BF16),
        ],
        compiler_params=pltpu.CompilerParams(
            dimension_semantics=("arbitrary",), vmem_limit_bytes=VMEM_LIMIT),
        name="proj",
    )(x2, g1, w_in_t, w_gate_up, bg, wo, w1)


def _gla_kernel(q_ref, k_ref, la_ref, v_ref, sg_ref, gn_ref, w2_ref,
                o_ref, w2b_ref, st_ref, b_ref):
    c = GLA_SUB

    w2b_ref[...] = w2_ref[...].astype(BF16)

    @pl.when(pl.program_id(0) == 0)
    def _():
        st_ref[...] = jnp.zeros_like(st_ref)

    ri = lax.broadcasted_iota(jnp.int32, (c, c), 0)
    ci = lax.broadcasted_iota(jnp.int32, (c, c), 1)
    first_head = lax.broadcasted_iota(jnp.int32, (c, LANES), 1) < GLA_DK
    st_row = lax.broadcasted_iota(jnp.int32, (2 * GLA_DV, LANES), 0)
    st_lane = lax.broadcasted_iota(jnp.int32, (2 * GLA_DV, LANES), 1)
    same_head = (st_row < GLA_DV) == (st_lane < GLA_DK)

    def masked_sum(m, rows):
        mb = m.astype(BF16)
        la_hi, la_lo = _split_bf16(la_ref[rows, :])
        return _dot(mb, la_hi) + _dot(mb, la_lo)

    b_min = None
    for s in range(GLA_NSUB):
        rows = slice(s * c, (s + 1) * c)
        b = masked_sum(ci <= ri, rows)
        b_ref[rows, :] = b
        b_min = b[c - 1:c, :] if b_min is None else jnp.minimum(b_min, b[c - 1:c, :])
    fast = jnp.min(b_min) > GLA_FAST_MIN_B

    def chunk(rows, states, scores_fn):
        b = b_ref[rows, :]
        b_last = b[c - 1:c, :]
        q = q_ref[rows, :]
        k = k_ref[rows, :]
        q_in = (q * jnp.exp(b)).astype(BF16)
        k_out = (k * jnp.exp(b_last - b)).astype(BF16)
        a_chunk = jnp.exp(b_last)
        new_states = []
        for p in range(GLA_HEADS // 2):
            kc = slice(p * LANES, (p + 1) * LANES)
            v_p = v_ref[rows, 2 * p * GLA_DV:2 * (p + 1) * GLA_DV]
            st = states[p]
            o_inter = _dot_nt(q_in[:, kc], st.astype(BF16))
            for j, s_h in enumerate(scores_fn(p, q, k, b, q_in)):
                h = 2 * p + j
                dv = slice(j * GLA_DV, (j + 1) * GLA_DV)
                o_h = _dot(s_h.astype(BF16), v_p[:, dv]) + o_inter[:, dv]
                ms = jnp.mean(o_h * o_h, axis=-1, keepdims=True)
                y = o_h * lax.rsqrt(ms + EPS) * gn_ref[...]
                cols = slice(h * GLA_DV, (h + 1) * GLA_DV)
                o_ref[rows, cols] = (y * sg_ref[rows, cols].astype(F32)).astype(BF16)
            upd = _dot_tn(v_p, k_out[:, kc])
            new_states.append(a_chunk[:, kc] * st + jnp.where(same_head, upd, 0.0))
        return new_states

    def fast_scores(p, q, k, b, q_in):
        kc = slice(p * LANES, (p + 1) * LANES)
        k_div = k[:, kc] * jnp.exp(-b[:, kc])
        k_blk = jnp.concatenate([jnp.where(first_head, k_div, 0.0),
                                 jnp.where(first_head, 0.0, k_div)], axis=0).astype(BF16)
        s2 = _dot_nt(q_in[:, kc], k_blk)
        return [jnp.where(ri >= ci, s2[:, j * c:(j + 1) * c], 0.0) for j in range(2)]

    def safe_scores(rows):
        q = q_ref[rows, :]
        k = k_ref[rows, :]
        qk = q * k
        scores = []
        for h in range(GLA_HEADS):
            dk = slice(h * GLA_DK, (h + 1) * GLA_DK)
            scores.append(jnp.where(ri == ci, jnp.sum(qk[:, dk], axis=-1, keepdims=True), 0.0))
        half = c // 2
        while half >= 1:
            ref = (ri & (-2 * half)) + (half - 1)
            lo = jnp.minimum(ri, ref)
            hi = jnp.maximum(ri, ref)
            e = jnp.exp(masked_sum((ci > lo) & (ci <= hi), rows))
            q_l = (q * e).astype(BF16)
            k_l = (k * e).astype(BF16)
            pair = (((ri ^ ci) & (-2 * half)) == 0) & ((ri & half) != 0) & ((ci & half) == 0)
            for h in range(GLA_HEADS):
                dk = slice(h * GLA_DK, (h + 1) * GLA_DK)
                scores[h] = jnp.where(pair, _dot_nt(q_l[:, dk], k_l[:, dk]), scores[h])
            half //= 2
        return scores

    @pl.when(fast)
    def _():
        states = [st_ref[p] for p in range(GLA_HEADS // 2)]
        for s in range(GLA_NSUB):
            states = chunk(slice(s * c, (s + 1) * c), states, fast_scores)
        for p in range(GLA_HEADS // 2):
            st_ref[p] = states[p]

    @pl.when(jnp.logical_not(fast))
    def _():
        def body(s, carry):
            rows = pl.ds(pl.multiple_of(s * c, c), c)
            scores = safe_scores(rows)
            states = chunk(rows, [st_ref[p] for p in range(GLA_HEADS // 2)],
                           lambda p, *_: scores[2 * p:2 * p + 2])
            for p in range(GLA_HEADS // 2):
                st_ref[p] = states[p]
            return carry

        lax.fori_loop(0, GLA_NSUB, body, 0)


def _gla_call(q, k, la, v, sg, gn, w2):
    s = q.shape[0]
    n = s // TM_GLA
    row = lambda i: (i, 0)
    const = lambda i: (0, 0)
    return pl.pallas_call(
        _gla_kernel,
        grid=(n,),
        in_specs=[
            pl.BlockSpec((TM_GLA, GLA_KEY), row),
            pl.BlockSpec((TM_GLA, GLA_KEY), row),
            pl.BlockSpec((TM_GLA, GLA_KEY), row),
            pl.BlockSpec((TM_GLA, GLA_VAL), row),
            pl.BlockSpec((TM_GLA, GLA_VAL), row),
            pl.BlockSpec((1, GLA_DV), const),
            pl.BlockSpec((D_FF // n, D_MODEL), row),
        ],
        out_specs=[
            pl.BlockSpec((TM_GLA, GLA_VAL), row),
            pl.BlockSpec((D_FF // n, D_MODEL), row),
        ],
        out_shape=[
            jax.ShapeDtypeStruct((s, GLA_VAL), BF16),
            jax.ShapeDtypeStruct((D_FF, D_MODEL), BF16),
        ],
        scratch_shapes=[
            pltpu.VMEM((GLA_HEADS // 2, 2 * GLA_DV, LANES), F32),
            pltpu.VMEM((TM_GLA, GLA_KEY), F32),
        ],
        compiler_params=pltpu.CompilerParams(
            dimension_semantics=("arbitrary",), vmem_limit_bytes=VMEM_LIMIT),
        name="gla",
    )(q, k, la, v, sg, gn, w2)


def _group_mean(t, low_half, size):
    lo = jnp.sum(jnp.where(low_half, t, 0.0), axis=-1, keepdims=True)
    hi = jnp.sum(jnp.where(low_half, 0.0, t), axis=-1, keepdims=True)
    return jnp.where(low_half, lo, hi) * (1.0 / size)


def _interleave(mxu_items, vec_items):
    done = 0
    for i, item in enumerate(mxu_items):
        item()
        upto = (i + 1) * len(vec_items) // len(mxu_items)
        for v_item in vec_items[done:upto]:
            v_item()
        done = upto


def _mlp_conv_kernel(x_ref, og_ref, un_ref, u0_ref, cw_ref, cb_ref, gg_ref, gb_ref,
                     wo_hbm, g2_ref, w1_hbm, w2_hbm, gf_ref,
                     o_ref, ext_ref, sh_ref, oc_ref, f_ref, wo_ref, w1_ref, w2_ref, w_sem):
    tm = TM_MLP

    def weight_copies():
        return [pltpu.make_async_copy(src, dst, w_sem.at[j])
                for j, (src, dst) in enumerate(((wo_hbm, wo_ref), (w1_hbm, w1_ref), (w2_hbm, w2_ref)))]

    grp = CONV_CH // CONV_GROUPS
    low_half = lax.broadcasted_iota(jnp.int32, (CONV_RB, LANES), 1) < grp
    first = CONV_HALO - (CONV_W - 1)

    def shifted_copy(p):
        sh_ref[p - 1] = ext_ref[p:p + CONV_SH_ROWS, :]

    def conv_block(base):
        for cg in range(CONV_CH // LANES):
            cols = slice(cg * LANES, (cg + 1) * LANES)
            acc = jnp.broadcast_to(cb_ref[:, cols], (CONV_RB, LANES))
            for t in range(CONV_W):
                shift = first + t
                p = shift % SUBLANES
                rows = pl.ds(base + (shift - p), CONV_RB)
                tap = ext_ref[rows, cols] if p == 0 else sh_ref[p - 1, rows, cols]
                acc = acc + cw_ref[t:t + 1, cols] * tap
            d = acc - _group_mean(acc, low_half, grp)
            var = _group_mean(d * d, low_half, grp)
            yn = d * lax.rsqrt(var + EPS) * gg_ref[:, cols] + gb_ref[:, cols]
            oc_ref[pl.ds(base, CONV_RB), cols] = (yn * _sigmoid(yn)).astype(BF16)

    @pl.when(pl.program_id(0) == 0)
    def _():
        for cp in weight_copies():
            cp.start()
        ext_ref[0:CONV_HALO, :] = jnp.zeros((CONV_HALO, CONV_CH), F32)
        ext_ref[CONV_HALO:, :] = u0_ref[...]
        for p in range(1, SUBLANES):
            shifted_copy(p)

        def body(r, carry):
            conv_block(pl.multiple_of(r * CONV_RB, CONV_RB))
            return carry

        lax.fori_loop(0, tm // CONV_RB, body, 0)
        for cp in weight_copies():
            cp.wait()

    h = x_ref[...] + _dot(og_ref[...], wo_ref[0:GLA_VAL, :]) + _dot(oc_ref[...], wo_ref[GLA_VAL:, :])
    hg = (h * g2_ref[...]).astype(BF16)
    r = lax.rsqrt(jnp.mean(h * h, axis=-1, keepdims=True) + EPS)
    r2 = r * r

    def roll_window():
        ext_ref[0:CONV_HALO, :] = ext_ref[tm:tm + CONV_HALO, :]
        ext_ref[CONV_HALO:, :] = un_ref[...]

    vec_items = [roll_window] + [functools.partial(shifted_copy, p) for p in range(1, SUBLANES)]
    vec_items += [functools.partial(conv_block, r * CONV_RB) for r in range(tm // CONV_RB)]

    def ff_in(j):
        cols = slice(j * FF_CHUNK, (j + 1) * FF_CHUNK)
        f = jnp.maximum(_dot(hg, w1_ref[:, cols]), 0.0)
        f_ref[:, cols] = (f * f * r2).astype(BF16)

    def ff_out(j):
        cols = slice(j * OUT_CHUNK, (j + 1) * OUT_CHUNK)
        o_ref[:, cols] = h[:, cols] + _dot(f_ref[...], w2_ref[:, cols])

    mxu_items = [functools.partial(ff_in, j) for j in range(D_FF // FF_CHUNK)]
    mxu_items += [functools.partial(ff_out, j) for j in range(D_MODEL // OUT_CHUNK)]
    _interleave(mxu_items, vec_items)

    acc = o_ref[...]
    ms2 = jnp.mean(acc * acc, axis=-1, keepdims=True)
    o_ref[...] = acc * lax.rsqrt(ms2 + EPS) * gf_ref[...]


def _mlp_conv_call(x2, og, u, cw, cb, gg, gb, wo, g2, w1, w2, gf):
    s = x2.shape[0]
    n = s // TM_MLP
    row = lambda i: (i, 0)
    nxt = lambda i: (jnp.minimum(i + 1, n - 1), 0)
    const = lambda i: (0, 0)
    single = pl.Buffered(1)
    return pl.pallas_call(
        _mlp_conv_kernel,
        grid=(n,),
        in_specs=[
            pl.BlockSpec((TM_MLP, D_MODEL), row),
            pl.BlockSpec((TM_MLP, GLA_VAL), row),
            pl.BlockSpec((TM_MLP, CONV_CH), nxt),
            pl.BlockSpec((TM_MLP, CONV_CH), const, pipeline_mode=single),
            pl.BlockSpec((CONV_W, CONV_CH), const),
            pl.BlockSpec((1, CONV_CH), const),
            pl.BlockSpec((1, CONV_CH), const),
            pl.BlockSpec((1, CONV_CH), const),
            pl.BlockSpec(memory_space=pl.ANY),
            pl.BlockSpec((1, D_MODEL), const),
            pl.BlockSpec(memory_space=pl.ANY),
            pl.BlockSpec(memory_space=pl.ANY),
            pl.BlockSpec((1, D_MODEL), const),
        ],
        out_specs=pl.BlockSpec((TM_MLP, D_MODEL), row),
        out_shape=jax.ShapeDtypeStruct((s, D_MODEL), F32),
        scratch_shapes=[
            pltpu.VMEM((TM_MLP + CONV_HALO, CONV_CH), F32),
            pltpu.VMEM((SUBLANES - 1, CONV_SH_ROWS, CONV_CH), F32),
            pltpu.VMEM((TM_MLP, CONV_CH), BF16),
            pltpu.VMEM((TM_MLP, D_FF), BF16),
            pltpu.VMEM((D_MODEL, D_MODEL), BF16),
            pltpu.VMEM((D_MODEL, D_FF), BF16),
            pltpu.VMEM((D_FF, D_MODEL), BF16),
            pltpu.SemaphoreType.DMA((3,)),
        ],
        compiler_params=pltpu.CompilerParams(
            dimension_semantics=("arbitrary",), vmem_limit_bytes=VMEM_LIMIT),
        name="mlp_conv",
    )(x2, og, u, u, cw, cb, gg, gb, wo, g2, w1, w2, gf)


def kernel(x, norm1_g, w_in, w_gate_up, b_gate, gla_norm_g, conv_w, conv_b, conv_norm_g,
           conv_norm_b, w_out, norm2_g, w_mlp_in, w_mlp_out, final_norm_g):
    bsz, seq, _ = x.shape
    x2 = x.reshape(bsz * seq, D_MODEL)
    assert bsz == 1, "state / halo carry across grid steps assumes one sequence"
    l = 0
    assert w_in.shape[0] == 1, "single layer: the projection weight is passed with its layer axis"
    w_in_t = jnp.swapaxes(w_in, 1, 2)
    q, k, la, v, sg, u, wo_b, w1_b = _proj_call(
        x2, norm1_g[l][None, :], w_in_t, w_gate_up[l], b_gate[l][None, :], w_out[l], w_mlp_in[l])
    o_gla, w2_b = _gla_call(q, k, la, v, sg, gla_norm_g[l][None, :], w_mlp_out[l])
    out = _mlp_conv_call(x2, o_gla, u, conv_w[l], conv_b[l][None, :], conv_norm_g[l][None, :],
                         conv_norm_b[l][None, :], wo_b, norm2_g[l][None, :], w1_b, w2_b,
                         final_norm_g[None, :])
    return out.reshape(bsz, seq, D_MODEL)
```

```python
import functools

import jax
import jax.numpy as jnp
from jax import lax
from jax.experimental import pallas as pl
from jax.experimental.pallas import tpu as pltpu

D_MODEL = 1024
GLA_HEADS = 4
GLA_DV = 128
GLA_DK = 64
GLA_KEY = GLA_HEADS * GLA_DK
GLA_VAL = GLA_HEADS * GLA_DV
GATE_RANK = 16
GATE_TAU = 16.0
CONV_CH = 512
CONV_GROUPS = 8
CONV_W = 31
D_FF = 4 * D_MODEL
EPS = 1e-6

OFF_Q = 0
OFF_K = OFF_Q + GLA_KEY
OFF_V = OFF_K + GLA_KEY
OFF_G = OFF_V + GLA_VAL
OFF_A = OFF_G + GLA_VAL
OFF_C = OFF_A + GATE_RANK
D_IN = OFF_C + 2 * CONV_CH

LANES = 128
SUBLANES = 8
Z_PAD = LANES
P_Q = 0
P_K = P_Q + GLA_KEY
P_V = P_K + GLA_KEY
P_G = P_V + GLA_VAL
P_CI = P_G + GLA_VAL
P_CG = P_CI + CONV_CH
P_Z = P_CG + CONV_CH
D_INP = P_Z + Z_PAD

TM_PROJ = 1024
GLA_SUB = 256
GLA_NSUB = 8
TM_GLA = GLA_SUB * GLA_NSUB
TM_MLP = 512
FF_CHUNK = 512
OUT_CHUNK = 256
CONV_HALO = 32
CONV_RB = 32
CONV_SH_ROWS = TM_MLP + CONV_HALO - SUBLANES
GLA_FAST_MIN_B = -60.0

V7X_VMEM_BYTES = 64 * 1024 * 1024
VMEM_LIMIT = V7X_VMEM_BYTES - 8 * 1024 * 1024

F32 = jnp.float32
BF16 = jnp.bfloat16


def _sigmoid(x):
    return 1.0 / (1.0 + jnp.exp(-x))


def _split_bf16(x):
    hi = x.astype(BF16)
    lo = (x - hi.astype(F32)).astype(BF16)
    return hi, lo


def _dot(a, b):
    return jnp.dot(a, b, preferred_element_type=F32)


def _dot_nt(a, b):
    return lax.dot_general(a, b, (((1,), (1,)), ((), ())), preferred_element_type=F32)


def _dot_tn(a, b):
    return lax.dot_general(a, b, (((0,), (0,)), ((), ())), preferred_element_type=F32)


def _proj_kernel(x_ref, g1_ref, w_ref, wg_ref, bg_ref, wo_ref, w1_ref, w2_ref,
                 q_ref, k_ref, la_ref, v_ref, sg_ref, u_ref, wob_ref, w1b_ref, w2b_ref,
                 wb_ref, wgb_ref):
    wob_ref[...] = wo_ref[...].astype(BF16)
    w1b_ref[...] = w1_ref[...].astype(BF16)
    w2b_ref[...] = w2_ref[...].astype(BF16)

    @pl.when(pl.program_id(0) == 0)
    def _():
        wb_ref[P_Q:P_CI, :] = w_ref[OFF_Q:OFF_A, :].astype(BF16)
        wb_ref[P_CI:P_Z, :] = w_ref[OFF_C:D_IN, :].astype(BF16)
        wb_ref[P_Z:P_Z + GATE_RANK, :] = w_ref[OFF_A:OFF_C, :].astype(BF16)
        wb_ref[P_Z + GATE_RANK:D_INP, :] = jnp.zeros((Z_PAD - GATE_RANK, D_MODEL), BF16)
        wgb_ref[...] = jnp.zeros_like(wgb_ref)
        wgb_ref[0:GATE_RANK, :] = wg_ref[...].astype(BF16)

    x = x_ref[...]
    ms = jnp.mean(x * x, axis=-1, keepdims=True)
    xn = (x * lax.rsqrt(ms + EPS) * g1_ref[...]).astype(BF16)
    proj = _dot_nt(xn, wb_ref[...])
    q_ref[...] = (proj[:, P_Q:P_K] * (GLA_DK ** -0.5)).astype(BF16)
    k_ref[...] = proj[:, P_K:P_V].astype(BF16)
    v_ref[...] = proj[:, P_V:P_G].astype(BF16)
    g = proj[:, P_G:P_CI]
    sg_ref[...] = (g * _sigmoid(g)).astype(BF16)
    u_ref[...] = proj[:, P_CI:P_CG] * _sigmoid(proj[:, P_CG:P_Z])
    z = proj[:, P_Z:D_INP].astype(BF16)
    a_logit = _dot(z, wgb_ref[...]) + bg_ref[...]
    la = jnp.minimum(a_logit, 0.0) - jnp.log(1.0 + jnp.exp(-jnp.abs(a_logit)))
    la_ref[...] = la * (1.0 / GATE_TAU)


def _proj_call(x2, g1, w_in_t, w_gate_up, bg, wo, w1, w2):
    s = x2.shape[0]
    n = s // TM_PROJ
    row = lambda i: (i, 0)
    col = lambda i: (0, i)
    const = lambda i: (0, 0)
    return pl.pallas_call(
        _proj_kernel,
        grid=(n,),
        in_specs=[
            pl.BlockSpec((TM_PROJ, D_MODEL), row),
            pl.BlockSpec((1, D_MODEL), const),
            pl.BlockSpec((None, D_IN, D_MODEL), lambda i: (0, 0, 0), pipeline_mode=pl.Buffered(1)),
            pl.BlockSpec((GATE_RANK, GLA_KEY), const),
            pl.BlockSpec((1, GLA_KEY), const),
            pl.BlockSpec((D_MODEL // n, D_MODEL), row),
            pl.BlockSpec((D_MODEL, D_FF // n), col),
            pl.BlockSpec((D_FF // n, D_MODEL), row),
        ],
        out_specs=[
            pl.BlockSpec((TM_PROJ, GLA_KEY), row),
            pl.BlockSpec((TM_PROJ, GLA_KEY), row),
            pl.BlockSpec((TM_PROJ, GLA_KEY), row),
            pl.BlockSpec((TM_PROJ, GLA_VAL), row),
            pl.BlockSpec((TM_PROJ, GLA_VAL), row),
            pl.BlockSpec((TM_PROJ, CONV_CH), row),
            pl.BlockSpec((D_MODEL // n, D_MODEL), row),
            pl.BlockSpec((D_MODEL, D_FF // n), col),
            pl.BlockSpec((D_FF // n, D_MODEL), row),
        ],
        out_shape=[
            jax.ShapeDtypeStruct((s, GLA_KEY), BF16),
            jax.ShapeDtypeStruct((s, GLA_KEY), BF16),
            jax.ShapeDtypeStruct((s, GLA_KEY), F32),
            jax.ShapeDtypeStruct((s, GLA_VAL), BF16),
            jax.ShapeDtypeStruct((s, GLA_VAL), BF16),
            jax.ShapeDtypeStruct((s, CONV_CH), F32),
            jax.ShapeDtypeStruct((D_MODEL, D_MODEL), BF16),
            jax.ShapeDtypeStruct((D_MODEL, D_FF), BF16),
            jax.ShapeDtypeStruct((D_FF, D_MODEL), BF16),
        ],
        scratch_shapes=[
            pltpu.VMEM((D_INP, D_MODEL), BF16),
            pltpu.VMEM((Z_PAD, GLA_KEY), BF16),
        ],
        compiler_params=pltpu.CompilerParams(
            dimension_semantics=("arbitrary",), vmem_limit_bytes=VMEM_LIMIT),
        name="proj",
    )(x2, g1, w_in_t, w_gate_up, bg, wo, w1, w2)


def _gla_kernel(q_ref, k_ref, la_ref, v_ref, sg_ref, gn_ref, o_ref, st_ref, b_ref):
    c = GLA_SUB

    @pl.when(pl.program_id(0) == 0)
    def _():
        st_ref[...] = jnp.zeros_like(st_ref)

    ri = lax.broadcasted_iota(jnp.int32, (c, c), 0)
    ci = lax.broadcasted_iota(jnp.int32, (c, c), 1)
    first_head = lax.broadcasted_iota(jnp.int32, (c, LANES), 1) < GLA_DK
    st_row = lax.broadcasted_iota(jnp.int32, (2 * GLA_DV, LANES), 0)
    st_lane = lax.broadcasted_iota(jnp.int32, (2 * GLA_DV, LANES), 1)
    same_head = (st_row < GLA_DV) == (st_lane < GLA_DK)

    def masked_sum(m, rows):
        mb = m.astype(BF16)
        la_hi, la_lo = _split_bf16(la_ref[rows, :])
        return _dot(mb, la_hi) + _dot(mb, la_lo)

    b_min = None
    for s in range(GLA_NSUB):
        rows = slice(s * c, (s + 1) * c)
        b = masked_sum(ci <= ri, rows)
        b_ref[rows, :] = b
        b_min = b[c - 1:c, :] if b_min is None else jnp.minimum(b_min, b[c - 1:c, :])
    fast = jnp.min(b_min) > GLA_FAST_MIN_B

    def chunk(rows, states, scores_fn):
        b = b_ref[rows, :]
        b_last = b[c - 1:c, :]
        q = q_ref[rows, :].astype(F32)
        k = k_ref[rows, :].astype(F32)
        q_in = (q * jnp.exp(b)).astype(BF16)
        k_out = (k * jnp.exp(b_last - b)).astype(BF16)
        a_chunk = jnp.exp(b_last)
        new_states = []
        for p in range(GLA_HEADS // 2):
            kc = slice(p * LANES, (p + 1) * LANES)
            v_p = v_ref[rows, 2 * p * GLA_DV:2 * (p + 1) * GLA_DV]
            st = states[p]
            o_inter = _dot_nt(q_in[:, kc], st.astype(BF16))
            for j, s_h in enumerate(scores_fn(p, q, k, b, q_in)):
                h = 2 * p + j
                dv = slice(j * GLA_DV, (j + 1) * GLA_DV)
                o_h = _dot(s_h.astype(BF16), v_p[:, dv]) + o_inter[:, dv]
                ms = jnp.mean(o_h * o_h, axis=-1, keepdims=True)
                y = o_h * lax.rsqrt(ms + EPS) * gn_ref[...]
                cols = slice(h * GLA_DV, (h + 1) * GLA_DV)
                o_ref[rows, cols] = (y * sg_ref[rows, cols].astype(F32)).astype(BF16)
            upd = _dot_tn(v_p, k_out[:, kc])
            new_states.append(a_chunk[:, kc] * st + jnp.where(same_head, upd, 0.0))
        return new_states

    def fast_scores(p, q, k, b, q_in):
        kc = slice(p * LANES, (p + 1) * LANES)
        k_div = k[:, kc] * jnp.exp(-b[:, kc])
        k_blk = jnp.concatenate([jnp.where(first_head, k_div, 0.0),
                                 jnp.where(first_head, 0.0, k_div)], axis=0).astype(BF16)
        s2 = _dot_nt(q_in[:, kc], k_blk)
        return [jnp.where(ri >= ci, s2[:, j * c:(j + 1) * c], 0.0) for j in range(2)]

    def safe_scores(rows):
        q = q_ref[rows, :].astype(F32)
        k = k_ref[rows, :].astype(F32)
        qk = q * k
        scores = []
        for h in range(GLA_HEADS):
            dk = slice(h * GLA_DK, (h + 1) * GLA_DK)
            scores.append(jnp.where(ri == ci, jnp.sum(qk[:, dk], axis=-1, keepdims=True), 0.0))
        half = c // 2
        while half >= 1:
            ref = (ri & (-2 * half)) + (half - 1)
            lo = jnp.minimum(ri, ref)
            hi = jnp.maximum(ri, ref)
            e = jnp.exp(masked_sum((ci > lo) & (ci <= hi), rows))
            q_l = (q * e).astype(BF16)
            k_l = (k * e).astype(BF16)
            pair = (((ri ^ ci) & (-2 * half)) == 0) & ((ri & half) != 0) & ((ci & half) == 0)
            for h in range(GLA_HEADS):
                dk = slice(h * GLA_DK, (h + 1) * GLA_DK)
                scores[h] = jnp.where(pair, _dot_nt(q_l[:, dk], k_l[:, dk]), scores[h])
            half //= 2
        return scores

    @pl.when(fast)
    def _():
        states = [st_ref[p] for p in range(GLA_HEADS // 2)]
        for s in range(GLA_NSUB):
            states = chunk(slice(s * c, (s + 1) * c), states, fast_scores)
        for p in range(GLA_HEADS // 2):
            st_ref[p] = states[p]

    @pl.when(jnp.logical_not(fast))
    def _():
        def body(s, carry):
            rows = pl.ds(pl.multiple_of(s * c, c), c)
            scores = safe_scores(rows)
            states = chunk(rows, [st_ref[p] for p in range(GLA_HEADS // 2)],
                           lambda p, *_: scores[2 * p:2 * p + 2])
            for p in range(GLA_HEADS // 2):
                st_ref[p] = states[p]
            return carry

        lax.fori_loop(0, GLA_NSUB, body, 0)


def _gla_call(q, k, la, v, sg, gn):
    s = q.shape[0]
    n = s // TM_GLA
    row = lambda i: (i, 0)
    const = lambda i: (0, 0)
    return pl.pallas_call(
        _gla_kernel,
        grid=(n,),
        in_specs=[
            pl.BlockSpec((TM_GLA, GLA_KEY), row),
            pl.BlockSpec((TM_GLA, GLA_KEY), row),
            pl.BlockSpec((TM_GLA, GLA_KEY), row),
            pl.BlockSpec((TM_GLA, GLA_VAL), row),
            pl.BlockSpec((TM_GLA, GLA_VAL), row),
            pl.BlockSpec((1, GLA_DV), const),
        ],
        out_specs=pl.BlockSpec((TM_GLA, GLA_VAL), row),
        out_shape=jax.ShapeDtypeStruct((s, GLA_VAL), BF16),
        scratch_shapes=[
            pltpu.VMEM((GLA_HEADS // 2, 2 * GLA_DV, LANES), F32),
            pltpu.VMEM((TM_GLA, GLA_KEY), F32),
        ],
        compiler_params=pltpu.CompilerParams(
            dimension_semantics=("arbitrary",), vmem_limit_bytes=VMEM_LIMIT),
        name="gla",
    )(q, k, la, v, sg, gn)


def _group_mean(t, low_half, size):
    lo = jnp.sum(jnp.where(low_half, t, 0.0), axis=-1, keepdims=True)
    hi = jnp.sum(jnp.where(low_half, 0.0, t), axis=-1, keepdims=True)
    return jnp.where(low_half, lo, hi) * (1.0 / size)


def _interleave(mxu_items, vec_items):
    done = 0
    for i, item in enumerate(mxu_items):
        item()
        upto = (i + 1) * len(vec_items) // len(mxu_items)
        for v_item in vec_items[done:upto]:
            v_item()
        done = upto


def _mlp_conv_kernel(x_ref, og_ref, un_ref, u0_ref, cw_ref, cb_ref, gg_ref, gb_ref,
                     wo_hbm, g2_ref, w1_hbm, w2_hbm, gf_ref,
                     o_ref, ext_ref, sh_ref, oc_ref, f_ref, wo_ref, w1_ref, w2_ref, w_sem):
    tm = TM_MLP

    def weight_copies():
        return [pltpu.make_async_copy(src, dst, w_sem.at[j])
                for j, (src, dst) in enumerate(((wo_hbm, wo_ref), (w1_hbm, w1_ref), (w2_hbm, w2_ref)))]

    grp = CONV_CH // CONV_GROUPS
    low_half = lax.broadcasted_iota(jnp.int32, (CONV_RB, LANES), 1) < grp
    first = CONV_HALO - (CONV_W - 1)

    def shifted_copy(p):
        sh_ref[p - 1] = ext_ref[p:p + CONV_SH_ROWS, :]

    def conv_block(base):
        for cg in range(CONV_CH // LANES):
            cols = slice(cg * LANES, (cg + 1) * LANES)
            acc = jnp.broadcast_to(cb_ref[:, cols], (CONV_RB, LANES))
            for t in range(CONV_W):
                shift = first + t
                p = shift % SUBLANES
                rows = pl.ds(base + (shift - p), CONV_RB)
                tap = ext_ref[rows, cols] if p == 0 else sh_ref[p - 1, rows, cols]
                acc = acc + cw_ref[t:t + 1, cols] * tap
            d = acc - _group_mean(acc, low_half, grp)
            var = _group_mean(d * d, low_half, grp)
            yn = d * lax.rsqrt(var + EPS) * gg_ref[:, cols] + gb_ref[:, cols]
            oc_ref[pl.ds(base, CONV_RB), cols] = (yn * _sigmoid(yn)).astype(BF16)

    @pl.when(pl.program_id(0) == 0)
    def _():
        for cp in weight_copies():
            cp.start()
        ext_ref[0:CONV_HALO, :] = jnp.zeros((CONV_HALO, CONV_CH), F32)
        ext_ref[CONV_HALO:, :] = u0_ref[...]
        for p in range(1, SUBLANES):
            shifted_copy(p)

        def body(r, carry):
            conv_block(pl.multiple_of(r * CONV_RB, CONV_RB))
            return carry

        lax.fori_loop(0, tm // CONV_RB, body, 0)
        for cp in weight_copies():
            cp.wait()

    h = x_ref[...] + _dot(og_ref[...], wo_ref[0:GLA_VAL, :]) + _dot(oc_ref[...], wo_ref[GLA_VAL:, :])
    hg = (h * g2_ref[...]).astype(BF16)
    r = lax.rsqrt(jnp.mean(h * h, axis=-1, keepdims=True) + EPS)
    r2 = r * r

    def roll_window():
        ext_ref[0:CONV_HALO, :] = ext_ref[tm:tm + CONV_HALO, :]
        ext_ref[CONV_HALO:, :] = un_ref[...]

    vec_items = [roll_window] + [functools.partial(shifted_copy, p) for p in range(1, SUBLANES)]
    vec_items += [functools.partial(conv_block, r * CONV_RB) for r in range(tm // CONV_RB)]

    def ff_in(j):
        cols = slice(j * FF_CHUNK, (j + 1) * FF_CHUNK)
        f = jnp.maximum(_dot(hg, w1_ref[:, cols]), 0.0)
        f_ref[:, cols] = (f * f * r2).astype(BF16)

    def ff_out(j):
        cols = slice(j * OUT_CHUNK, (j + 1) * OUT_CHUNK)
        o_ref[:, cols] = h[:, cols] + _dot(f_ref[...], w2_ref[:, cols])

    mxu_items = [functools.partial(ff_in, j) for j in range(D_FF // FF_CHUNK)]
    mxu_items += [functools.partial(ff_out, j) for j in range(D_MODEL // OUT_CHUNK)]
    _interleave(mxu_items, vec_items)

    acc = o_ref[...]
    ms2 = jnp.mean(acc * acc, axis=-1, keepdims=True)
    o_ref[...] = acc * lax.rsqrt(ms2 + EPS) * gf_ref[...]


def _mlp_conv_call(x2, og, u, cw, cb, gg, gb, wo, g2, w1, w2, gf):
    s = x2.shape[0]
    n = s // TM_MLP
    row = lambda i: (i, 0)
    nxt = lambda i: (jnp.minimum(i + 1, n - 1), 0)
    const = lambda i: (0, 0)
    single = pl.Buffered(1)
    return pl.pallas_call(
        _mlp_conv_kernel,
        grid=(n,),
        in_specs=[
            pl.BlockSpec((TM_MLP, D_MODEL), row),
            pl.BlockSpec((TM_MLP, GLA_VAL), row),
            pl.BlockSpec((TM_MLP, CONV_CH), nxt),
            pl.BlockSpec((TM_MLP, CONV_CH), const, pipeline_mode=single),
            pl.BlockSpec((CONV_W, CONV_CH), const),
            pl.BlockSpec((1, CONV_CH), const),
            pl.BlockSpec((1, CONV_CH), const),
            pl.BlockSpec((1, CONV_CH), const),
            pl.BlockSpec(memory_space=pl.ANY),
            pl.BlockSpec((1, D_MODEL), const),
            pl.BlockSpec(memory_space=pl.ANY),
            pl.BlockSpec(memory_space=pl.ANY),
            pl.BlockSpec((1, D_MODEL), const),
        ],
        out_specs=pl.BlockSpec((TM_MLP, D_MODEL), row),
        out_shape=jax.ShapeDtypeStruct((s, D_MODEL), F32),
        scratch_shapes=[
            pltpu.VMEM((TM_MLP + CONV_HALO, CONV_CH), F32),
            pltpu.VMEM((SUBLANES - 1, CONV_SH_ROWS, CONV_CH), F32),
            pltpu.VMEM((TM_MLP, CONV_CH), BF16),
            pltpu.VMEM((TM_MLP, D_FF), BF16),
            pltpu.VMEM((D_MODEL, D_MODEL), BF16),
            pltpu.VMEM((D_MODEL, D_FF), BF16),
            pltpu.VMEM((D_FF, D_MODEL), BF16),
            pltpu.SemaphoreType.DMA((3,)),
        ],
        compiler_params=pltpu.CompilerParams(
            dimension_semantics=("arbitrary",), vmem_limit_bytes=VMEM_LIMIT),
        name="mlp_conv",
    )(x2, og, u, u, cw, cb, gg, gb, wo, g2, w1, w2, gf)


def kernel(x, norm1_g, w_in, w_gate_up, b_gate, gla_norm_g, conv_w, conv_b, conv_norm_g,
           conv_norm_b, w_out, norm2_g, w_mlp_in, w_mlp_out, final_norm_g):
    bsz, seq, _ = x.shape
    x2 = x.reshape(bsz * seq, D_MODEL)
    assert bsz == 1, "state / halo carry across grid steps assumes one sequence"
    l = 0
    assert w_in.shape[0] == 1, "single layer: the projection weight is passed with its layer axis"
    w_in_t = jnp.swapaxes(w_in, 1, 2)
    q, k, la, v, sg, u, wo_b, w1_b, w2_b = _proj_call(
        x2, norm1_g[l][None, :], w_in_t, w_gate_up[l], b_gate[l][None, :], w_out[l], w_mlp_in[l],
        w_mlp_out[l])
    o_gla = _gla_call(q, k, la, v, sg, gla_norm_g[l][None, :])
    out = _mlp_conv_call(x2, o_gla, u, conv_w[l], conv_b[l][None, :], conv_norm_g[l][None, :],
                         conv_norm_b[l][None, :], wo_b, norm2_g[l][None, :], w1_b, w2_b,
                         final_norm_g[None, :])
    return out.reshape(bsz, seq, D_MODEL)
```

```python
import functools

import jax
import jax.numpy as jnp
from jax import lax
from jax.experimental import pallas as pl
from jax.experimental.pallas import tpu as pltpu

D_MODEL = 1024
GLA_HEADS = 4
GLA_DV = 128
GLA_DK = 64
GLA_KEY = GLA_HEADS * GLA_DK
GLA_VAL = GLA_HEADS * GLA_DV
GATE_RANK = 16
GATE_TAU = 16.0
CONV_CH = 512
CONV_GROUPS = 8
CONV_W = 31
D_FF = 4 * D_MODEL
EPS = 1e-6

OFF_Q = 0
OFF_K = OFF_Q + GLA_KEY
OFF_V = OFF_K + GLA_KEY
OFF_G = OFF_V + GLA_VAL
OFF_A = OFF_G + GLA_VAL
OFF_C = OFF_A + GATE_RANK
D_IN = OFF_C + 2 * CONV_CH

LANES = 128
SUBLANES = 8
Z_PAD = LANES
P_Q = 0
P_K = P_Q + GLA_KEY
P_V = P_K + GLA_KEY
P_G = P_V + GLA_VAL
P_CI = P_G + GLA_VAL
P_CG = P_CI + CONV_CH
P_Z = P_CG + CONV_CH
D_INP = P_Z + Z_PAD

TM_PROJ = 1024
GLA_SUB = 256
GLA_NSUB = 8
TM_GLA = GLA_SUB * GLA_NSUB
TM_MLP = 512
FF_CHUNK = 512
OUT_CHUNK = 256
CONV_HALO = 32
CONV_RB = 32
CONV_SH_ROWS = TM_MLP + CONV_HALO - SUBLANES
GLA_FAST_MIN_B = -60.0

V7X_VMEM_BYTES = 64 * 1024 * 1024
VMEM_LIMIT = V7X_VMEM_BYTES - 8 * 1024 * 1024

F32 = jnp.float32
BF16 = jnp.bfloat16


def _sigmoid(x):
    return 1.0 / (1.0 + jnp.exp(-x))


def _split_bf16(x):
    hi = x.astype(BF16)
    lo = (x - hi.astype(F32)).astype(BF16)
    return hi, lo


def _dot(a, b):
    return jnp.dot(a, b, preferred_element_type=F32)


def _dot_nt(a, b):
    return lax.dot_general(a, b, (((1,), (1,)), ((), ())), preferred_element_type=F32)


def _dot_tn(a, b):
    return lax.dot_general(a, b, (((0,), (0,)), ((), ())), preferred_element_type=F32)


def _proj_kernel(x_ref, g1_ref, w_ref, wg_ref, bg_ref, wo_ref, w1_ref, w2_ref,
                 q_ref, k_ref, la_ref, v_ref, sg_ref, u_ref, wob_ref, w1b_ref, w2b_ref,
                 wb_ref, wgb_ref):
    wob_ref[...] = wo_ref[...].astype(BF16)
    w1b_ref[...] = w1_ref[...].astype(BF16)
    w2b_ref[...] = w2_ref[...].astype(BF16)

    @pl.when(pl.program_id(0) == 0)
    def _():
        wb_ref[P_Q:P_CI, :] = w_ref[OFF_Q:OFF_A, :].astype(BF16)
        wb_ref[P_CI:P_Z, :] = w_ref[OFF_C:D_IN, :].astype(BF16)
        wb_ref[P_Z:P_Z + GATE_RANK, :] = w_ref[OFF_A:OFF_C, :].astype(BF16)
        wb_ref[P_Z + GATE_RANK:D_INP, :] = jnp.zeros((Z_PAD - GATE_RANK, D_MODEL), BF16)
        wgb_ref[...] = jnp.zeros_like(wgb_ref)
        wgb_ref[0:GATE_RANK, :] = wg_ref[...].astype(BF16)

    x = x_ref[...]
    ms = jnp.mean(x * x, axis=-1, keepdims=True)
    xn = (x * lax.rsqrt(ms + EPS) * g1_ref[...]).astype(BF16)
    proj = _dot_nt(xn, wb_ref[...])
    q_ref[...] = (proj[:, P_Q:P_K] * (GLA_DK ** -0.5)).astype(BF16)
    k_ref[...] = proj[:, P_K:P_V].astype(BF16)
    v_ref[...] = proj[:, P_V:P_G].astype(BF16)
    g = proj[:, P_G:P_CI]
    sg_ref[...] = (g * _sigmoid(g)).astype(BF16)
    u_ref[...] = proj[:, P_CI:P_CG] * _sigmoid(proj[:, P_CG:P_Z])
    z = proj[:, P_Z:D_INP].astype(BF16)
    a_logit = _dot(z, wgb_ref[...]) + bg_ref[...]
    la = jnp.minimum(a_logit, 0.0) - jnp.log(1.0 + jnp.exp(-jnp.abs(a_logit)))
    la_ref[...] = la * (1.0 / GATE_TAU)


def _proj_call(x2, g1, w_in_t, w_gate_up, bg, wo, w1, w2):
    s = x2.shape[0]
    n = s // TM_PROJ
    row = lambda i: (i, 0)
    col = lambda i: (0, i)
    const = lambda i: (0, 0)
    return pl.pallas_call(
        _proj_kernel,
        grid=(n,),
        in_specs=[
            pl.BlockSpec((TM_PROJ, D_MODEL), row),
            pl.BlockSpec((1, D_MODEL), const),
            pl.BlockSpec((None, D_IN, D_MODEL), lambda i: (0, 0, 0), pipeline_mode=pl.Buffered(1)),
            pl.BlockSpec((GATE_RANK, GLA_KEY), const),
            pl.BlockSpec((1, GLA_KEY), const),
            pl.BlockSpec((D_MODEL // n, D_MODEL), row),
            pl.BlockSpec((D_MODEL, D_FF // n), col),
            pl.BlockSpec((D_FF // n, D_MODEL), row),
        ],
        out_specs=[
            pl.BlockSpec((TM_PROJ, GLA_KEY), row),
            pl.BlockSpec((TM_PROJ, GLA_KEY), row),
            pl.BlockSpec((TM_PROJ, GLA_KEY), row),
            pl.BlockSpec((TM_PROJ, GLA_VAL), row),
            pl.BlockSpec((TM_PROJ, GLA_VAL), row),
            pl.BlockSpec((TM_PROJ, CONV_CH), row),
            pl.BlockSpec((D_MODEL // n, D_MODEL), row),
            pl.BlockSpec((D_MODEL, D_FF // n), col),
            pl.BlockSpec((D_FF // n, D_MODEL), row),
        ],
        out_shape=[
            jax.ShapeDtypeStruct((s, GLA_KEY), BF16),
            jax.ShapeDtypeStruct((s, GLA_KEY), BF16),
            jax.ShapeDtypeStruct((s, GLA_KEY), F32),
            jax.ShapeDtypeStruct((s, GLA_VAL), BF16),
            jax.ShapeDtypeStruct((s, GLA_VAL), BF16),
            jax.ShapeDtypeStruct((s, CONV_CH), F32),
            jax.ShapeDtypeStruct((D_MODEL, D_MODEL), BF16),
            jax.ShapeDtypeStruct((D_MODEL, D_FF), BF16),
            jax.ShapeDtypeStruct((D_FF, D_MODEL), BF16),
        ],
        scratch_shapes=[
            pltpu.VMEM((D_INP, D_MODEL), BF16),
            pltpu.VMEM((Z_PAD, GLA_KEY), BF16),
        ],
        compiler_params=pltpu.CompilerParams(
            dimension_semantics=("arbitrary",), vmem_limit_bytes=VMEM_LIMIT),
        name="proj",
    )(x2, g1, w_in_t, w_gate_up, bg, wo, w1, w2)


def _gla_kernel(q_ref, k_ref, la_ref, v_ref, sg_ref, gn_ref, o_ref, st_ref, b_ref):
    c = GLA_SUB

    @pl.when(pl.program_id(0) == 0)
    def _():
        st_ref[...] = jnp.zeros_like(st_ref)

    ri = lax.broadcasted_iota(jnp.int32, (c, c), 0)
    ci = lax.broadcasted_iota(jnp.int32, (c, c), 1)
    first_head = lax.broadcasted_iota(jnp.int32, (c, LANES), 1) < GLA_DK
    st_row = lax.broadcasted_iota(jnp.int32, (2 * GLA_DV, LANES), 0)
    st_lane = lax.broadcasted_iota(jnp.int32, (2 * GLA_DV, LANES), 1)
    same_head = (st_row < GLA_DV) == (st_lane < GLA_DK)

    def masked_sum(m, rows):
        mb = m.astype(BF16)
        la_hi, la_lo = _split_bf16(la_ref[rows, :])
        return _dot(mb, la_hi) + _dot(mb, la_lo)

    b_min = None
    for s in range(GLA_NSUB):
        rows = slice(s * c, (s + 1) * c)
        b = masked_sum(ci <= ri, rows)
        b_ref[rows, :] = b
        b_min = b[c - 1:c, :] if b_min is None else jnp.minimum(b_min, b[c - 1:c, :])
    fast = jnp.min(b_min) > GLA_FAST_MIN_B

    def chunk_pre(rows, scores_fn):
        b = b_ref[rows, :]
        b_last = b[c - 1:c, :]
        q = q_ref[rows, :].astype(F32)
        k = k_ref[rows, :].astype(F32)
        q_in = (q * jnp.exp(b)).astype(BF16)
        k_out = (k * jnp.exp(b_last - b)).astype(BF16)
        intra, upds = [], []
        for p in range(GLA_HEADS // 2):
            kc = slice(p * LANES, (p + 1) * LANES)
            v_p = v_ref[rows, 2 * p * GLA_DV:2 * (p + 1) * GLA_DV]
            for j, s_h in enumerate(scores_fn(p, q, k, b, q_in)):
                intra.append(_dot(s_h.astype(BF16), v_p[:, j * GLA_DV:(j + 1) * GLA_DV]))
            upd = _dot_tn(v_p, k_out[:, kc])
            upds.append(jnp.where(same_head, upd, 0.0))
        return rows, q_in, jnp.exp(b_last), intra, upds

    def chunk_post(pre, states):
        rows, q_in, a_chunk, intra, upds = pre
        new_states = []
        for p in range(GLA_HEADS // 2):
            kc = slice(p * LANES, (p + 1) * LANES)
            st = states[p]
            o_inter = _dot_nt(q_in[:, kc], st.astype(BF16))
            for j in range(2):
                h = 2 * p + j
                o_h = intra[h] + o_inter[:, j * GLA_DV:(j + 1) * GLA_DV]
                ms = jnp.mean(o_h * o_h, axis=-1, keepdims=True)
                y = o_h * lax.rsqrt(ms + EPS) * gn_ref[...]
                cols = slice(h * GLA_DV, (h + 1) * GLA_DV)
                o_ref[rows, cols] = (y * sg_ref[rows, cols].astype(F32)).astype(BF16)
            new_states.append(a_chunk[:, kc] * st + upds[p])
        return new_states

    def fast_scores(p, q, k, b, q_in):
        kc = slice(p * LANES, (p + 1) * LANES)
        k_div = k[:, kc] * jnp.exp(-b[:, kc])
        k_blk = jnp.concatenate([jnp.where(first_head, k_div, 0.0),
                                 jnp.where(first_head, 0.0, k_div)], axis=0).astype(BF16)
        s2 = _dot_nt(q_in[:, kc], k_blk)
        return [jnp.where(ri >= ci, s2[:, j * c:(j + 1) * c], 0.0) for j in range(2)]

    def safe_scores(rows):
        q = q_ref[rows, :].astype(F32)
        k = k_ref[rows, :].astype(F32)
        qk = q * k
        scores = []
        for h in range(GLA_HEADS):
            dk = slice(h * GLA_DK, (h + 1) * GLA_DK)
            scores.append(jnp.where(ri == ci, jnp.sum(qk[:, dk], axis=-1, keepdims=True), 0.0))
        half = c // 2
        while half >= 1:
            ref = (ri & (-2 * half)) + (half - 1)
            lo = jnp.minimum(ri, ref)
            hi = jnp.maximum(ri, ref)
            e = jnp.exp(masked_sum((ci > lo) & (ci <= hi), rows))
            q_l = (q * e).astype(BF16)
            k_l = (k * e).astype(BF16)
            pair = (((ri ^ ci) & (-2 * half)) == 0) & ((ri & half) != 0) & ((ci & half) == 0)
            for h in range(GLA_HEADS):
                dk = slice(h * GLA_DK, (h + 1) * GLA_DK)
                scores[h] = jnp.where(pair, _dot_nt(q_l[:, dk], k_l[:, dk]), scores[h])
            half //= 2
        return scores

    @pl.when(fast)
    def _():
        pres = [chunk_pre(slice(s * c, (s + 1) * c), fast_scores) for s in range(GLA_NSUB)]
        states = [st_ref[p] for p in range(GLA_HEADS // 2)]
        for pre in pres:
            states = chunk_post(pre, states)
        for p in range(GLA_HEADS // 2):
            st_ref[p] = states[p]

    @pl.when(jnp.logical_not(fast))
    def _():
        def body(s, carry):
            rows = pl.ds(pl.multiple_of(s * c, c), c)
            scores = safe_scores(rows)
            pre = chunk_pre(rows, lambda p, *_: scores[2 * p:2 * p + 2])
            states = chunk_post(pre, [st_ref[p] for p in range(GLA_HEADS // 2)])
            for p in range(GLA_HEADS // 2):
                st_ref[p] = states[p]
            return carry

        lax.fori_loop(0, GLA_NSUB, body, 0)


def _gla_call(q, k, la, v, sg, gn):
    s = q.shape[0]
    n = s // TM_GLA
    row = lambda i: (i, 0)
    const = lambda i: (0, 0)
    return pl.pallas_call(
        _gla_kernel,
        grid=(n,),
        in_specs=[
            pl.BlockSpec((TM_GLA, GLA_KEY), row),
            pl.BlockSpec((TM_GLA, GLA_KEY), row),
            pl.BlockSpec((TM_GLA, GLA_KEY), row),
            pl.BlockSpec((TM_GLA, GLA_VAL), row),
            pl.BlockSpec((TM_GLA, GLA_VAL), row),
            pl.BlockSpec((1, GLA_DV), const),
        ],
        out_specs=pl.BlockSpec((TM_GLA, GLA_VAL), row),
        out_shape=jax.ShapeDtypeStruct((s, GLA_VAL), BF16),
        scratch_shapes=[
            pltpu.VMEM((GLA_HEADS // 2, 2 * GLA_DV, LANES), F32),
            pltpu.VMEM((TM_GLA, GLA_KEY), F32),
        ],
        compiler_params=pltpu.CompilerParams(
            dimension_semantics=("arbitrary",), vmem_limit_bytes=VMEM_LIMIT),
        name="gla",
    )(q, k, la, v, sg, gn)


def _group_mean(t, low_half, size):
    lo = jnp.sum(jnp.where(low_half, t, 0.0), axis=-1, keepdims=True)
    hi = jnp.sum(jnp.where(low_half, 0.0, t), axis=-1, keepdims=True)
    return jnp.where(low_half, lo, hi) * (1.0 / size)


def _interleave(mxu_items, vec_items):
    done = 0
    for i, item in enumerate(mxu_items):
        item()
        upto = (i + 1) * len(vec_items) // len(mxu_items)
        for v_item in vec_items[done:upto]:
            v_item()
        done = upto


def _mlp_conv_kernel(x_ref, og_ref, un_ref, u0_ref, cw_ref, cb_ref, gg_ref, gb_ref,
                     wo_hbm, g2_ref, w1_hbm, w2_hbm, gf_ref,
                     o_ref, ext_ref, sh_ref, oc_ref, f_ref, wo_ref, w1_ref, w2_ref, w_sem):
    tm = TM_MLP

    def weight_copies():
        return [pltpu.make_async_copy(src, dst, w_sem.at[j])
                for j, (src, dst) in enumerate(((wo_hbm, wo_ref), (w1_hbm, w1_ref), (w2_hbm, w2_ref)))]

    grp = CONV_CH // CONV_GROUPS
    low_half = lax.broadcasted_iota(jnp.int32, (CONV_RB, LANES), 1) < grp
    first = CONV_HALO - (CONV_W - 1)

    def shifted_copy(p):
        sh_ref[p - 1] = ext_ref[p:p + CONV_SH_ROWS, :]

    def conv_block(base):
        for cg in range(CONV_CH // LANES):
            cols = slice(cg * LANES, (cg + 1) * LANES)
            acc = jnp.broadcast_to(cb_ref[:, cols], (CONV_RB, LANES))
            for t in range(CONV_W):
                shift = first + t
                p = shift % SUBLANES
                rows = pl.ds(base + (shift - p), CONV_RB)
                tap = ext_ref[rows, cols] if p == 0 else sh_ref[p - 1, rows, cols]
                acc = acc + cw_ref[t:t + 1, cols] * tap
            d = acc - _group_mean(acc, low_half, grp)
            var = _group_mean(d * d, low_half, grp)
            yn = d * lax.rsqrt(var + EPS) * gg_ref[:, cols] + gb_ref[:, cols]
            oc_ref[pl.ds(base, CONV_RB), cols] = (yn * _sigmoid(yn)).astype(BF16)

    @pl.when(pl.program_id(0) == 0)
    def _():
        for cp in weight_copies():
            cp.start()
        ext_ref[0:CONV_HALO, :] = jnp.zeros((CONV_HALO, CONV_CH), F32)
        ext_ref[CONV_HALO:, :] = u0_ref[...]
        for p in range(1, SUBLANES):
            shifted_copy(p)

        def body(r, carry):
            conv_block(pl.multiple_of(r * CONV_RB, CONV_RB))
            return carry

        lax.fori_loop(0, tm // CONV_RB, body, 0)
        for cp in weight_copies():
            cp.wait()

    h = x_ref[...] + _dot(og_ref[...], wo_ref[0:GLA_VAL, :]) + _dot(oc_ref[...], wo_ref[GLA_VAL:, :])
    hg = (h * g2_ref[...]).astype(BF16)
    r = lax.rsqrt(jnp.mean(h * h, axis=-1, keepdims=True) + EPS)
    r2 = r * r

    def roll_window():
        ext_ref[0:CONV_HALO, :] = ext_ref[tm:tm + CONV_HALO, :]
        ext_ref[CONV_HALO:, :] = un_ref[...]

    vec_items = [roll_window] + [functools.partial(shifted_copy, p) for p in range(1, SUBLANES)]
    vec_items += [functools.partial(conv_block, r * CONV_RB) for r in range(tm // CONV_RB)]

    def ff_in(j):
        cols = slice(j * FF_CHUNK, (j + 1) * FF_CHUNK)
        f = jnp.maximum(_dot(hg, w1_ref[:, cols]), 0.0)
        f_ref[:, cols] = (f * f * r2).astype(BF16)

    def ff_out(j):
        cols = slice(j * OUT_CHUNK, (j + 1) * OUT_CHUNK)
        o_ref[:, cols] = h[:, cols] + _dot(f_ref[...], w2_ref[:, cols])

    mxu_items = [functools.partial(ff_in, j) for j in range(D_FF // FF_CHUNK)]
    mxu_items += [functools.partial(ff_out, j) for j in range(D_MODEL // OUT_CHUNK)]
    _interleave(mxu_items, vec_items)

    acc = o_ref[...]
    ms2 = jnp.mean(acc * acc, axis=-1, keepdims=True)
    o_ref[...] = acc * lax.rsqrt(ms2 + EPS) * gf_ref[...]


def _mlp_conv_call(x2, og, u, cw, cb, gg, gb, wo, g2, w1, w2, gf):
    s = x2.shape[0]
    n = s // TM_MLP
    row = lambda i: (i, 0)
    nxt = lambda i: (jnp.minimum(i + 1, n - 1), 0)
    const = lambda i: (0, 0)
    single = pl.Buffered(1)
    return pl.pallas_call(
        _mlp_conv_kernel,
        grid=(n,),
        in_specs=[
            pl.BlockSpec((TM_MLP, D_MODEL), row),
            pl.BlockSpec((TM_MLP, GLA_VAL), row),
            pl.BlockSpec((TM_MLP, CONV_CH), nxt),
            pl.BlockSpec((TM_MLP, CONV_CH), const, pipeline_mode=single),
            pl.BlockSpec((CONV_W, CONV_CH), const),
            pl.BlockSpec((1, CONV_CH), const),
            pl.BlockSpec((1, CONV_CH), const),
            pl.BlockSpec((1, CONV_CH), const),
            pl.BlockSpec(memory_space=pl.ANY),
            pl.BlockSpec((1, D_MODEL), const),
            pl.BlockSpec(memory_space=pl.ANY),
            pl.BlockSpec(memory_space=pl.ANY),
            pl.BlockSpec((1, D_MODEL), const),
        ],
        out_specs=pl.BlockSpec((TM_MLP, D_MODEL), row),
        out_shape=jax.ShapeDtypeStruct((s, D_MODEL), F32),
        scratch_shapes=[
            pltpu.VMEM((TM_MLP + CONV_HALO, CONV_CH), F32),
            pltpu.VMEM((SUBLANES - 1, CONV_SH_ROWS, CONV_CH), F32),
            pltpu.VMEM((TM_MLP, CONV_CH), BF16),
            pltpu.VMEM((TM_MLP, D_FF), BF16),
            pltpu.VMEM((D_MODEL, D_MODEL), BF16),
            pltpu.VMEM((D_MODEL, D_FF), BF16),
            pltpu.VMEM((D_FF, D_MODEL), BF16),
            pltpu.SemaphoreType.DMA((3,)),
        ],
        compiler_params=pltpu.CompilerParams(
            dimension_semantics=("arbitrary",), vmem_limit_bytes=VMEM_LIMIT),
        name="mlp_conv",
    )(x2, og, u, u, cw, cb, gg, gb, wo, g2, w1, w2, gf)


def kernel(x, norm1_g, w_in, w_gate_up, b_gate, gla_norm_g, conv_w, conv_b, conv_norm_g,
           conv_norm_b, w_out, norm2_g, w_mlp_in, w_mlp_out, final_norm_g):
    bsz, seq, _ = x.shape
    x2 = x.reshape(bsz * seq, D_MODEL)
    assert bsz == 1, "state / halo carry across grid steps assumes one sequence"
    l = 0
    assert w_in.shape[0] == 1, "single layer: the projection weight is passed with its layer axis"
    w_in_t = jnp.swapaxes(w_in, 1, 2)
    q, k, la, v, sg, u, wo_b, w1_b, w2_b = _proj_call(
        x2, norm1_g[l][None, :], w_in_t, w_gate_up[l], b_gate[l][None, :], w_out[l], w_mlp_in[l],
        w_mlp_out[l])
    o_gla = _gla_call(q, k, la, v, sg, gla_norm_g[l][None, :])
    out = _mlp_conv_call(x2, o_gla, u, conv_w[l], conv_b[l][None, :], conv_norm_g[l][None, :],
                         conv_norm_b[l][None, :], wo_b, norm2_g[l][None, :], w1_b, w2_b,
                         final_norm_g[None, :])
    return out.reshape(bsz, seq, D_MODEL)
```

```python
import functools

import jax
import jax.numpy as jnp
from jax import lax
from jax.experimental import pallas as pl
from jax.experimental.pallas import tpu as pltpu

D_MODEL = 1024
GLA_HEADS = 4
GLA_DV = 128
GLA_DK = 64
GLA_KEY = GLA_HEADS * GLA_DK
GLA_VAL = GLA_HEADS * GLA_DV
GATE_RANK = 16
GATE_TAU = 16.0
CONV_CH = 512
CONV_GROUPS = 8
CONV_W = 31
D_FF = 4 * D_MODEL
EPS = 1e-6

OFF_Q = 0
OFF_K = OFF_Q + GLA_KEY
OFF_V = OFF_K + GLA_KEY
OFF_G = OFF_V + GLA_VAL
OFF_A = OFF_G + GLA_VAL
OFF_C = OFF_A + GATE_RANK
D_IN = OFF_C + 2 * CONV_CH

LANES = 128
SUBLANES = 8
Z_PAD = LANES
P_Q = 0
P_K = P_Q + GLA_KEY
P_V = P_K + GLA_KEY
P_G = P_V + GLA_VAL
P_CI = P_G + GLA_VAL
P_CG = P_CI + CONV_CH
P_Z = P_CG + CONV_CH
D_INP = P_Z + Z_PAD

TM_PROJ = 1024
GLA_SUB = 256
GLA_NSUB = 8
TM_GLA = GLA_SUB * GLA_NSUB
TM_MLP = 512
FF_CHUNK = 512
OUT_CHUNK = 256
CONV_HALO = 32
CONV_RB = 32
CONV_SH_ROWS = TM_MLP + CONV_HALO - SUBLANES
GLA_FAST_MIN_B = -60.0

V7X_VMEM_BYTES = 64 * 1024 * 1024
VMEM_LIMIT = V7X_VMEM_BYTES - 8 * 1024 * 1024

F32 = jnp.float32
BF16 = jnp.bfloat16


def _sigmoid(x):
    return 1.0 / (1.0 + jnp.exp(-x))


def _split_bf16(x):
    hi = x.astype(BF16)
    lo = (x - hi.astype(F32)).astype(BF16)
    return hi, lo


def _dot(a, b):
    return jnp.dot(a, b, preferred_element_type=F32)


def _dot_nt(a, b):
    return lax.dot_general(a, b, (((1,), (1,)), ((), ())), preferred_element_type=F32)


def _dot_tn(a, b):
    return lax.dot_general(a, b, (((0,), (0,)), ((), ())), preferred_element_type=F32)


def _proj_kernel(x_ref, g1_ref, w_ref, wg_ref, bg_ref, wo_ref, w1_ref, w2_ref,
                 q_ref, k_ref, la_ref, v_ref, sg_ref, u_ref, wob_ref, w1b_ref, w2b_ref,
                 wb_ref, wgb_ref):
    wob_ref[...] = wo_ref[...].astype(BF16)
    w1b_ref[...] = w1_ref[...].astype(BF16)
    w2b_ref[...] = w2_ref[...].astype(BF16)

    @pl.when(pl.program_id(0) == 0)
    def _():
        wb_ref[P_Q:P_CI, :] = w_ref[OFF_Q:OFF_A, :].astype(BF16)
        wb_ref[P_CI:P_Z, :] = w_ref[OFF_C:D_IN, :].astype(BF16)
        wb_ref[P_Z:P_Z + GATE_RANK, :] = w_ref[OFF_A:OFF_C, :].astype(BF16)
        wb_ref[P_Z + GATE_RANK:D_INP, :] = jnp.zeros((Z_PAD - GATE_RANK, D_MODEL), BF16)
        wgb_ref[...] = jnp.zeros_like(wgb_ref)
        wgb_ref[0:GATE_RANK, :] = wg_ref[...].astype(BF16)

    x = x_ref[...]
    ms = jnp.mean(x * x, axis=-1, keepdims=True)
    xn = (x * lax.rsqrt(ms + EPS) * g1_ref[...]).astype(BF16)
    proj = _dot_nt(xn, wb_ref[...])
    q_ref[...] = (proj[:, P_Q:P_K] * (GLA_DK ** -0.5)).astype(BF16)
    k_ref[...] = proj[:, P_K:P_V].astype(BF16)
    v_ref[...] = proj[:, P_V:P_G].astype(BF16)
    g = proj[:, P_G:P_CI]
    sg_ref[...] = (g * _sigmoid(g)).astype(BF16)
    u_ref[...] = proj[:, P_CI:P_CG] * _sigmoid(proj[:, P_CG:P_Z])
    z = proj[:, P_Z:D_INP].astype(BF16)
    a_logit = _dot(z, wgb_ref[...]) + bg_ref[...]
    la = jnp.minimum(a_logit, 0.0) - jnp.log(1.0 + jnp.exp(-jnp.abs(a_logit)))
    la_ref[...] = la * (1.0 / GATE_TAU)


def _proj_call(x2, g1, w_in_t, w_gate_up, bg, wo, w1, w2):
    s = x2.shape[0]
    n = s // TM_PROJ
    row = lambda i: (i, 0)
    col = lambda i: (0, i)
    const = lambda i: (0, 0)
    return pl.pallas_call(
        _proj_kernel,
        grid=(n,),
        in_specs=[
            pl.BlockSpec((TM_PROJ, D_MODEL), row),
            pl.BlockSpec((1, D_MODEL), const),
            pl.BlockSpec((None, D_IN, D_MODEL), lambda i: (0, 0, 0), pipeline_mode=pl.Buffered(1)),
            pl.BlockSpec((GATE_RANK, GLA_KEY), const),
            pl.BlockSpec((1, GLA_KEY), const),
            pl.BlockSpec((D_MODEL // n, D_MODEL), row),
            pl.BlockSpec((D_MODEL, D_FF // n), col),
            pl.BlockSpec((D_FF // n, D_MODEL), row),
        ],
        out_specs=[
            pl.BlockSpec((TM_PROJ, GLA_KEY), row),
            pl.BlockSpec((TM_PROJ, GLA_KEY), row),
            pl.BlockSpec((TM_PROJ, GLA_KEY), row),
            pl.BlockSpec((TM_PROJ, GLA_VAL), row),
            pl.BlockSpec((TM_PROJ, GLA_VAL), row),
            pl.BlockSpec((TM_PROJ, CONV_CH), row),
            pl.BlockSpec((D_MODEL // n, D_MODEL), row),
            pl.BlockSpec((D_MODEL, D_FF // n), col),
            pl.BlockSpec((D_FF // n, D_MODEL), row),
        ],
        out_shape=[
            jax.ShapeDtypeStruct((s, GLA_KEY), BF16),
            jax.ShapeDtypeStruct((s, GLA_KEY), BF16),
            jax.ShapeDtypeStruct((s, GLA_KEY), F32),
            jax.ShapeDtypeStruct((s, GLA_VAL), BF16),
            jax.ShapeDtypeStruct((s, GLA_VAL), BF16),
            jax.ShapeDtypeStruct((s, CONV_CH), F32),
            jax.ShapeDtypeStruct((D_MODEL, D_MODEL), BF16),
            jax.ShapeDtypeStruct((D_MODEL, D_FF), BF16),
            jax.ShapeDtypeStruct((D_FF, D_MODEL), BF16),
        ],
        scratch_shapes=[
            pltpu.VMEM((D_INP, D_MODEL), BF16),
            pltpu.VMEM((Z_PAD, GLA_KEY), BF16),
        ],
        compiler_params=pltpu.CompilerParams(
            dimension_semantics=("arbitrary",), vmem_limit_bytes=VMEM_LIMIT),
        name="proj",
    )(x2, g1, w_in_t, w_gate_up, bg, wo, w1, w2)


def _gla_kernel(q_ref, k_ref, la_ref, lan_ref, la0_ref, v_ref, sg_ref, gn_ref,
                o_ref, st_ref, b_ref, bmin_ref):
    c = GLA_SUB

    ri = lax.broadcasted_iota(jnp.int32, (c, c), 0)
    ci = lax.broadcasted_iota(jnp.int32, (c, c), 1)
    first_head = lax.broadcasted_iota(jnp.int32, (c, LANES), 1) < GLA_DK
    st_row = lax.broadcasted_iota(jnp.int32, (2 * GLA_DV, LANES), 0)
    st_lane = lax.broadcasted_iota(jnp.int32, (2 * GLA_DV, LANES), 1)
    same_head = (st_row < GLA_DV) == (st_lane < GLA_DK)

    def masked_sum(m, rows):
        mb = m.astype(BF16)
        la_hi, la_lo = _split_bf16(la_ref[rows, :])
        return _dot(mb, la_hi) + _dot(mb, la_lo)

    def cumulate(src_ref):
        tril = (ci <= ri).astype(BF16)
        b_min = None
        for s in range(GLA_NSUB):
            rows = slice(s * c, (s + 1) * c)
            la_hi, la_lo = _split_bf16(src_ref[rows, :])
            b = _dot(tril, la_hi) + _dot(tril, la_lo)
            b_ref[rows, :] = b
            b_min = b[c - 1:c, :] if b_min is None else jnp.minimum(b_min, b[c - 1:c, :])
        bmin_ref[0] = jnp.min(b_min)

    @pl.when(pl.program_id(0) == 0)
    def _():
        st_ref[...] = jnp.zeros_like(st_ref)
        cumulate(la0_ref)

    fast = bmin_ref[0] > GLA_FAST_MIN_B

    def chunk_pre(rows, scores_fn):
        b = b_ref[rows, :]
        b_last = b[c - 1:c, :]
        q = q_ref[rows, :].astype(F32)
        k = k_ref[rows, :].astype(F32)
        q_in = (q * jnp.exp(b)).astype(BF16)
        k_out = (k * jnp.exp(b_last - b)).astype(BF16)
        intra, upds = [], []
        for p in range(GLA_HEADS // 2):
            kc = slice(p * LANES, (p + 1) * LANES)
            v_p = v_ref[rows, 2 * p * GLA_DV:2 * (p + 1) * GLA_DV]
            for j, s_h in enumerate(scores_fn(p, q, k, b, q_in)):
                intra.append(_dot(s_h.astype(BF16), v_p[:, j * GLA_DV:(j + 1) * GLA_DV]))
            upd = _dot_tn(v_p, k_out[:, kc])
            upds.append(jnp.where(same_head, upd, 0.0))
        return rows, q_in, jnp.exp(b_last), intra, upds

    def chunk_post(pre, states):
        rows, q_in, a_chunk, intra, upds = pre
        new_states = []
        for p in range(GLA_HEADS // 2):
            kc = slice(p * LANES, (p + 1) * LANES)
            st = states[p]
            o_inter = _dot_nt(q_in[:, kc], st.astype(BF16))
            for j in range(2):
                h = 2 * p + j
                o_h = intra[h] + o_inter[:, j * GLA_DV:(j + 1) * GLA_DV]
                ms = jnp.mean(o_h * o_h, axis=-1, keepdims=True)
                y = o_h * lax.rsqrt(ms + EPS) * gn_ref[...]
                cols = slice(h * GLA_DV, (h + 1) * GLA_DV)
                o_ref[rows, cols] = (y * sg_ref[rows, cols].astype(F32)).astype(BF16)
            new_states.append(a_chunk[:, kc] * st + upds[p])
        return new_states

    def fast_scores(p, q, k, b, q_in):
        kc = slice(p * LANES, (p + 1) * LANES)
        k_div = k[:, kc] * jnp.exp(-b[:, kc])
        k_blk = jnp.concatenate([jnp.where(first_head, k_div, 0.0),
                                 jnp.where(first_head, 0.0, k_div)], axis=0).astype(BF16)
        s2 = _dot_nt(q_in[:, kc], k_blk)
        return [jnp.where(ri >= ci, s2[:, j * c:(j + 1) * c], 0.0) for j in range(2)]

    def safe_scores(rows):
        q = q_ref[rows, :].astype(F32)
        k = k_ref[rows, :].astype(F32)
        qk = q * k
        scores = []
        for h in range(GLA_HEADS):
            dk = slice(h * GLA_DK, (h + 1) * GLA_DK)
            scores.append(jnp.where(ri == ci, jnp.sum(qk[:, dk], axis=-1, keepdims=True), 0.0))
        half = c // 2
        while half >= 1:
            ref = (ri & (-2 * half)) + (half - 1)
            lo = jnp.minimum(ri, ref)
            hi = jnp.maximum(ri, ref)
            e = jnp.exp(masked_sum((ci > lo) & (ci <= hi), rows))
            q_l = (q * e).astype(BF16)
            k_l = (k * e).astype(BF16)
            pair = (((ri ^ ci) & (-2 * half)) == 0) & ((ri & half) != 0) & ((ci & half) == 0)
            for h in range(GLA_HEADS):
                dk = slice(h * GLA_DK, (h + 1) * GLA_DK)
                scores[h] = jnp.where(pair, _dot_nt(q_l[:, dk], k_l[:, dk]), scores[h])
            half //= 2
        return scores

    @pl.when(fast)
    def _():
        pres = [chunk_pre(slice(s * c, (s + 1) * c), fast_scores) for s in range(GLA_NSUB)]
        cumulate(lan_ref)
        states = [st_ref[p] for p in range(GLA_HEADS // 2)]
        for pre in pres:
            states = chunk_post(pre, states)
        for p in range(GLA_HEADS // 2):
            st_ref[p] = states[p]

    @pl.when(jnp.logical_not(fast))
    def _():
        def body(s, carry):
            rows = pl.ds(pl.multiple_of(s * c, c), c)
            scores = safe_scores(rows)
            pre = chunk_pre(rows, lambda p, *_: scores[2 * p:2 * p + 2])
            states = chunk_post(pre, [st_ref[p] for p in range(GLA_HEADS // 2)])
            for p in range(GLA_HEADS // 2):
                st_ref[p] = states[p]
            return carry

        lax.fori_loop(0, GLA_NSUB, body, 0)
        cumulate(lan_ref)


def _gla_call(q, k, la, v, sg, gn):
    s = q.shape[0]
    n = s // TM_GLA
    row = lambda i: (i, 0)
    nxt = lambda i: (jnp.minimum(i + 1, n - 1), 0)
    const = lambda i: (0, 0)
    return pl.pallas_call(
        _gla_kernel,
        grid=(n,),
        in_specs=[
            pl.BlockSpec((TM_GLA, GLA_KEY), row),
            pl.BlockSpec((TM_GLA, GLA_KEY), row),
            pl.BlockSpec((TM_GLA, GLA_KEY), row),
            pl.BlockSpec((TM_GLA, GLA_KEY), nxt),
            pl.BlockSpec((TM_GLA, GLA_KEY), const, pipeline_mode=pl.Buffered(1)),
            pl.BlockSpec((TM_GLA, GLA_VAL), row),
            pl.BlockSpec((TM_GLA, GLA_VAL), row),
            pl.BlockSpec((1, GLA_DV), const),
        ],
        out_specs=pl.BlockSpec((TM_GLA, GLA_VAL), row),
        out_shape=jax.ShapeDtypeStruct((s, GLA_VAL), BF16),
        scratch_shapes=[
            pltpu.VMEM((GLA_HEADS // 2, 2 * GLA_DV, LANES), F32),
            pltpu.VMEM((TM_GLA, GLA_KEY), F32),
            pltpu.SMEM((1,), F32),
        ],
        compiler_params=pltpu.CompilerParams(
            dimension_semantics=("arbitrary",), vmem_limit_bytes=VMEM_LIMIT),
        name="gla",
    )(q, k, la, la, la, v, sg, gn)


def _group_mean(t, low_half, size):
    lo = jnp.sum(jnp.where(low_half, t, 0.0), axis=-1, keepdims=True)
    hi = jnp.sum(jnp.where(low_half, 0.0, t), axis=-1, keepdims=True)
    return jnp.where(low_half, lo, hi) * (1.0 / size)


def _interleave(mxu_items, vec_items):
    done = 0
    for i, item in enumerate(mxu_items):
        item()
        upto = (i + 1) * len(vec_items) // len(mxu_items)
        for v_item in vec_items[done:upto]:
            v_item()
        done = upto


def _mlp_conv_kernel(x_ref, og_ref, un_ref, u0_ref, cw_ref, cb_ref, gg_ref, gb_ref,
                     wo_hbm, g2_ref, w1_hbm, w2_hbm, gf_ref,
                     o_ref, ext_ref, sh_ref, oc_ref, f_ref, wo_ref, w1_ref, w2_ref, w_sem):
    tm = TM_MLP

    def weight_copies():
        return [pltpu.make_async_copy(src, dst, w_sem.at[j])
                for j, (src, dst) in enumerate(((wo_hbm, wo_ref), (w1_hbm, w1_ref), (w2_hbm, w2_ref)))]

    grp = CONV_CH // CONV_GROUPS
    low_half = lax.broadcasted_iota(jnp.int32, (CONV_RB, LANES), 1) < grp
    first = CONV_HALO - (CONV_W - 1)

    def shifted_copy(p):
        sh_ref[p - 1] = ext_ref[p:p + CONV_SH_ROWS, :]

    def conv_block(base):
        for cg in range(CONV_CH // LANES):
            cols = slice(cg * LANES, (cg + 1) * LANES)
            acc = jnp.broadcast_to(cb_ref[:, cols], (CONV_RB, LANES))
            for t in range(CONV_W):
                shift = first + t
                p = shift % SUBLANES
                rows = pl.ds(base + (shift - p), CONV_RB)
                tap = ext_ref[rows, cols] if p == 0 else sh_ref[p - 1, rows, cols]
                acc = acc + cw_ref[t:t + 1, cols] * tap
            d = acc - _group_mean(acc, low_half, grp)
            var = _group_mean(d * d, low_half, grp)
            yn = d * lax.rsqrt(var + EPS) * gg_ref[:, cols] + gb_ref[:, cols]
            oc_ref[pl.ds(base, CONV_RB), cols] = (yn * _sigmoid(yn)).astype(BF16)

    @pl.when(pl.program_id(0) == 0)
    def _():
        for cp in weight_copies():
            cp.start()
        ext_ref[0:CONV_HALO, :] = jnp.zeros((CONV_HALO, CONV_CH), F32)
        ext_ref[CONV_HALO:, :] = u0_ref[...]
        for p in range(1, SUBLANES):
            shifted_copy(p)

        def body(r, carry):
            conv_block(pl.multiple_of(r * CONV_RB, CONV_RB))
            return carry

        lax.fori_loop(0, tm // CONV_RB, body, 0)
        for cp in weight_copies():
            cp.wait()

    h = x_ref[...] + _dot(og_ref[...], wo_ref[0:GLA_VAL, :]) + _dot(oc_ref[...], wo_ref[GLA_VAL:, :])
    hg = (h * g2_ref[...]).astype(BF16)
    r = lax.rsqrt(jnp.mean(h * h, axis=-1, keepdims=True) + EPS)
    r2 = r * r

    def roll_window():
        ext_ref[0:CONV_HALO, :] = ext_ref[tm:tm + CONV_HALO, :]
        ext_ref[CONV_HALO:, :] = un_ref[...]

    vec_items = [roll_window] + [functools.partial(shifted_copy, p) for p in range(1, SUBLANES)]
    vec_items += [functools.partial(conv_block, r * CONV_RB) for r in range(tm // CONV_RB)]

    def ff_in(j):
        cols = slice(j * FF_CHUNK, (j + 1) * FF_CHUNK)
        f = jnp.maximum(_dot(hg, w1_ref[:, cols]), 0.0)
        f_ref[:, cols] = (f * f * r2).astype(BF16)

    def ff_out(j):
        cols = slice(j * OUT_CHUNK, (j + 1) * OUT_CHUNK)
        o_ref[:, cols] = h[:, cols] + _dot(f_ref[...], w2_ref[:, cols])

    mxu_items = [functools.partial(ff_in, j) for j in range(D_FF // FF_CHUNK)]
    mxu_items += [functools.partial(ff_out, j) for j in range(D_MODEL // OUT_CHUNK)]
    _interleave(mxu_items, vec_items)

    acc = o_ref[...]
    ms2 = jnp.mean(acc * acc, axis=-1, keepdims=True)
    o_ref[...] = acc * lax.rsqrt(ms2 + EPS) * gf_ref[...]


def _mlp_conv_call(x2, og, u, cw, cb, gg, gb, wo, g2, w1, w2, gf):
    s = x2.shape[0]
    n = s // TM_MLP
    row = lambda i: (i, 0)
    nxt = lambda i: (jnp.minimum(i + 1, n - 1), 0)
    const = lambda i: (0, 0)
    single = pl.Buffered(1)
    return pl.pallas_call(
        _mlp_conv_kernel,
        grid=(n,),
        in_specs=[
            pl.BlockSpec((TM_MLP, D_MODEL), row),
            pl.BlockSpec((TM_MLP, GLA_VAL), row),
            pl.BlockSpec((TM_MLP, CONV_CH), nxt),
            pl.BlockSpec((TM_MLP, CONV_CH), const, pipeline_mode=single),
            pl.BlockSpec((CONV_W, CONV_CH), const),
            pl.BlockSpec((1, CONV_CH), const),
            pl.BlockSpec((1, CONV_CH), const),
            pl.BlockSpec((1, CONV_CH), const),
            pl.BlockSpec(memory_space=pl.ANY),
            pl.BlockSpec((1, D_MODEL), const),
            pl.BlockSpec(memory_space=pl.ANY),
            pl.BlockSpec(memory_space=pl.ANY),
            pl.BlockSpec((1, D_MODEL), const),
        ],
        out_specs=pl.BlockSpec((TM_MLP, D_MODEL), row),
        out_shape=jax.ShapeDtypeStruct((s, D_MODEL), F32),
        scratch_shapes=[
            pltpu.VMEM((TM_MLP + CONV_HALO, CONV_CH), F32),
            pltpu.VMEM((SUBLANES - 1, CONV_SH_ROWS, CONV_CH), F32),
            pltpu.VMEM((TM_MLP, CONV_CH), BF16),
            pltpu.VMEM((TM_MLP, D_FF), BF16),
            pltpu.VMEM((D_MODEL, D_MODEL), BF16),
            pltpu.VMEM((D_MODEL, D_FF), BF16),
            pltpu.VMEM((D_FF, D_MODEL), BF16),
            pltpu.SemaphoreType.DMA((3,)),
        ],
        compiler_params=pltpu.CompilerParams(
            dimension_semantics=("arbitrary",), vmem_limit_bytes=VMEM_LIMIT),
        name="mlp_conv",
    )(x2, og, u, u, cw, cb, gg, gb, wo, g2, w1, w2, gf)


def kernel(x, norm1_g, w_in, w_gate_up, b_gate, gla_norm_g, conv_w, conv_b, conv_norm_g,
           conv_norm_b, w_out, norm2_g, w_mlp_in, w_mlp_out, final_norm_g):
    bsz, seq, _ = x.shape
    x2 = x.reshape(bsz * seq, D_MODEL)
    assert bsz == 1, "state / halo carry across grid steps assumes one sequence"
    l = 0
    assert w_in.shape[0] == 1, "single layer: the projection weight is passed with its layer axis"
    w_in_t = jnp.swapaxes(w_in, 1, 2)
    q, k, la, v, sg, u, wo_b, w1_b, w2_b = _proj_call(
        x2, norm1_g[l][None, :], w_in_t, w_gate_up[l], b_gate[l][None, :], w_out[l], w_mlp_in[l],
        w_mlp_out[l])
    o_gla = _gla_call(q, k, la, v, sg, gla_norm_g[l][None, :])
    out = _mlp_conv_call(x2, o_gla, u, conv_w[l], conv_b[l][None, :], conv_norm_g[l][None, :],
                         conv_norm_b[l][None, :], wo_b, norm2_g[l][None, :], w1_b, w2_b,
                         final_norm_g[None, :])
    return out.reshape(bsz, seq, D_MODEL)
```

```python
import functools

import jax
import jax.numpy as jnp
from jax import lax
from jax.experimental import pallas as pl
from jax.experimental.pallas import tpu as pltpu

D_MODEL = 1024
GLA_HEADS = 4
GLA_DV = 128
GLA_DK = 64
GLA_KEY = GLA_HEADS * GLA_DK
GLA_VAL = GLA_HEADS * GLA_DV
GATE_RANK = 16
GATE_TAU = 16.0
CONV_CH = 512
CONV_GROUPS = 8
CONV_W = 31
D_FF = 4 * D_MODEL
EPS = 1e-6

OFF_Q = 0
OFF_K = OFF_Q + GLA_KEY
OFF_V = OFF_K + GLA_KEY
OFF_G = OFF_V + GLA_VAL
OFF_A = OFF_G + GLA_VAL
OFF_C = OFF_A + GATE_RANK
D_IN = OFF_C + 2 * CONV_CH

LANES = 128
SUBLANES = 8
Z_PAD = LANES
P_Q = 0
P_K = P_Q + GLA_KEY
P_V = P_K + GLA_KEY
P_G = P_V + GLA_VAL
P_CI = P_G + GLA_VAL
P_CG = P_CI + CONV_CH
P_Z = P_CG + CONV_CH
D_INP = P_Z + Z_PAD

TM_PROJ = 1024
GLA_SUB = 256
GLA_NSUB = 8
TM_GLA = GLA_SUB * GLA_NSUB
TM_MLP = 512
FF_CHUNK = 512
OUT_CHUNK = 256
CONV_HALO = 32
CONV_RB = 32
CONV_SH_ROWS = TM_MLP + CONV_HALO - SUBLANES
GLA_FAST_MIN_B = -60.0

V7X_VMEM_BYTES = 64 * 1024 * 1024
VMEM_LIMIT = V7X_VMEM_BYTES - 8 * 1024 * 1024

F32 = jnp.float32
BF16 = jnp.bfloat16


def _sigmoid(x):
    return 1.0 / (1.0 + jnp.exp(-x))


def _split_bf16(x):
    hi = x.astype(BF16)
    lo = (x - hi.astype(F32)).astype(BF16)
    return hi, lo


def _dot(a, b):
    return jnp.dot(a, b, preferred_element_type=F32)


def _dot_nt(a, b):
    return lax.dot_general(a, b, (((1,), (1,)), ((), ())), preferred_element_type=F32)


def _dot_tn(a, b):
    return lax.dot_general(a, b, (((0,), (0,)), ((), ())), preferred_element_type=F32)


def _proj_kernel(x_ref, g1_ref, w_ref, wg_ref, bg_ref, wo_ref, w1_ref, w2_ref,
                 q_ref, k_ref, la_ref, v_ref, sg_ref, u_ref, wob_ref, w1b_ref, w2b_ref,
                 wb_ref, wgb_ref):
    wob_ref[...] = wo_ref[...].astype(BF16)
    w1b_ref[...] = w1_ref[...].astype(BF16)
    w2b_ref[...] = w2_ref[...].astype(BF16)

    @pl.when(pl.program_id(0) == 0)
    def _():
        wb_ref[P_Q:P_CI, :] = w_ref[OFF_Q:OFF_A, :].astype(BF16)
        wb_ref[P_CI:P_Z, :] = w_ref[OFF_C:D_IN, :].astype(BF16)
        wb_ref[P_Z:P_Z + GATE_RANK, :] = w_ref[OFF_A:OFF_C, :].astype(BF16)
        wb_ref[P_Z + GATE_RANK:D_INP, :] = jnp.zeros((Z_PAD - GATE_RANK, D_MODEL), BF16)
        wgb_ref[...] = jnp.zeros_like(wgb_ref)
        wgb_ref[0:GATE_RANK, :] = wg_ref[...].astype(BF16)

    x = x_ref[...]
    ms = jnp.mean(x * x, axis=-1, keepdims=True)
    xn = (x * lax.rsqrt(ms + EPS) * g1_ref[...]).astype(BF16)
    proj = _dot_nt(xn, wb_ref[...])
    q_ref[...] = (proj[:, P_Q:P_K] * (GLA_DK ** -0.5)).astype(BF16)
    k_ref[...] = proj[:, P_K:P_V].astype(BF16)
    v_ref[...] = proj[:, P_V:P_G].astype(BF16)
    g = proj[:, P_G:P_CI]
    sg_ref[...] = (g * _sigmoid(g)).astype(BF16)
    u_ref[...] = proj[:, P_CI:P_CG] * _sigmoid(proj[:, P_CG:P_Z])
    z = proj[:, P_Z:D_INP].astype(BF16)
    a_logit = _dot(z, wgb_ref[...]) + bg_ref[...]
    la = jnp.minimum(a_logit, 0.0) - jnp.log(1.0 + jnp.exp(-jnp.abs(a_logit)))
    la_ref[...] = la * (1.0 / GATE_TAU)


def _proj_call(x2, g1, w_in_t, w_gate_up, bg, wo, w1, w2):
    s = x2.shape[0]
    n = s // TM_PROJ
    row = lambda i: (i, 0)
    col = lambda i: (0, i)
    const = lambda i: (0, 0)
    return pl.pallas_call(
        _proj_kernel,
        grid=(n,),
        in_specs=[
            pl.BlockSpec((TM_PROJ, D_MODEL), row),
            pl.BlockSpec((1, D_MODEL), const),
            pl.BlockSpec((None, D_IN, D_MODEL), lambda i: (0, 0, 0), pipeline_mode=pl.Buffered(1)),
            pl.BlockSpec((GATE_RANK, GLA_KEY), const),
            pl.BlockSpec((1, GLA_KEY), const),
            pl.BlockSpec((D_MODEL // n, D_MODEL), row),
            pl.BlockSpec((D_MODEL, D_FF // n), col),
            pl.BlockSpec((D_FF // n, D_MODEL), row),
        ],
        out_specs=[
            pl.BlockSpec((TM_PROJ, GLA_KEY), row),
            pl.BlockSpec((TM_PROJ, GLA_KEY), row),
            pl.BlockSpec((TM_PROJ, GLA_KEY), row),
            pl.BlockSpec((TM_PROJ, GLA_VAL), row),
            pl.BlockSpec((TM_PROJ, GLA_VAL), row),
            pl.BlockSpec((TM_PROJ, CONV_CH), row),
            pl.BlockSpec((D_MODEL // n, D_MODEL), row),
            pl.BlockSpec((D_MODEL, D_FF // n), col),
            pl.BlockSpec((D_FF // n, D_MODEL), row),
        ],
        out_shape=[
            jax.ShapeDtypeStruct((s, GLA_KEY), BF16),
            jax.ShapeDtypeStruct((s, GLA_KEY), BF16),
            jax.ShapeDtypeStruct((s, GLA_KEY), F32),
            jax.ShapeDtypeStruct((s, GLA_VAL), BF16),
            jax.ShapeDtypeStruct((s, GLA_VAL), BF16),
            jax.ShapeDtypeStruct((s, CONV_CH), F32),
            jax.ShapeDtypeStruct((D_MODEL, D_MODEL), BF16),
            jax.ShapeDtypeStruct((D_MODEL, D_FF), BF16),
            jax.ShapeDtypeStruct((D_FF, D_MODEL), BF16),
        ],
        scratch_shapes=[
            pltpu.VMEM((D_INP, D_MODEL), BF16),
            pltpu.VMEM((Z_PAD, GLA_KEY), BF16),
        ],
        compiler_params=pltpu.CompilerParams(
            dimension_semantics=("arbitrary",), vmem_limit_bytes=VMEM_LIMIT),
        name="proj",
    )(x2, g1, w_in_t, w_gate_up, bg, wo, w1, w2)


def _gla_kernel(q_ref, k_ref, la_ref, v_ref, sg_ref, gn_ref, o_ref, st_ref, b_ref):
    c = GLA_SUB

    @pl.when(pl.program_id(0) == 0)
    def _():
        st_ref[...] = jnp.zeros_like(st_ref)

    ri = lax.broadcasted_iota(jnp.int32, (c, c), 0)
    ci = lax.broadcasted_iota(jnp.int32, (c, c), 1)
    first_head = lax.broadcasted_iota(jnp.int32, (c, LANES), 1) < GLA_DK
    st_row = lax.broadcasted_iota(jnp.int32, (2 * GLA_DV, LANES), 0)
    st_lane = lax.broadcasted_iota(jnp.int32, (2 * GLA_DV, LANES), 1)
    same_head = (st_row < GLA_DV) == (st_lane < GLA_DK)

    def masked_sum(m, rows):
        mb = m.astype(BF16)
        la_hi, la_lo = _split_bf16(la_ref[rows, :])
        return _dot(mb, la_hi) + _dot(mb, la_lo)

    b_min = None
    for s in range(GLA_NSUB):
        rows = slice(s * c, (s + 1) * c)
        b = masked_sum(ci <= ri, rows)
        b_ref[rows, :] = b
        b_min = b[c - 1:c, :] if b_min is None else jnp.minimum(b_min, b[c - 1:c, :])
    fast = jnp.min(b_min) > GLA_FAST_MIN_B

    def chunk_pre(rows, scores_fn):
        b = b_ref[rows, :]
        b_last = b[c - 1:c, :]
        q = q_ref[rows, :].astype(F32)
        k = k_ref[rows, :].astype(F32)
        q_in = (q * jnp.exp(b)).astype(BF16)
        k_out = (k * jnp.exp(b_last - b)).astype(BF16)
        intra, upds = [], []
        for p in range(GLA_HEADS // 2):
            kc = slice(p * LANES, (p + 1) * LANES)
            v_p = v_ref[rows, 2 * p * GLA_DV:2 * (p + 1) * GLA_DV]
            for j, s_h in enumerate(scores_fn(p, q, k, b, q_in)):
                intra.append(_dot(s_h.astype(BF16), v_p[:, j * GLA_DV:(j + 1) * GLA_DV]))
            upd = _dot_tn(v_p, k_out[:, kc])
            upds.append(jnp.where(same_head, upd, 0.0))
        return rows, q_in, jnp.exp(b_last), intra, upds

    def chunk_post(pre, states):
        rows, q_in, a_chunk, intra, upds = pre
        new_states = []
        for p in range(GLA_HEADS // 2):
            kc = slice(p * LANES, (p + 1) * LANES)
            st = states[p]
            o_inter = _dot_nt(q_in[:, kc], st.astype(BF16))
            for j in range(2):
                h = 2 * p + j
                o_h = intra[h] + o_inter[:, j * GLA_DV:(j + 1) * GLA_DV]
                ms = jnp.mean(o_h * o_h, axis=-1, keepdims=True)
                y = o_h * lax.rsqrt(ms + EPS) * gn_ref[...]
                cols = slice(h * GLA_DV, (h + 1) * GLA_DV)
                o_ref[rows, cols] = (y * sg_ref[rows, cols].astype(F32)).astype(BF16)
            new_states.append(a_chunk[:, kc] * st + upds[p])
        return new_states

    def fast_scores(p, q, k, b, q_in):
        kc = slice(p * LANES, (p + 1) * LANES)
        k_div = k[:, kc] * jnp.exp(-b[:, kc])
        k_blk = jnp.concatenate([jnp.where(first_head, k_div, 0.0),
                                 jnp.where(first_head, 0.0, k_div)], axis=0).astype(BF16)
        s2 = _dot_nt(q_in[:, kc], k_blk)
        return [jnp.where(ri >= ci, s2[:, j * c:(j + 1) * c], 0.0) for j in range(2)]

    def safe_scores(rows):
        q = q_ref[rows, :].astype(F32)
        k = k_ref[rows, :].astype(F32)
        qk = q * k
        scores = []
        for h in range(GLA_HEADS):
            dk = slice(h * GLA_DK, (h + 1) * GLA_DK)
            scores.append(jnp.where(ri == ci, jnp.sum(qk[:, dk], axis=-1, keepdims=True), 0.0))
        half = c // 2
        while half >= 1:
            ref = (ri & (-2 * half)) + (half - 1)
            lo = jnp.minimum(ri, ref)
            hi = jnp.maximum(ri, ref)
            e = jnp.exp(masked_sum((ci > lo) & (ci <= hi), rows))
            q_l = (q * e).astype(BF16)
            k_l = (k * e).astype(BF16)
            pair = (((ri ^ ci) & (-2 * half)) == 0) & ((ri & half) != 0) & ((ci & half) == 0)
            for h in range(GLA_HEADS):
                dk = slice(h * GLA_DK, (h + 1) * GLA_DK)
                scores[h] = jnp.where(pair, _dot_nt(q_l[:, dk], k_l[:, dk]), scores[h])
            half //= 2
        return scores

    @pl.when(fast)
    def _():
        pres = [chunk_pre(slice(s * c, (s + 1) * c), fast_scores) for s in range(GLA_NSUB)]
        states = [st_ref[p] for p in range(GLA_HEADS // 2)]
        for pre in pres:
            states = chunk_post(pre, states)
        for p in range(GLA_HEADS // 2):
            st_ref[p] = states[p]

    @pl.when(jnp.logical_not(fast))
    def _():
        def body(s, carry):
            rows = pl.ds(pl.multiple_of(s * c, c), c)
            scores = safe_scores(rows)
            pre = chunk_pre(rows, lambda p, *_: scores[2 * p:2 * p + 2])
            states = chunk_post(pre, [st_ref[p] for p in range(GLA_HEADS // 2)])
            for p in range(GLA_HEADS // 2):
                st_ref[p] = states[p]
            return carry

        lax.fori_loop(0, GLA_NSUB, body, 0)


def _gla_call(q, k, la, v, sg, gn):
    s = q.shape[0]
    n = s // TM_GLA
    row = lambda i: (i, 0)
    const = lambda i: (0, 0)
    return pl.pallas_call(
        _gla_kernel,
        grid=(n,),
        in_specs=[
            pl.BlockSpec((TM_GLA, GLA_KEY), row),
            pl.BlockSpec((TM_GLA, GLA_KEY), row),
            pl.BlockSpec((TM_GLA, GLA_KEY), row),
            pl.BlockSpec((TM_GLA, GLA_VAL), row),
            pl.BlockSpec((TM_GLA, GLA_VAL), row),
            pl.BlockSpec((1, GLA_DV), const),
        ],
        out_specs=pl.BlockSpec((TM_GLA, GLA_VAL), row),
        out_shape=jax.ShapeDtypeStruct((s, GLA_VAL), BF16),
        scratch_shapes=[
            pltpu.VMEM((GLA_HEADS // 2, 2 * GLA_DV, LANES), F32),
            pltpu.VMEM((TM_GLA, GLA_KEY), F32),
        ],
        compiler_params=pltpu.CompilerParams(
            dimension_semantics=("arbitrary",), vmem_limit_bytes=VMEM_LIMIT),
        name="gla",
    )(q, k, la, v, sg, gn)


def _group_mean(t, low_half, size):
    lo = jnp.sum(jnp.where(low_half, t, 0.0), axis=-1, keepdims=True)
    hi = jnp.sum(jnp.where(low_half, 0.0, t), axis=-1, keepdims=True)
    return jnp.where(low_half, lo, hi) * (1.0 / size)


def _interleave(mxu_items, vec_items):
    done = 0
    for i, item in enumerate(mxu_items):
        item()
        upto = (i + 1) * len(vec_items) // len(mxu_items)
        for v_item in vec_items[done:upto]:
            v_item()
        done = upto


def _mlp_conv_kernel(x_ref, og_ref, un_ref, u0_ref, cw_ref, cb_ref, gg_ref, gb_ref,
                     wo_hbm, g2_ref, w1_hbm, w2_hbm, gf_ref,
                     o_ref, ext_ref, sh_ref, oc_ref, f_ref, wo_ref, w1_ref, w2_ref, w_sem):
    tm = TM_MLP

    def weight_copies():
        return [pltpu.make_async_copy(src, dst, w_sem.at[j])
                for j, (src, dst) in enumerate(((wo_hbm, wo_ref), (w1_hbm, w1_ref), (w2_hbm, w2_ref)))]

    grp = CONV_CH // CONV_GROUPS
    low_half = lax.broadcasted_iota(jnp.int32, (CONV_RB, LANES), 1) < grp
    first = CONV_HALO - (CONV_W - 1)

    def shifted_copy(p):
        sh_ref[p - 1] = ext_ref[p:p + CONV_SH_ROWS, :]

    def conv_block(base):
        for cg in range(CONV_CH // LANES):
            cols = slice(cg * LANES, (cg + 1) * LANES)
            acc = jnp.broadcast_to(cb_ref[:, cols], (CONV_RB, LANES))
            for t in range(CONV_W):
                shift = first + t
                p = shift % SUBLANES
                rows = pl.ds(base + (shift - p), CONV_RB)
                tap = ext_ref[rows, cols] if p == 0 else sh_ref[p - 1, rows, cols]
                acc = acc + cw_ref[t, :, cols] * tap
            d = acc - _group_mean(acc, low_half, grp)
            var = _group_mean(d * d, low_half, grp)
            yn = d * lax.rsqrt(var + EPS) * gg_ref[:, cols] + gb_ref[:, cols]
            oc_ref[pl.ds(base, CONV_RB), cols] = (yn * _sigmoid(yn)).astype(BF16)

    @pl.when(pl.program_id(0) == 0)
    def _():
        for cp in weight_copies():
            cp.start()
        ext_ref[0:CONV_HALO, :] = jnp.zeros((CONV_HALO, CONV_CH), F32)
        ext_ref[CONV_HALO:, :] = u0_ref[...]
        for p in range(1, SUBLANES):
            shifted_copy(p)

        def body(r, carry):
            conv_block(pl.multiple_of(r * CONV_RB, CONV_RB))
            return carry

        lax.fori_loop(0, tm // CONV_RB, body, 0)
        for cp in weight_copies():
            cp.wait()

    h = x_ref[...] + _dot(og_ref[...], wo_ref[0:GLA_VAL, :]) + _dot(oc_ref[...], wo_ref[GLA_VAL:, :])
    hg = (h * g2_ref[...]).astype(BF16)
    r = lax.rsqrt(jnp.mean(h * h, axis=-1, keepdims=True) + EPS)
    r2 = r * r

    def roll_window():
        ext_ref[0:CONV_HALO, :] = ext_ref[tm:tm + CONV_HALO, :]
        ext_ref[CONV_HALO:, :] = un_ref[...]

    vec_items = [roll_window] + [functools.partial(shifted_copy, p) for p in range(1, SUBLANES)]
    vec_items += [functools.partial(conv_block, r * CONV_RB) for r in range(tm // CONV_RB)]

    def ff_in(j):
        cols = slice(j * FF_CHUNK, (j + 1) * FF_CHUNK)
        f = jnp.maximum(_dot(hg, w1_ref[:, cols]), 0.0)
        f_ref[:, cols] = (f * f * r2).astype(BF16)

    def ff_out(j):
        cols = slice(j * OUT_CHUNK, (j + 1) * OUT_CHUNK)
        o_ref[:, cols] = h[:, cols] + _dot(f_ref[...], w2_ref[:, cols])

    mxu_items = [functools.partial(ff_in, j) for j in range(D_FF // FF_CHUNK)]
    mxu_items += [functools.partial(ff_out, j) for j in range(D_MODEL // OUT_CHUNK)]
    _interleave(mxu_items, vec_items)

    acc = o_ref[...]
    ms2 = jnp.mean(acc * acc, axis=-1, keepdims=True)
    o_ref[...] = acc * lax.rsqrt(ms2 + EPS) * gf_ref[...]


def _mlp_conv_call(x2, og, u, cw, cb, gg, gb, wo, g2, w1, w2, gf):
    s = x2.shape[0]
    n = s // TM_MLP
    row = lambda i: (i, 0)
    nxt = lambda i: (jnp.minimum(i + 1, n - 1), 0)
    const = lambda i: (0, 0)
    single = pl.Buffered(1)
    return pl.pallas_call(
        _mlp_conv_kernel,
        grid=(n,),
        in_specs=[
            pl.BlockSpec((TM_MLP, D_MODEL), row),
            pl.BlockSpec((TM_MLP, GLA_VAL), row),
            pl.BlockSpec((TM_MLP, CONV_CH), nxt),
            pl.BlockSpec((TM_MLP, CONV_CH), const, pipeline_mode=single),
            pl.BlockSpec((CONV_W, 1, CONV_CH), lambda i: (0, 0, 0)),
            pl.BlockSpec((1, CONV_CH), const),
            pl.BlockSpec((1, CONV_CH), const),
            pl.BlockSpec((1, CONV_CH), const),
            pl.BlockSpec(memory_space=pl.ANY),
            pl.BlockSpec((1, D_MODEL), const),
            pl.BlockSpec(memory_space=pl.ANY),
            pl.BlockSpec(memory_space=pl.ANY),
            pl.BlockSpec((1, D_MODEL), const),
        ],
        out_specs=pl.BlockSpec((TM_MLP, D_MODEL), row),
        out_shape=jax.ShapeDtypeStruct((s, D_MODEL), F32),
        scratch_shapes=[
            pltpu.VMEM((TM_MLP + CONV_HALO, CONV_CH), F32),
            pltpu.VMEM((SUBLANES - 1, CONV_SH_ROWS, CONV_CH), F32),
            pltpu.VMEM((TM_MLP, CONV_CH), BF16),
            pltpu.VMEM((TM_MLP, D_FF), BF16),
            pltpu.VMEM((D_MODEL, D_MODEL), BF16),
            pltpu.VMEM((D_MODEL, D_FF), BF16),
            pltpu.VMEM((D_FF, D_MODEL), BF16),
            pltpu.SemaphoreType.DMA((3,)),
        ],
        compiler_params=pltpu.CompilerParams(
            dimension_semantics=("arbitrary",), vmem_limit_bytes=VMEM_LIMIT),
        name="mlp_conv",
    )(x2, og, u, u, cw, cb, gg, gb, wo, g2, w1, w2, gf)


def kernel(x, norm1_g, w_in, w_gate_up, b_gate, gla_norm_g, conv_w, conv_b, conv_norm_g,
           conv_norm_b, w_out, norm2_g, w_mlp_in, w_mlp_out, final_norm_g):
    bsz, seq, _ = x.shape
    x2 = x.reshape(bsz * seq, D_MODEL)
    assert bsz == 1, "state / halo carry across grid steps assumes one sequence"
    l = 0
    assert w_in.shape[0] == 1, "single layer: the projection weight is passed with its layer axis"
    w_in_t = jnp.swapaxes(w_in, 1, 2)
    q, k, la, v, sg, u, wo_b, w1_b, w2_b = _proj_call(
        x2, norm1_g[l][None, :], w_in_t, w_gate_up[l], b_gate[l][None, :], w_out[l], w_mlp_in[l],
        w_mlp_out[l])
    o_gla = _gla_call(q, k, la, v, sg, gla_norm_g[l][None, :])
    out = _mlp_conv_call(x2, o_gla, u, jnp.swapaxes(conv_w, 0, 1), conv_b[l][None, :], conv_norm_g[l][None, :],
                         conv_norm_b[l][None, :], wo_b, norm2_g[l][None, :], w1_b, w2_b,
                         final_norm_g[None, :])
    return out.reshape(bsz, seq, D_MODEL)
```

```python
import functools

import jax
import jax.numpy as jnp
from jax import lax
from jax.experimental import pallas as pl
from jax.experimental.pallas import tpu as pltpu

D_MODEL = 1024
GLA_HEADS = 4
GLA_DV = 128
GLA_DK = 64
GLA_KEY = GLA_HEADS * GLA_DK
GLA_VAL = GLA_HEADS * GLA_DV
GATE_RANK = 16
GATE_TAU = 16.0
CONV_CH = 512
CONV_GROUPS = 8
CONV_W = 31
D_FF = 4 * D_MODEL
EPS = 1e-6

OFF_Q = 0
OFF_K = OFF_Q + GLA_KEY
OFF_V = OFF_K + GLA_KEY
OFF_G = OFF_V + GLA_VAL
OFF_A = OFF_G + GLA_VAL
OFF_C = OFF_A + GATE_RANK
D_IN = OFF_C + 2 * CONV_CH

LANES = 128
SUBLANES = 8
Z_PAD = LANES
P_Z = 0
P_VA = P_Z + Z_PAD
P_CG = P_VA + LANES
P_CI = P_CG + CONV_CH
P_SPLIT = P_CI + CONV_CH
P_Q = P_SPLIT
P_G = P_Q + GLA_KEY
P_K = P_G + GLA_VAL
P_VB = P_K + GLA_KEY
D_INP = P_VB + GLA_VAL - LANES

TM_PROJ = 1024
GLA_SUB = 256
GLA_NSUB = 8
TM_GLA = GLA_SUB * GLA_NSUB
TM_MLP = 512
FF_CHUNK = 512
OUT_CHUNK = 256
CONV_HALO = 32
CONV_RB = 32
CONV_PRO_BLOCKS = 4
CONV_SH_ROWS = TM_MLP + CONV_HALO - SUBLANES
GLA_FAST_MIN_B = -60.0

V7X_VMEM_BYTES = 64 * 1024 * 1024
VMEM_LIMIT = V7X_VMEM_BYTES - 8 * 1024 * 1024

F32 = jnp.float32
BF16 = jnp.bfloat16


def _sigmoid(x):
    return 1.0 / (1.0 + jnp.exp(-x))


def _split_bf16(x):
    hi = x.astype(BF16)
    lo = (x - hi.astype(F32)).astype(BF16)
    return hi, lo


def _dot(a, b):
    return jnp.dot(a, b, preferred_element_type=F32)


def _dot_nt(a, b):
    return lax.dot_general(a, b, (((1,), (1,)), ((), ())), preferred_element_type=F32)


def _dot_tn(a, b):
    return lax.dot_general(a, b, (((0,), (0,)), ((), ())), preferred_element_type=F32)


def _proj_kernel(x_ref, g1_ref, w_ref, wg_ref, bg_ref, wo_ref, w1_ref, w2_ref,
                 q_ref, k_ref, la_ref, v_ref, sg_ref, u_ref, wob_ref, w1b_ref, w2b_ref,
                 wb_ref, wgb_ref):
    @pl.when(pl.program_id(0) == 0)
    def _():
        def put(dst, src, size, scale=None):
            w = w_ref[src:src + size, :] * g1_ref[...]
            wb_ref[dst:dst + size, :] = (w if scale is None else w * scale).astype(BF16)

        put(P_Q, OFF_Q, GLA_KEY, GLA_DK ** -0.5)
        put(P_K, OFF_K, GLA_KEY)
        put(P_VA, OFF_V, LANES)
        put(P_VB, OFF_V + LANES, GLA_VAL - LANES)
        put(P_G, OFF_G, GLA_VAL)
        put(P_CI, OFF_C, CONV_CH)
        put(P_CG, OFF_C + CONV_CH, CONV_CH)
        put(P_Z, OFF_A, GATE_RANK)
        wb_ref[P_Z + GATE_RANK:P_Z + Z_PAD, :] = jnp.zeros((Z_PAD - GATE_RANK, D_MODEL), BF16)
        wgb_ref[...] = jnp.zeros_like(wgb_ref)
        wgb_ref[0:GATE_RANK, :] = wg_ref[...].astype(BF16)

    x = x_ref[...]
    r = lax.rsqrt(jnp.mean(x * x, axis=-1, keepdims=True) + EPS)
    xb = x.astype(BF16)
    head = _dot_nt(xb, wb_ref[0:P_SPLIT, :]) * r
    z = head[:, P_Z:P_Z + Z_PAD].astype(BF16)
    a_logit = _dot(z, wgb_ref[...]) + bg_ref[...]
    rest = _dot_nt(xb, wb_ref[P_SPLIT:D_INP, :]) * r
    u_ref[...] = head[:, P_CI:P_CI + CONV_CH] * _sigmoid(head[:, P_CG:P_CG + CONV_CH])
    v_ref[:, 0:LANES] = head[:, P_VA:P_VA + LANES].astype(BF16)
    g = rest[:, P_G - P_SPLIT:P_G - P_SPLIT + GLA_VAL]
    sg_ref[...] = (g * _sigmoid(g)).astype(BF16)
    q_ref[...] = rest[:, P_Q - P_SPLIT:P_Q - P_SPLIT + GLA_KEY].astype(BF16)
    k_ref[...] = rest[:, P_K - P_SPLIT:P_K - P_SPLIT + GLA_KEY].astype(BF16)
    v_ref[:, LANES:GLA_VAL] = rest[:, P_VB - P_SPLIT:D_INP - P_SPLIT].astype(BF16)
    wob_ref[...] = wo_ref[...].astype(BF16)
    w1b_ref[...] = w1_ref[...].astype(BF16)
    w2b_ref[...] = w2_ref[...].astype(BF16)
    la = jnp.minimum(a_logit, 0.0) - jnp.log(1.0 + jnp.exp(-jnp.abs(a_logit)))
    la_ref[...] = la * (1.0 / GATE_TAU)


def _proj_call(x2, g1, w_in_t, w_gate_up, bg, wo, w1, w2):
    s = x2.shape[0]
    n = s // TM_PROJ
    row = lambda i: (i, 0)
    col = lambda i: (0, i)
    const = lambda i: (0, 0)
    return pl.pallas_call(
        _proj_kernel,
        grid=(n,),
        in_specs=[
            pl.BlockSpec((TM_PROJ, D_MODEL), row),
            pl.BlockSpec((1, D_MODEL), const),
            pl.BlockSpec((None, D_IN, D_MODEL), lambda i: (0, 0, 0), pipeline_mode=pl.Buffered(1)),
            pl.BlockSpec((GATE_RANK, GLA_KEY), const),
            pl.BlockSpec((1, GLA_KEY), const),
            pl.BlockSpec((D_MODEL // n, D_MODEL), row),
            pl.BlockSpec((D_MODEL, D_FF // n), col),
            pl.BlockSpec((D_FF // n, D_MODEL), row),
        ],
        out_specs=[
            pl.BlockSpec((TM_PROJ, GLA_KEY), row),
            pl.BlockSpec((TM_PROJ, GLA_KEY), row),
            pl.BlockSpec((TM_PROJ, GLA_KEY), row),
            pl.BlockSpec((TM_PROJ, GLA_VAL), row),
            pl.BlockSpec((TM_PROJ, GLA_VAL), row),
            pl.BlockSpec((TM_PROJ, CONV_CH), row),
            pl.BlockSpec((D_MODEL // n, D_MODEL), row),
            pl.BlockSpec((D_MODEL, D_FF // n), col),
            pl.BlockSpec((D_FF // n, D_MODEL), row),
        ],
        out_shape=[
            jax.ShapeDtypeStruct((s, GLA_KEY), BF16),
            jax.ShapeDtypeStruct((s, GLA_KEY), BF16),
            jax.ShapeDtypeStruct((s, GLA_KEY), F32),
            jax.ShapeDtypeStruct((s, GLA_VAL), BF16),
            jax.ShapeDtypeStruct((s, GLA_VAL), BF16),
            jax.ShapeDtypeStruct((s, CONV_CH), F32),
            jax.ShapeDtypeStruct((D_MODEL, D_MODEL), BF16),
            jax.ShapeDtypeStruct((D_MODEL, D_FF), BF16),
            jax.ShapeDtypeStruct((D_FF, D_MODEL), BF16),
        ],
        scratch_shapes=[
            pltpu.VMEM((D_INP, D_MODEL), BF16),
            pltpu.VMEM((Z_PAD, GLA_KEY), BF16),
        ],
        compiler_params=pltpu.CompilerParams(
            dimension_semantics=("arbitrary",), vmem_limit_bytes=VMEM_LIMIT),
        name="proj",
    )(x2, g1, w_in_t, w_gate_up, bg, wo, w1, w2)


def _gla_kernel(q_ref, k_ref, la_ref, v_ref, sg_ref, gn_ref, o_ref, st_ref, b_ref):
    c = GLA_SUB

    @pl.when(pl.program_id(0) == 0)
    def _():
        st_ref[...] = jnp.zeros_like(st_ref)

    ri = lax.broadcasted_iota(jnp.int32, (c, c), 0)
    ci = lax.broadcasted_iota(jnp.int32, (c, c), 1)
    first_head = lax.broadcasted_iota(jnp.int32, (c, LANES), 1) < GLA_DK
    st_row = lax.broadcasted_iota(jnp.int32, (2 * GLA_DV, LANES), 0)
    st_lane = lax.broadcasted_iota(jnp.int32, (2 * GLA_DV, LANES), 1)
    same_head = (st_row < GLA_DV) == (st_lane < GLA_DK)

    def masked_sum(m, rows):
        mb = m.astype(BF16)
        la_hi, la_lo = _split_bf16(la_ref[rows, :])
        return _dot(mb, la_hi) + _dot(mb, la_lo)

    b_min = None
    for s in range(GLA_NSUB):
        rows = slice(s * c, (s + 1) * c)
        b = masked_sum(ci <= ri, rows)
        b_ref[rows, :] = b
        b_min = b[c - 1:c, :] if b_min is None else jnp.minimum(b_min, b[c - 1:c, :])
    fast = jnp.min(b_min) > GLA_FAST_MIN_B

    def chunk_pre(rows, scores_fn):
        b = b_ref[rows, :]
        b_last = b[c - 1:c, :]
        q = q_ref[rows, :].astype(F32)
        k = k_ref[rows, :].astype(F32)
        q_in = (q * jnp.exp(b)).astype(BF16)
        k_out = (k * jnp.exp(b_last - b)).astype(BF16)
        intra, upds = [], []
        for p in range(GLA_HEADS // 2):
            kc = slice(p * LANES, (p + 1) * LANES)
            v_p = v_ref[rows, 2 * p * GLA_DV:2 * (p + 1) * GLA_DV]
            for j, s_h in enumerate(scores_fn(p, q, k, b, q_in)):
                intra.append(_dot(s_h.astype(BF16), v_p[:, j * GLA_DV:(j + 1) * GLA_DV]))
            upd = _dot_tn(v_p, k_out[:, kc])
            upds.append(jnp.where(same_head, upd, 0.0))
        return rows, q_in, jnp.exp(b_last), intra, upds

    def chunk_post(pre, states):
        rows, q_in, a_chunk, intra, upds = pre
        new_states = []
        for p in range(GLA_HEADS // 2):
            kc = slice(p * LANES, (p + 1) * LANES)
            st = states[p]
            o_inter = _dot_nt(q_in[:, kc], st.astype(BF16))
            for j in range(2):
                h = 2 * p + j
                o_h = intra[h] + o_inter[:, j * GLA_DV:(j + 1) * GLA_DV]
                ms = jnp.mean(o_h * o_h, axis=-1, keepdims=True)
                y = o_h * lax.rsqrt(ms + EPS) * gn_ref[...]
                cols = slice(h * GLA_DV, (h + 1) * GLA_DV)
                o_ref[rows, cols] = (y * sg_ref[rows, cols].astype(F32)).astype(BF16)
            new_states.append(a_chunk[:, kc] * st + upds[p])
        return new_states

    def fast_scores(p, q, k, b, q_in):
        kc = slice(p * LANES, (p + 1) * LANES)
        k_div = k[:, kc] * jnp.exp(-b[:, kc])
        k_blk = jnp.concatenate([jnp.where(first_head, k_div, 0.0),
                                 jnp.where(first_head, 0.0, k_div)], axis=0).astype(BF16)
        s2 = _dot_nt(q_in[:, kc], k_blk)
        return [jnp.where(ri >= ci, s2[:, j * c:(j + 1) * c], 0.0) for j in range(2)]

    def safe_scores(rows):
        q = q_ref[rows, :].astype(F32)
        k = k_ref[rows, :].astype(F32)
        qk = q * k
        scores = []
        for h in range(GLA_HEADS):
            dk = slice(h * GLA_DK, (h + 1) * GLA_DK)
            scores.append(jnp.where(ri == ci, jnp.sum(qk[:, dk], axis=-1, keepdims=True), 0.0))
        half = c // 2
        while half >= 1:
            ref = (ri & (-2 * half)) + (half - 1)
            lo = jnp.minimum(ri, ref)
            hi = jnp.maximum(ri, ref)
            e = jnp.exp(masked_sum((ci > lo) & (ci <= hi), rows))
            q_l = (q * e).astype(BF16)
            k_l = (k * e).astype(BF16)
            pair = (((ri ^ ci) & (-2 * half)) == 0) & ((ri & half) != 0) & ((ci & half) == 0)
            for h in range(GLA_HEADS):
                dk = slice(h * GLA_DK, (h + 1) * GLA_DK)
                scores[h] = jnp.where(pair, _dot_nt(q_l[:, dk], k_l[:, dk]), scores[h])
            half //= 2
        return scores

    @pl.when(fast)
    def _():
        pres = [chunk_pre(slice(s * c, (s + 1) * c), fast_scores) for s in range(GLA_NSUB)]
        states = [st_ref[p] for p in range(GLA_HEADS // 2)]
        for pre in pres:
            states = chunk_post(pre, states)
        for p in range(GLA_HEADS // 2):
            st_ref[p] = states[p]

    @pl.when(jnp.logical_not(fast))
    def _():
        def body(s, carry):
            rows = pl.ds(pl.multiple_of(s * c, c), c)
            scores = safe_scores(rows)
            pre = chunk_pre(rows, lambda p, *_: scores[2 * p:2 * p + 2])
            states = chunk_post(pre, [st_ref[p] for p in range(GLA_HEADS // 2)])
            for p in range(GLA_HEADS // 2):
                st_ref[p] = states[p]
            return carry

        lax.fori_loop(0, GLA_NSUB, body, 0)


def _gla_call(q, k, la, v, sg, gn):
    s = q.shape[0]
    n = s // TM_GLA
    row = lambda i: (i, 0)
    const = lambda i: (0, 0)
    return pl.pallas_call(
        _gla_kernel,
        grid=(n,),
        in_specs=[
            pl.BlockSpec((TM_GLA, GLA_KEY), row),
            pl.BlockSpec((TM_GLA, GLA_KEY), row),
            pl.BlockSpec((TM_GLA, GLA_KEY), row),
            pl.BlockSpec((TM_GLA, GLA_VAL), row),
            pl.BlockSpec((TM_GLA, GLA_VAL), row),
            pl.BlockSpec((1, GLA_DV), const),
        ],
        out_specs=pl.BlockSpec((TM_GLA, GLA_VAL), row),
        out_shape=jax.ShapeDtypeStruct((s, GLA_VAL), BF16),
        scratch_shapes=[
            pltpu.VMEM((GLA_HEADS // 2, 2 * GLA_DV, LANES), F32),
            pltpu.VMEM((TM_GLA, GLA_KEY), F32),
        ],
        compiler_params=pltpu.CompilerParams(
            dimension_semantics=("arbitrary",), vmem_limit_bytes=VMEM_LIMIT),
        name="gla",
    )(q, k, la, v, sg, gn)


def _group_mean(t, low_half, size):
    lo = jnp.sum(jnp.where(low_half, t, 0.0), axis=-1, keepdims=True)
    hi = jnp.sum(jnp.where(low_half, 0.0, t), axis=-1, keepdims=True)
    return jnp.where(low_half, lo, hi) * (1.0 / size)


def _interleave(mxu_items, vec_items):
    done = 0
    for i, item in enumerate(mxu_items):
        item()
        upto = (i + 1) * len(vec_items) // len(mxu_items)
        for v_item in vec_items[done:upto]:
            v_item()
        done = upto


def _mlp_conv_kernel(x_ref, og_ref, un_ref, u0_ref, cw_ref, cb_ref, gg_ref, gb_ref,
                     wo_hbm, g2_ref, w1_hbm, w2_hbm, gf_ref,
                     o_ref, ext_ref, sh_ref, oc_ref, f_ref, wo_ref, w1_ref, w2_ref, w_sem):
    tm = TM_MLP

    def weight_copies():
        return [pltpu.make_async_copy(src, dst, w_sem.at[j])
                for j, (src, dst) in enumerate(((wo_hbm, wo_ref), (w1_hbm, w1_ref), (w2_hbm, w2_ref)))]

    grp = CONV_CH // CONV_GROUPS
    low_half = lax.broadcasted_iota(jnp.int32, (CONV_RB, LANES), 1) < grp
    first = CONV_HALO - (CONV_W - 1)

    def shifted_copy(p):
        sh_ref[p - 1] = ext_ref[p:p + CONV_SH_ROWS, :]

    def conv_block(base):
        for cg in range(CONV_CH // LANES):
            cols = slice(cg * LANES, (cg + 1) * LANES)
            acc = jnp.broadcast_to(cb_ref[:, cols], (CONV_RB, LANES))
            for t in range(CONV_W):
                shift = first + t
                p = shift % SUBLANES
                rows = pl.ds(base + (shift - p), CONV_RB)
                tap = ext_ref[rows, cols] if p == 0 else sh_ref[p - 1, rows, cols]
                acc = acc + cw_ref[t, :, cols] * tap
            d = acc - _group_mean(acc, low_half, grp)
            var = _group_mean(d * d, low_half, grp)
            yn = d * lax.rsqrt(var + EPS) * gg_ref[:, cols] + gb_ref[:, cols]
            oc_ref[pl.ds(base, CONV_RB), cols] = (yn * _sigmoid(yn)).astype(BF16)

    @pl.when(pl.program_id(0) == 0)
    def _():
        for cp in weight_copies():
            cp.start()
        ext_ref[0:CONV_HALO, :] = jnp.zeros((CONV_HALO, CONV_CH), F32)
        ext_ref[CONV_HALO:, :] = u0_ref[...]
        for p in range(1, SUBLANES):
            shifted_copy(p)

        def body(r, carry):
            for k in range(CONV_PRO_BLOCKS):
                conv_block(pl.multiple_of((r * CONV_PRO_BLOCKS + k) * CONV_RB, CONV_RB))
            return carry

        lax.fori_loop(0, tm // (CONV_RB * CONV_PRO_BLOCKS), body, 0)
        for cp in weight_copies():
            cp.wait()

    h = x_ref[...] + _dot(og_ref[...], wo_ref[0:GLA_VAL, :]) + _dot(oc_ref[...], wo_ref[GLA_VAL:, :])
    hg = (h * g2_ref[...]).astype(BF16)
    r = lax.rsqrt(jnp.mean(h * h, axis=-1, keepdims=True) + EPS)
    r2 = r * r

    def roll_window():
        ext_ref[0:CONV_HALO, :] = ext_ref[tm:tm + CONV_HALO, :]
        ext_ref[CONV_HALO:, :] = un_ref[...]

    vec_items = [roll_window] + [functools.partial(shifted_copy, p) for p in range(1, SUBLANES)]
    vec_items += [functools.partial(conv_block, r * CONV_RB) for r in range(tm // CONV_RB)]

    def ff_in(j):
        cols = slice(j * FF_CHUNK, (j + 1) * FF_CHUNK)
        f = jnp.maximum(_dot(hg, w1_ref[:, cols]), 0.0)
        f_ref[:, cols] = (f * f * r2).astype(BF16)

    def ff_out(j):
        cols = slice(j * OUT_CHUNK, (j + 1) * OUT_CHUNK)
        o_ref[:, cols] = h[:, cols] + _dot(f_ref[...], w2_ref[:, cols])

    mxu_items = [functools.partial(ff_in, j) for j in range(D_FF // FF_CHUNK)]
    mxu_items += [functools.partial(ff_out, j) for j in range(D_MODEL // OUT_CHUNK)]
    _interleave(mxu_items, vec_items)

    acc = o_ref[...]
    ms2 = jnp.mean(acc * acc, axis=-1, keepdims=True)
    o_ref[...] = acc * lax.rsqrt(ms2 + EPS) * gf_ref[...]


def _mlp_conv_call(x2, og, u, cw, cb, gg, gb, wo, g2, w1, w2, gf):
    s = x2.shape[0]
    n = s // TM_MLP
    row = lambda i: (i, 0)
    nxt = lambda i: (jnp.minimum(i + 1, n - 1), 0)
    const = lambda i: (0, 0)
    single = pl.Buffered(1)
    return pl.pallas_call(
        _mlp_conv_kernel,
        grid=(n,),
        in_specs=[
            pl.BlockSpec((TM_MLP, D_MODEL), row),
            pl.BlockSpec((TM_MLP, GLA_VAL), row),
            pl.BlockSpec((TM_MLP, CONV_CH), nxt),
            pl.BlockSpec((TM_MLP, CONV_CH), const, pipeline_mode=single),
            pl.BlockSpec((CONV_W, 1, CONV_CH), lambda i: (0, 0, 0)),
            pl.BlockSpec((1, CONV_CH), const),
            pl.BlockSpec((1, CONV_CH), const),
            pl.BlockSpec((1, CONV_CH), const),
            pl.BlockSpec(memory_space=pl.ANY),
            pl.BlockSpec((1, D_MODEL), const),
            pl.BlockSpec(memory_space=pl.ANY),
            pl.BlockSpec(memory_space=pl.ANY),
            pl.BlockSpec((1, D_MODEL), const),
        ],
        out_specs=pl.BlockSpec((TM_MLP, D_MODEL), row),
        out_shape=jax.ShapeDtypeStruct((s, D_MODEL), F32),
        scratch_shapes=[
            pltpu.VMEM((TM_MLP + CONV_HALO, CONV_CH), F32),
            pltpu.VMEM((SUBLANES - 1, CONV_SH_ROWS, CONV_CH), F32),
            pltpu.VMEM((TM_MLP, CONV_CH), BF16),
            pltpu.VMEM((TM_MLP, D_FF), BF16),
            pltpu.VMEM((D_MODEL, D_MODEL), BF16),
            pltpu.VMEM((D_MODEL, D_FF), BF16),
            pltpu.VMEM((D_FF, D_MODEL), BF16),
            pltpu.SemaphoreType.DMA((3,)),
        ],
        compiler_params=pltpu.CompilerParams(
            dimension_semantics=("arbitrary",), vmem_limit_bytes=VMEM_LIMIT),
        name="mlp_conv",
    )(x2, og, u, u, cw, cb, gg, gb, wo, g2, w1, w2, gf)


def kernel(x, norm1_g, w_in, w_gate_up, b_gate, gla_norm_g, conv_w, conv_b, conv_norm_g,
           conv_norm_b, w_out, norm2_g, w_mlp_in, w_mlp_out, final_norm_g):
    bsz, seq, _ = x.shape
    x2 = x.reshape(bsz * seq, D_MODEL)
    assert bsz == 1, "state / halo carry across grid steps assumes one sequence"
    l = 0
    assert w_in.shape[0] == 1, "single layer: the projection weight is passed with its layer axis"
    w_in_t = jnp.swapaxes(w_in, 1, 2)
    q, k, la, v, sg, u, wo_b, w1_b, w2_b = _proj_call(
        x2, norm1_g[l][None, :], w_in_t, w_gate_up[l], b_gate[l][None, :], w_out[l], w_mlp_in[l],
        w_mlp_out[l])
    o_gla = _gla_call(q, k, la, v, sg, gla_norm_g[l][None, :])
    out = _mlp_conv_call(x2, o_gla, u, jnp.swapaxes(conv_w, 0, 1), conv_b[l][None, :], conv_norm_g[l][None, :],
                         conv_norm_b[l][None, :], wo_b, norm2_g[l][None, :], w1_b, w2_b,
                         final_norm_g[None, :])
    return out.reshape(bsz, seq, D_MODEL)
```

```python
import functools

import jax
import jax.numpy as jnp
from jax import lax
from jax.experimental import pallas as pl
from jax.experimental.pallas import tpu as pltpu

D_MODEL = 1024
GLA_HEADS = 4
GLA_DV = 128
GLA_DK = 64
GLA_KEY = GLA_HEADS * GLA_DK
GLA_VAL = GLA_HEADS * GLA_DV
GATE_RANK = 16
GATE_TAU = 16.0
CONV_CH = 512
CONV_GROUPS = 8
CONV_W = 31
D_FF = 4 * D_MODEL
EPS = 1e-6

OFF_Q = 0
OFF_K = OFF_Q + GLA_KEY
OFF_V = OFF_K + GLA_KEY
OFF_G = OFF_V + GLA_VAL
OFF_A = OFF_G + GLA_VAL
OFF_C = OFF_A + GATE_RANK
D_IN = OFF_C + 2 * CONV_CH

LANES = 128
SUBLANES = 8
Z_PAD = LANES
P_Z = 0
P_VA = P_Z + Z_PAD
P_CG = P_VA + LANES
P_CI = P_CG + CONV_CH
P_SPLIT = P_CI + CONV_CH
P_Q = P_SPLIT
P_G = P_Q + GLA_KEY
P_K = P_G + GLA_VAL
P_VB = P_K + GLA_KEY
D_INP = P_VB + GLA_VAL - LANES

TM_PROJ = 1024
GLA_SUB = 256
GLA_NSUB = 8
TM_GLA = GLA_SUB * GLA_NSUB
TM_MLP = 512
FF_CHUNK = 512
OUT_CHUNK = 256
CONV_HALO = 32
CONV_RB = 32
CONV_PRO_BLOCKS = 4
CONV_SH_ROWS = TM_MLP + CONV_HALO - SUBLANES
GLA_FAST_MIN_B = -60.0

V7X_VMEM_BYTES = 64 * 1024 * 1024
VMEM_LIMIT = V7X_VMEM_BYTES - 8 * 1024 * 1024

F32 = jnp.float32
BF16 = jnp.bfloat16


def _sigmoid(x):
    return 1.0 / (1.0 + jnp.exp(-x))


def _split_bf16(x):
    hi = x.astype(BF16)
    lo = (x - hi.astype(F32)).astype(BF16)
    return hi, lo


def _dot(a, b):
    return jnp.dot(a, b, preferred_element_type=F32)


def _dot_nt(a, b):
    return lax.dot_general(a, b, (((1,), (1,)), ((), ())), preferred_element_type=F32)


def _dot_tn(a, b):
    return lax.dot_general(a, b, (((0,), (0,)), ((), ())), preferred_element_type=F32)


def _proj_kernel(x_ref, g1_ref, w_ref, wg_ref, bg_ref, wo_ref, w1_ref, w2_ref,
                 q_ref, k_ref, la_ref, v_ref, sg_ref, u_ref, wob_ref, w1b_ref, w2b_ref,
                 wb_ref, wgb_ref):
    @pl.when(pl.program_id(0) == 0)
    def _():
        def put(dst, src, size, scale=None):
            w = w_ref[src:src + size, :] * g1_ref[...]
            wb_ref[dst:dst + size, :] = (w if scale is None else w * scale).astype(BF16)

        put(P_Q, OFF_Q, GLA_KEY, GLA_DK ** -0.5)
        put(P_K, OFF_K, GLA_KEY)
        put(P_VA, OFF_V, LANES)
        put(P_VB, OFF_V + LANES, GLA_VAL - LANES)
        put(P_G, OFF_G, GLA_VAL)
        put(P_CI, OFF_C, CONV_CH)
        put(P_CG, OFF_C + CONV_CH, CONV_CH)
        put(P_Z, OFF_A, GATE_RANK)
        wb_ref[P_Z + GATE_RANK:P_Z + Z_PAD, :] = jnp.zeros((Z_PAD - GATE_RANK, D_MODEL), BF16)
        wgb_ref[...] = jnp.zeros_like(wgb_ref)
        wgb_ref[0:GATE_RANK, :] = wg_ref[...].astype(BF16)

    x = x_ref[...]
    r = lax.rsqrt(jnp.mean(x * x, axis=-1, keepdims=True) + EPS)
    xb = x.astype(BF16)
    head = _dot_nt(xb, wb_ref[0:P_SPLIT, :]) * r
    z = head[:, P_Z:P_Z + Z_PAD].astype(BF16)
    a_logit = _dot(z, wgb_ref[...]) + bg_ref[...]
    rest = _dot_nt(xb, wb_ref[P_SPLIT:D_INP, :]) * r
    u_ref[...] = head[:, P_CI:P_CI + CONV_CH] * _sigmoid(head[:, P_CG:P_CG + CONV_CH])
    v_ref[:, 0:LANES] = head[:, P_VA:P_VA + LANES].astype(BF16)
    g = rest[:, P_G - P_SPLIT:P_G - P_SPLIT + GLA_VAL]
    sg_ref[...] = (g * _sigmoid(g)).astype(BF16)
    q_ref[...] = rest[:, P_Q - P_SPLIT:P_Q - P_SPLIT + GLA_KEY].astype(BF16)
    k_ref[...] = rest[:, P_K - P_SPLIT:P_K - P_SPLIT + GLA_KEY].astype(BF16)
    v_ref[:, LANES:GLA_VAL] = rest[:, P_VB - P_SPLIT:D_INP - P_SPLIT].astype(BF16)
    wob_ref[...] = wo_ref[...].astype(BF16)
    w1b_ref[...] = w1_ref[...].astype(BF16)
    w2b_ref[...] = w2_ref[...].astype(BF16)
    la = jnp.minimum(a_logit, 0.0) - jnp.log(1.0 + jnp.exp(-jnp.abs(a_logit)))
    la_ref[...] = la * (1.0 / GATE_TAU)


def _proj_call(x2, g1, w_in_t, w_gate_up, bg, wo, w1, w2):
    s = x2.shape[0]
    n = s // TM_PROJ
    row = lambda i: (i, 0)
    col = lambda i: (0, i)
    const = lambda i: (0, 0)
    return pl.pallas_call(
        _proj_kernel,
        grid=(n,),
        in_specs=[
            pl.BlockSpec((TM_PROJ, D_MODEL), row),
            pl.BlockSpec((1, D_MODEL), const),
            pl.BlockSpec((None, D_IN, D_MODEL), lambda i: (0, 0, 0), pipeline_mode=pl.Buffered(1)),
            pl.BlockSpec((GATE_RANK, GLA_KEY), const),
            pl.BlockSpec((1, GLA_KEY), const),
            pl.BlockSpec((D_MODEL // n, D_MODEL), row),
            pl.BlockSpec((D_MODEL, D_FF // n), col),
            pl.BlockSpec((D_FF // n, D_MODEL), row),
        ],
        out_specs=[
            pl.BlockSpec((TM_PROJ, GLA_KEY), row),
            pl.BlockSpec((TM_PROJ, GLA_KEY), row),
            pl.BlockSpec((TM_PROJ, GLA_KEY), row),
            pl.BlockSpec((TM_PROJ, GLA_VAL), row),
            pl.BlockSpec((TM_PROJ, GLA_VAL), row),
            pl.BlockSpec((TM_PROJ, CONV_CH), row),
            pl.BlockSpec((D_MODEL // n, D_MODEL), row),
            pl.BlockSpec((D_MODEL, D_FF // n), col),
            pl.BlockSpec((D_FF // n, D_MODEL), row),
        ],
        out_shape=[
            jax.ShapeDtypeStruct((s, GLA_KEY), BF16),
            jax.ShapeDtypeStruct((s, GLA_KEY), BF16),
            jax.ShapeDtypeStruct((s, GLA_KEY), F32),
            jax.ShapeDtypeStruct((s, GLA_VAL), BF16),
            jax.ShapeDtypeStruct((s, GLA_VAL), BF16),
            jax.ShapeDtypeStruct((s, CONV_CH), F32),
            jax.ShapeDtypeStruct((D_MODEL, D_MODEL), BF16),
            jax.ShapeDtypeStruct((D_MODEL, D_FF), BF16),
            jax.ShapeDtypeStruct((D_FF, D_MODEL), BF16),
        ],
        scratch_shapes=[
            pltpu.VMEM((D_INP, D_MODEL), BF16),
            pltpu.VMEM((Z_PAD, GLA_KEY), BF16),
        ],
        compiler_params=pltpu.CompilerParams(
            dimension_semantics=("arbitrary",), vmem_limit_bytes=VMEM_LIMIT),
        name="proj",
    )(x2, g1, w_in_t, w_gate_up, bg, wo, w1, w2)


def _gla_kernel(q_ref, k_ref, la_ref, v_ref, sg_ref, gn_ref, o_ref, st_ref, b_ref):
    c = GLA_SUB

    @pl.when(pl.program_id(0) == 0)
    def _():
        st_ref[...] = jnp.zeros_like(st_ref)

    ri = lax.broadcasted_iota(jnp.int32, (c, c), 0)
    ci = lax.broadcasted_iota(jnp.int32, (c, c), 1)
    first_head = lax.broadcasted_iota(jnp.int32, (c, LANES), 1) < GLA_DK
    st_row = lax.broadcasted_iota(jnp.int32, (2 * GLA_DV, LANES), 0)
    st_lane = lax.broadcasted_iota(jnp.int32, (2 * GLA_DV, LANES), 1)
    same_head = (st_row < GLA_DV) == (st_lane < GLA_DK)

    def masked_sum(m, rows):
        mb = m.astype(BF16)
        la_hi, la_lo = _split_bf16(la_ref[rows, :])
        return _dot(mb, la_hi) + _dot(mb, la_lo)

    b_min = None
    for s in range(GLA_NSUB):
        rows = slice(s * c, (s + 1) * c)
        b_ref[rows, :] = masked_sum(ci <= ri, rows)
        total = jnp.sum(la_ref[rows, :], axis=0, keepdims=True)
        b_min = total if b_min is None else jnp.minimum(b_min, total)
    fast = jnp.min(b_min) > GLA_FAST_MIN_B

    def chunk_pre(rows, scores_fn):
        b = b_ref[rows, :]
        b_last = b[c - 1:c, :]
        q = q_ref[rows, :].astype(F32)
        k = k_ref[rows, :].astype(F32)
        q_in = (q * jnp.exp(b)).astype(BF16)
        k_out = (k * jnp.exp(b_last - b)).astype(BF16)
        intra, upds = [], []
        for p in range(GLA_HEADS // 2):
            kc = slice(p * LANES, (p + 1) * LANES)
            v_p = v_ref[rows, 2 * p * GLA_DV:2 * (p + 1) * GLA_DV]
            for j, s_h in enumerate(scores_fn(p, q, k, b, q_in)):
                intra.append(_dot(s_h.astype(BF16), v_p[:, j * GLA_DV:(j + 1) * GLA_DV]))
            upd = _dot_tn(v_p, k_out[:, kc])
            upds.append(jnp.where(same_head, upd, 0.0))
        return rows, q_in, jnp.exp(b_last), intra, upds

    def chunk_post(pre, states):
        rows, q_in, a_chunk, intra, upds = pre
        new_states = []
        for p in range(GLA_HEADS // 2):
            kc = slice(p * LANES, (p + 1) * LANES)
            st = states[p]
            o_inter = _dot_nt(q_in[:, kc], st.astype(BF16))
            for j in range(2):
                h = 2 * p + j
                o_h = intra[h] + o_inter[:, j * GLA_DV:(j + 1) * GLA_DV]
                ms = jnp.mean(o_h * o_h, axis=-1, keepdims=True)
                y = o_h * lax.rsqrt(ms + EPS) * gn_ref[...]
                cols = slice(h * GLA_DV, (h + 1) * GLA_DV)
                o_ref[rows, cols] = (y * sg_ref[rows, cols].astype(F32)).astype(BF16)
            new_states.append(a_chunk[:, kc] * st + upds[p])
        return new_states

    def fast_scores(p, q, k, b, q_in):
        kc = slice(p * LANES, (p + 1) * LANES)
        k_div = k[:, kc] * jnp.exp(-b[:, kc])
        k_blk = jnp.concatenate([jnp.where(first_head, k_div, 0.0),
                                 jnp.where(first_head, 0.0, k_div)], axis=0).astype(BF16)
        s2 = _dot_nt(q_in[:, kc], k_blk)
        return [jnp.where(ri >= ci, s2[:, j * c:(j + 1) * c], 0.0) for j in range(2)]

    def safe_scores(rows):
        q = q_ref[rows, :].astype(F32)
        k = k_ref[rows, :].astype(F32)
        qk = q * k
        scores = []
        for h in range(GLA_HEADS):
            dk = slice(h * GLA_DK, (h + 1) * GLA_DK)
            scores.append(jnp.where(ri == ci, jnp.sum(qk[:, dk], axis=-1, keepdims=True), 0.0))
        half = c // 2
        while half >= 1:
            ref = (ri & (-2 * half)) + (half - 1)
            lo = jnp.minimum(ri, ref)
            hi = jnp.maximum(ri, ref)
            e = jnp.exp(masked_sum((ci > lo) & (ci <= hi), rows))
            q_l = (q * e).astype(BF16)
            k_l = (k * e).astype(BF16)
            pair = (((ri ^ ci) & (-2 * half)) == 0) & ((ri & half) != 0) & ((ci & half) == 0)
            for h in range(GLA_HEADS):
                dk = slice(h * GLA_DK, (h + 1) * GLA_DK)
                scores[h] = jnp.where(pair, _dot_nt(q_l[:, dk], k_l[:, dk]), scores[h])
            half //= 2
        return scores

    @pl.when(fast)
    def _():
        pres = [chunk_pre(slice(s * c, (s + 1) * c), fast_scores) for s in range(GLA_NSUB)]
        states = [st_ref[p] for p in range(GLA_HEADS // 2)]
        for pre in pres:
            states = chunk_post(pre, states)
        for p in range(GLA_HEADS // 2):
            st_ref[p] = states[p]

    @pl.when(jnp.logical_not(fast))
    def _():
        def body(s, carry):
            rows = pl.ds(pl.multiple_of(s * c, c), c)
            scores = safe_scores(rows)
            pre = chunk_pre(rows, lambda p, *_: scores[2 * p:2 * p + 2])
            states = chunk_post(pre, [st_ref[p] for p in range(GLA_HEADS // 2)])
            for p in range(GLA_HEADS // 2):
                st_ref[p] = states[p]
            return carry

        lax.fori_loop(0, GLA_NSUB, body, 0)


def _gla_call(q, k, la, v, sg, gn):
    s = q.shape[0]
    n = s // TM_GLA
    row = lambda i: (i, 0)
    const = lambda i: (0, 0)
    return pl.pallas_call(
        _gla_kernel,
        grid=(n,),
        in_specs=[
            pl.BlockSpec((TM_GLA, GLA_KEY), row),
            pl.BlockSpec((TM_GLA, GLA_KEY), row),
            pl.BlockSpec((TM_GLA, GLA_KEY), row),
            pl.BlockSpec((TM_GLA, GLA_VAL), row),
            pl.BlockSpec((TM_GLA, GLA_VAL), row),
            pl.BlockSpec((1, GLA_DV), const),
        ],
        out_specs=pl.BlockSpec((TM_GLA, GLA_VAL), row),
        out_shape=jax.ShapeDtypeStruct((s, GLA_VAL), BF16),
        scratch_shapes=[
            pltpu.VMEM((GLA_HEADS // 2, 2 * GLA_DV, LANES), F32),
            pltpu.VMEM((TM_GLA, GLA_KEY), F32),
        ],
        compiler_params=pltpu.CompilerParams(
            dimension_semantics=("arbitrary",), vmem_limit_bytes=VMEM_LIMIT),
        name="gla",
    )(q, k, la, v, sg, gn)


def _group_mean(t, low_half, size):
    lo = jnp.sum(jnp.where(low_half, t, 0.0), axis=-1, keepdims=True)
    hi = jnp.sum(jnp.where(low_half, 0.0, t), axis=-1, keepdims=True)
    return jnp.where(low_half, lo, hi) * (1.0 / size)


def _interleave(mxu_items, vec_items):
    done = 0
    for i, item in enumerate(mxu_items):
        item()
        upto = (i + 1) * len(vec_items) // len(mxu_items)
        for v_item in vec_items[done:upto]:
            v_item()
        done = upto


def _mlp_conv_kernel(x_ref, og_ref, un_ref, u0_ref, cw_ref, cb_ref, gg_ref, gb_ref,
                     wo_hbm, g2_ref, w1_hbm, w2_hbm, gf_ref,
                     o_ref, ext_ref, sh_ref, oc_ref, f_ref, wo_ref, w1_ref, w2_ref, w_sem):
    tm = TM_MLP

    def weight_copies():
        return [pltpu.make_async_copy(src, dst, w_sem.at[j])
                for j, (src, dst) in enumerate(((wo_hbm, wo_ref), (w1_hbm, w1_ref), (w2_hbm, w2_ref)))]

    grp = CONV_CH // CONV_GROUPS
    low_half = lax.broadcasted_iota(jnp.int32, (CONV_RB, LANES), 1) < grp
    first = CONV_HALO - (CONV_W - 1)

    def shifted_copy(p):
        sh_ref[p - 1] = ext_ref[p:p + CONV_SH_ROWS, :]

    def conv_block(base):
        for cg in range(CONV_CH // LANES):
            cols = slice(cg * LANES, (cg + 1) * LANES)
            acc = jnp.broadcast_to(cb_ref[:, cols], (CONV_RB, LANES))
            for t in range(CONV_W):
                shift = first + t
                p = shift % SUBLANES
                rows = pl.ds(base + (shift - p), CONV_RB)
                tap = ext_ref[rows, cols] if p == 0 else sh_ref[p - 1, rows, cols]
                acc = acc + cw_ref[t, :, cols] * tap
            d = acc - _group_mean(acc, low_half, grp)
            var = _group_mean(d * d, low_half, grp)
            yn = d * lax.rsqrt(var + EPS) * gg_ref[:, cols] + gb_ref[:, cols]
            oc_ref[pl.ds(base, CONV_RB), cols] = (yn * _sigmoid(yn)).astype(BF16)

    @pl.when(pl.program_id(0) == 0)
    def _():
        for cp in weight_copies():
            cp.start()
        ext_ref[0:CONV_HALO, :] = jnp.zeros((CONV_HALO, CONV_CH), F32)
        ext_ref[CONV_HALO:, :] = u0_ref[...]
        for p in range(1, SUBLANES):
            shifted_copy(p)

        def body(r, carry):
            for k in range(CONV_PRO_BLOCKS):
                conv_block(pl.multiple_of((r * CONV_PRO_BLOCKS + k) * CONV_RB, CONV_RB))
            return carry

        lax.fori_loop(0, tm // (CONV_RB * CONV_PRO_BLOCKS), body, 0)
        for cp in weight_copies():
            cp.wait()

    h = x_ref[...] + _dot(og_ref[...], wo_ref[0:GLA_VAL, :]) + _dot(oc_ref[...], wo_ref[GLA_VAL:, :])
    hg = (h * g2_ref[...]).astype(BF16)
    r = lax.rsqrt(jnp.mean(h * h, axis=-1, keepdims=True) + EPS)
    r2 = r * r

    def roll_window():
        ext_ref[0:CONV_HALO, :] = ext_ref[tm:tm + CONV_HALO, :]
        ext_ref[CONV_HALO:, :] = un_ref[...]

    vec_items = [roll_window] + [functools.partial(shifted_copy, p) for p in range(1, SUBLANES)]
    vec_items += [functools.partial(conv_block, r * CONV_RB) for r in range(tm // CONV_RB)]

    def ff_in(j):
        cols = slice(j * FF_CHUNK, (j + 1) * FF_CHUNK)
        f = jnp.maximum(_dot(hg, w1_ref[:, cols]), 0.0)
        f_ref[:, cols] = (f * f * r2).astype(BF16)

    def ff_out(j):
        cols = slice(j * OUT_CHUNK, (j + 1) * OUT_CHUNK)
        o_ref[:, cols] = h[:, cols] + _dot(f_ref[...], w2_ref[:, cols])

    mxu_items = [functools.partial(ff_in, j) for j in range(D_FF // FF_CHUNK)]
    mxu_items += [functools.partial(ff_out, j) for j in range(D_MODEL // OUT_CHUNK)]
    _interleave(mxu_items, vec_items)

    acc = o_ref[...]
    ms2 = jnp.mean(acc * acc, axis=-1, keepdims=True)
    o_ref[...] = acc * lax.rsqrt(ms2 + EPS) * gf_ref[...]


def _mlp_conv_call(x2, og, u, cw, cb, gg, gb, wo, g2, w1, w2, gf):
    s = x2.shape[0]
    n = s // TM_MLP
    row = lambda i: (i, 0)
    nxt = lambda i: (jnp.minimum(i + 1, n - 1), 0)
    const = lambda i: (0, 0)
    single = pl.Buffered(1)
    return pl.pallas_call(
        _mlp_conv_kernel,
        grid=(n,),
        in_specs=[
            pl.BlockSpec((TM_MLP, D_MODEL), row),
            pl.BlockSpec((TM_MLP, GLA_VAL), row),
            pl.BlockSpec((TM_MLP, CONV_CH), nxt),
            pl.BlockSpec((TM_MLP, CONV_CH), const, pipeline_mode=single),
            pl.BlockSpec((CONV_W, 1, CONV_CH), lambda i: (0, 0, 0)),
            pl.BlockSpec((1, CONV_CH), const),
            pl.BlockSpec((1, CONV_CH), const),
            pl.BlockSpec((1, CONV_CH), const),
            pl.BlockSpec(memory_space=pl.ANY),
            pl.BlockSpec((1, D_MODEL), const),
            pl.BlockSpec(memory_space=pl.ANY),
            pl.BlockSpec(memory_space=pl.ANY),
            pl.BlockSpec((1, D_MODEL), const),
        ],
        out_specs=pl.BlockSpec((TM_MLP, D_MODEL), row),
        out_shape=jax.ShapeDtypeStruct((s, D_MODEL), F32),
        scratch_shapes=[
            pltpu.VMEM((TM_MLP + CONV_HALO, CONV_CH), F32),
            pltpu.VMEM((SUBLANES - 1, CONV_SH_ROWS, CONV_CH), F32),
            pltpu.VMEM((TM_MLP, CONV_CH), BF16),
            pltpu.VMEM((TM_MLP, D_FF), BF16),
            pltpu.VMEM((D_MODEL, D_MODEL), BF16),
            pltpu.VMEM((D_MODEL, D_FF), BF16),
            pltpu.VMEM((D_FF, D_MODEL), BF16),
            pltpu.SemaphoreType.DMA((3,)),
        ],
        compiler_params=pltpu.CompilerParams(
            dimension_semantics=("arbitrary",), vmem_limit_bytes=VMEM_LIMIT),
        name="mlp_conv",
    )(x2, og, u, u, cw, cb, gg, gb, wo, g2, w1, w2, gf)


def kernel(x, norm1_g, w_in, w_gate_up, b_gate, gla_norm_g, conv_w, conv_b, conv_norm_g,
           conv_norm_b, w_out, norm2_g, w_mlp_in, w_mlp_out, final_norm_g):
    bsz, seq, _ = x.shape
    x2 = x.reshape(bsz * seq, D_MODEL)
    assert bsz == 1, "state / halo carry across grid steps assumes one sequence"
    l = 0
    assert w_in.shape[0] == 1, "single layer: the projection weight is passed with its layer axis"
    w_in_t = jnp.swapaxes(w_in, 1, 2)
    q, k, la, v, sg, u, wo_b, w1_b, w2_b = _proj_call(
        x2, norm1_g[l][None, :], w_in_t, w_gate_up[l], b_gate[l][None, :], w_out[l], w_mlp_in[l],
        w_mlp_out[l])
    o_gla = _gla_call(q, k, la, v, sg, gla_norm_g[l][None, :])
    out = _mlp_conv_call(x2, o_gla, u, jnp.swapaxes(conv_w, 0, 1), conv_b[l][None, :], conv_norm_g[l][None, :],
                         conv_norm_b[l][None, :], wo_b, norm2_g[l][None, :], w1_b, w2_b,
                         final_norm_g[None, :])
    return out.reshape(bsz, seq, D_MODEL)
```

```python
import functools

import jax
import jax.numpy as jnp
from jax import lax
from jax.experimental import pallas as pl
from jax.experimental.pallas import tpu as pltpu

D_MODEL = 1024
GLA_HEADS = 4
GLA_DV = 128
GLA_DK = 64
GLA_KEY = GLA_HEADS * GLA_DK
GLA_VAL = GLA_HEADS * GLA_DV
GATE_RANK = 16
GATE_TAU = 16.0
CONV_CH = 512
CONV_GROUPS = 8
CONV_W = 31
D_FF = 4 * D_MODEL
EPS = 1e-6

OFF_Q = 0
OFF_K = OFF_Q + GLA_KEY
OFF_V = OFF_K + GLA_KEY
OFF_G = OFF_V + GLA_VAL
OFF_A = OFF_G + GLA_VAL
OFF_C = OFF_A + GATE_RANK
D_IN = OFF_C + 2 * CONV_CH

LANES = 128
SUBLANES = 8
Z_PAD = LANES
P_Z = 0
P_VA = P_Z + Z_PAD
P_CG = P_VA + LANES
P_CI = P_CG + CONV_CH
P_SPLIT = P_CI + CONV_CH
P_Q = P_SPLIT
P_G = P_Q + GLA_KEY
P_K = P_G + GLA_VAL
P_VB = P_K + GLA_KEY
D_INP = P_VB + GLA_VAL - LANES

TM_PROJ = 1024
GLA_SUB = 256
GLA_NSUB = 8
TM_GLA = GLA_SUB * GLA_NSUB
TM_MLP = 512
FF_CHUNK = 512
OUT_CHUNK = 256
CONV_HALO = 32
CONV_RB = 32
CONV_PRO_BLOCKS = 4
CONV_SH_ROWS = TM_MLP + CONV_HALO - SUBLANES
GLA_FAST_MIN_B = -60.0

V7X_VMEM_BYTES = 64 * 1024 * 1024
VMEM_LIMIT = V7X_VMEM_BYTES - 8 * 1024 * 1024

F32 = jnp.float32
BF16 = jnp.bfloat16


def _sigmoid(x):
    return 1.0 / (1.0 + jnp.exp(-x))


def _split_bf16(x):
    hi = x.astype(BF16)
    lo = (x - hi.astype(F32)).astype(BF16)
    return hi, lo


def _dot(a, b):
    return jnp.dot(a, b, preferred_element_type=F32)


def _dot_nt(a, b):
    return lax.dot_general(a, b, (((1,), (1,)), ((), ())), preferred_element_type=F32)


def _dot_tn(a, b):
    return lax.dot_general(a, b, (((0,), (0,)), ((), ())), preferred_element_type=F32)


def _proj_kernel(x_ref, g1_ref, w_ref, wg_ref, bg_ref, wo_ref, w1_ref, w2_ref,
                 q_ref, k_ref, la_ref, v_ref, sg_ref, u_ref, wob_ref, w1b_ref, w2b_ref,
                 wb_ref, wgb_ref):
    @pl.when(pl.program_id(0) == 0)
    def _():
        def put(dst, src, size, scale=None):
            w = w_ref[src:src + size, :] * g1_ref[...]
            wb_ref[dst:dst + size, :] = (w if scale is None else w * scale).astype(BF16)

        put(P_Q, OFF_Q, GLA_KEY, GLA_DK ** -0.5)
        put(P_K, OFF_K, GLA_KEY)
        put(P_VA, OFF_V, LANES)
        put(P_VB, OFF_V + LANES, GLA_VAL - LANES)
        put(P_G, OFF_G, GLA_VAL)
        put(P_CI, OFF_C, CONV_CH)
        put(P_CG, OFF_C + CONV_CH, CONV_CH)
        put(P_Z, OFF_A, GATE_RANK)
        wb_ref[P_Z + GATE_RANK:P_Z + Z_PAD, :] = jnp.zeros((Z_PAD - GATE_RANK, D_MODEL), BF16)
        wgb_ref[...] = jnp.zeros_like(wgb_ref)
        wgb_ref[0:GATE_RANK, :] = wg_ref[...].astype(BF16)

    x = x_ref[...]
    r = lax.rsqrt(jnp.mean(x * x, axis=-1, keepdims=True) + EPS)
    xb = x.astype(BF16)
    head = _dot_nt(xb, wb_ref[0:P_SPLIT, :]) * r
    z = head[:, P_Z:P_Z + Z_PAD].astype(BF16)
    a_logit = _dot(z, wgb_ref[...]) + bg_ref[...]
    rest = _dot_nt(xb, wb_ref[P_SPLIT:D_INP, :]) * r
    u_ref[...] = head[:, P_CI:P_CI + CONV_CH] * _sigmoid(head[:, P_CG:P_CG + CONV_CH])
    v_ref[:, 0:LANES] = head[:, P_VA:P_VA + LANES].astype(BF16)
    g = rest[:, P_G - P_SPLIT:P_G - P_SPLIT + GLA_VAL]
    sg_ref[...] = (g * _sigmoid(g)).astype(BF16)
    q_ref[...] = rest[:, P_Q - P_SPLIT:P_Q - P_SPLIT + GLA_KEY].astype(BF16)
    k_ref[...] = rest[:, P_K - P_SPLIT:P_K - P_SPLIT + GLA_KEY].astype(BF16)
    v_ref[:, LANES:GLA_VAL] = rest[:, P_VB - P_SPLIT:D_INP - P_SPLIT].astype(BF16)
    wob_ref[...] = wo_ref[...].astype(BF16)
    w1b_ref[...] = w1_ref[...].astype(BF16)
    w2b_ref[...] = w2_ref[...].astype(BF16)
    la = jnp.minimum(a_logit, 0.0) - jnp.log(1.0 + jnp.exp(-jnp.abs(a_logit)))
    la_ref[...] = la * (1.0 / GATE_TAU)


def _proj_call(x2, g1, w_in_t, w_gate_up, bg, wo, w1, w2):
    s = x2.shape[0]
    n = s // TM_PROJ
    row = lambda i: (i, 0)
    col = lambda i: (0, i)
    const = lambda i: (0, 0)
    return pl.pallas_call(
        _proj_kernel,
        grid=(n,),
        in_specs=[
            pl.BlockSpec((TM_PROJ, D_MODEL), row),
            pl.BlockSpec((1, D_MODEL), const),
            pl.BlockSpec((None, D_IN, D_MODEL), lambda i: (0, 0, 0), pipeline_mode=pl.Buffered(1)),
            pl.BlockSpec((GATE_RANK, GLA_KEY), const),
            pl.BlockSpec((1, GLA_KEY), const),
            pl.BlockSpec((D_MODEL // n, D_MODEL), row),
            pl.BlockSpec((D_MODEL, D_FF // n), col),
            pl.BlockSpec((D_FF // n, D_MODEL), row),
        ],
        out_specs=[
            pl.BlockSpec((TM_PROJ, GLA_KEY), row),
            pl.BlockSpec((TM_PROJ, GLA_KEY), row),
            pl.BlockSpec((TM_PROJ, GLA_KEY), row),
            pl.BlockSpec((TM_PROJ, GLA_VAL), row),
            pl.BlockSpec((TM_PROJ, GLA_VAL), row),
            pl.BlockSpec((TM_PROJ, CONV_CH), row),
            pl.BlockSpec((D_MODEL // n, D_MODEL), row),
            pl.BlockSpec((D_MODEL, D_FF // n), col),
            pl.BlockSpec((D_FF // n, D_MODEL), row),
        ],
        out_shape=[
            jax.ShapeDtypeStruct((s, GLA_KEY), BF16),
            jax.ShapeDtypeStruct((s, GLA_KEY), BF16),
            jax.ShapeDtypeStruct((s, GLA_KEY), F32),
            jax.ShapeDtypeStruct((s, GLA_VAL), BF16),
            jax.ShapeDtypeStruct((s, GLA_VAL), BF16),
            jax.ShapeDtypeStruct((s, CONV_CH), F32),
            jax.ShapeDtypeStruct((D_MODEL, D_MODEL), BF16),
            jax.ShapeDtypeStruct((D_MODEL, D_FF), BF16),
            jax.ShapeDtypeStruct((D_FF, D_MODEL), BF16),
        ],
        scratch_shapes=[
            pltpu.VMEM((D_INP, D_MODEL), BF16),
            pltpu.VMEM((Z_PAD, GLA_KEY), BF16),
        ],
        compiler_params=pltpu.CompilerParams(
            dimension_semantics=("arbitrary",), vmem_limit_bytes=VMEM_LIMIT),
        name="proj",
    )(x2, g1, w_in_t, w_gate_up, bg, wo, w1, w2)


def _gla_kernel(q_ref, k_ref, la_ref, v_ref, sg_ref, gn_ref, o_ref, st_ref, b_ref):
    c = GLA_SUB

    @pl.when(pl.program_id(0) == 0)
    def _():
        st_ref[...] = jnp.zeros_like(st_ref)

    ri = lax.broadcasted_iota(jnp.int32, (c, c), 0)
    ci = lax.broadcasted_iota(jnp.int32, (c, c), 1)
    first_head = lax.broadcasted_iota(jnp.int32, (c, LANES), 1) < GLA_DK
    st_row = lax.broadcasted_iota(jnp.int32, (2 * GLA_DV, LANES), 0)
    st_lane = lax.broadcasted_iota(jnp.int32, (2 * GLA_DV, LANES), 1)
    same_head = (st_row < GLA_DV) == (st_lane < GLA_DK)

    def masked_sum(m, rows):
        mb = m.astype(BF16)
        la_hi, la_lo = _split_bf16(la_ref[rows, :])
        return _dot(jnp.concatenate([mb, mb], axis=1), jnp.concatenate([la_hi, la_lo], axis=0))

    b_min = None
    for s in range(GLA_NSUB):
        rows = slice(s * c, (s + 1) * c)
        b_ref[rows, :] = masked_sum(ci <= ri, rows)
        total = jnp.sum(la_ref[rows, :], axis=0, keepdims=True)
        b_min = total if b_min is None else jnp.minimum(b_min, total)
    fast = jnp.min(b_min) > GLA_FAST_MIN_B

    def chunk_pre(rows, scores_fn):
        b = b_ref[rows, :]
        b_last = b[c - 1:c, :]
        q = q_ref[rows, :].astype(F32)
        k = k_ref[rows, :].astype(F32)
        q_in = (q * jnp.exp(b)).astype(BF16)
        k_out = (k * jnp.exp(b_last - b)).astype(BF16)
        intra, upds = [], []
        for p in range(GLA_HEADS // 2):
            kc = slice(p * LANES, (p + 1) * LANES)
            v_p = v_ref[rows, 2 * p * GLA_DV:2 * (p + 1) * GLA_DV]
            for j, s_h in enumerate(scores_fn(p, q, k, b, q_in)):
                intra.append(_dot(s_h.astype(BF16), v_p[:, j * GLA_DV:(j + 1) * GLA_DV]))
            upd = _dot_tn(v_p, k_out[:, kc])
            upds.append(jnp.where(same_head, upd, 0.0))
        return rows, q_in, jnp.exp(b_last), intra, upds

    def chunk_post(pre, states):
        rows, q_in, a_chunk, intra, upds = pre
        new_states = []
        for p in range(GLA_HEADS // 2):
            kc = slice(p * LANES, (p + 1) * LANES)
            st = states[p]
            o_inter = _dot_nt(q_in[:, kc], st.astype(BF16))
            for j in range(2):
                h = 2 * p + j
                o_h = intra[h] + o_inter[:, j * GLA_DV:(j + 1) * GLA_DV]
                ms = jnp.mean(o_h * o_h, axis=-1, keepdims=True)
                y = o_h * lax.rsqrt(ms + EPS) * gn_ref[...]
                cols = slice(h * GLA_DV, (h + 1) * GLA_DV)
                o_ref[rows, cols] = (y * sg_ref[rows, cols].astype(F32)).astype(BF16)
            new_states.append(a_chunk[:, kc] * st + upds[p])
        return new_states

    def fast_scores(p, q, k, b, q_in):
        kc = slice(p * LANES, (p + 1) * LANES)
        k_div = k[:, kc] * jnp.exp(-b[:, kc])
        k_blk = jnp.concatenate([jnp.where(first_head, k_div, 0.0),
                                 jnp.where(first_head, 0.0, k_div)], axis=0).astype(BF16)
        s2 = _dot_nt(q_in[:, kc], k_blk)
        return [jnp.where(ri >= ci, s2[:, j * c:(j + 1) * c], 0.0) for j in range(2)]

    def safe_scores(rows):
        q = q_ref[rows, :].astype(F32)
        k = k_ref[rows, :].astype(F32)
        qk = q * k
        scores = []
        for h in range(GLA_HEADS):
            dk = slice(h * GLA_DK, (h + 1) * GLA_DK)
            scores.append(jnp.where(ri == ci, jnp.sum(qk[:, dk], axis=-1, keepdims=True), 0.0))
        half = c // 2
        while half >= 1:
            ref = (ri & (-2 * half)) + (half - 1)
            lo = jnp.minimum(ri, ref)
            hi = jnp.maximum(ri, ref)
            e = jnp.exp(masked_sum((ci > lo) & (ci <= hi), rows))
            q_l = (q * e).astype(BF16)
            k_l = (k * e).astype(BF16)
            pair = (((ri ^ ci) & (-2 * half)) == 0) & ((ri & half) != 0) & ((ci & half) == 0)
            for h in range(GLA_HEADS):
                dk = slice(h * GLA_DK, (h + 1) * GLA_DK)
                scores[h] = jnp.where(pair, _dot_nt(q_l[:, dk], k_l[:, dk]), scores[h])
            half //= 2
        return scores

    @pl.when(fast)
    def _():
        pres = [chunk_pre(slice(s * c, (s + 1) * c), fast_scores) for s in range(GLA_NSUB)]
        states = [st_ref[p] for p in range(GLA_HEADS // 2)]
        for pre in pres:
            states = chunk_post(pre, states)
        for p in range(GLA_HEADS // 2):
            st_ref[p] = states[p]

    @pl.when(jnp.logical_not(fast))
    def _():
        def body(s, carry):
            rows = pl.ds(pl.multiple_of(s * c, c), c)
            scores = safe_scores(rows)
            pre = chunk_pre(rows, lambda p, *_: scores[2 * p:2 * p + 2])
            states = chunk_post(pre, [st_ref[p] for p in range(GLA_HEADS // 2)])
            for p in range(GLA_HEADS // 2):
                st_ref[p] = states[p]
            return carry

        lax.fori_loop(0, GLA_NSUB, body, 0)


def _gla_call(q, k, la, v, sg, gn):
    s = q.shape[0]
    n = s // TM_GLA
    row = lambda i: (i, 0)
    const = lambda i: (0, 0)
    return pl.pallas_call(
        _gla_kernel,
        grid=(n,),
        in_specs=[
            pl.BlockSpec((TM_GLA, GLA_KEY), row),
            pl.BlockSpec((TM_GLA, GLA_KEY), row),
            pl.BlockSpec((TM_GLA, GLA_KEY), row),
            pl.BlockSpec((TM_GLA, GLA_VAL), row),
            pl.BlockSpec((TM_GLA, GLA_VAL), row),
            pl.BlockSpec((1, GLA_DV), const),
        ],
        out_specs=pl.BlockSpec((TM_GLA, GLA_VAL), row),
        out_shape=jax.ShapeDtypeStruct((s, GLA_VAL), BF16),
        scratch_shapes=[
            pltpu.VMEM((GLA_HEADS // 2, 2 * GLA_DV, LANES), F32),
            pltpu.VMEM((TM_GLA, GLA_KEY), F32),
        ],
        compiler_params=pltpu.CompilerParams(
            dimension_semantics=("arbitrary",), vmem_limit_bytes=VMEM_LIMIT),
        name="gla",
    )(q, k, la, v, sg, gn)


def _group_mean(t, low_half, size):
    lo = jnp.sum(jnp.where(low_half, t, 0.0), axis=-1, keepdims=True)
    hi = jnp.sum(jnp.where(low_half, 0.0, t), axis=-1, keepdims=True)
    return jnp.where(low_half, lo, hi) * (1.0 / size)


def _interleave(mxu_items, vec_items):
    done = 0
    for i, item in enumerate(mxu_items):
        item()
        upto = (i + 1) * len(vec_items) // len(mxu_items)
        for v_item in vec_items[done:upto]:
            v_item()
        done = upto


def _mlp_conv_kernel(x_ref, og_ref, un_ref, u0_ref, cw_ref, cb_ref, gg_ref, gb_ref,
                     wo_hbm, g2_ref, w1_hbm, w2_hbm, gf_ref,
                     o_ref, ext_ref, sh_ref, oc_ref, f_ref, wo_ref, w1_ref, w2_ref, w_sem):
    tm = TM_MLP

    def weight_copies():
        return [pltpu.make_async_copy(src, dst, w_sem.at[j])
                for j, (src, dst) in enumerate(((wo_hbm, wo_ref), (w1_hbm, w1_ref), (w2_hbm, w2_ref)))]

    grp = CONV_CH // CONV_GROUPS
    low_half = lax.broadcasted_iota(jnp.int32, (CONV_RB, LANES), 1) < grp
    first = CONV_HALO - (CONV_W - 1)

    def shifted_copy(p):
        sh_ref[p - 1] = ext_ref[p:p + CONV_SH_ROWS, :]

    def conv_block(base):
        for cg in range(CONV_CH // LANES):
            cols = slice(cg * LANES, (cg + 1) * LANES)
            acc = jnp.broadcast_to(cb_ref[:, cols], (CONV_RB, LANES))
            for t in range(CONV_W):
                shift = first + t
                p = shift % SUBLANES
                rows = pl.ds(base + (shift - p), CONV_RB)
                tap = ext_ref[rows, cols] if p == 0 else sh_ref[p - 1, rows, cols]
                acc = acc + cw_ref[t, :, cols] * tap
            d = acc - _group_mean(acc, low_half, grp)
            var = _group_mean(d * d, low_half, grp)
            yn = d * lax.rsqrt(var + EPS) * gg_ref[:, cols] + gb_ref[:, cols]
            oc_ref[pl.ds(base, CONV_RB), cols] = (yn * _sigmoid(yn)).astype(BF16)

    @pl.when(pl.program_id(0) == 0)
    def _():
        for cp in weight_copies():
            cp.start()
        ext_ref[0:CONV_HALO, :] = jnp.zeros((CONV_HALO, CONV_CH), F32)
        ext_ref[CONV_HALO:, :] = u0_ref[...]
        for p in range(1, SUBLANES):
            shifted_copy(p)

        def body(r, carry):
            for k in range(CONV_PRO_BLOCKS):
                conv_block(pl.multiple_of((r * CONV_PRO_BLOCKS + k) * CONV_RB, CONV_RB))
            return carry

        lax.fori_loop(0, tm // (CONV_RB * CONV_PRO_BLOCKS), body, 0)
        for cp in weight_copies():
            cp.wait()

    h = x_ref[...] + _dot(og_ref[...], wo_ref[0:GLA_VAL, :]) + _dot(oc_ref[...], wo_ref[GLA_VAL:, :])
    hg = (h * g2_ref[...]).astype(BF16)
    r = lax.rsqrt(jnp.mean(h * h, axis=-1, keepdims=True) + EPS)
    r2 = r * r

    def roll_window():
        ext_ref[0:CONV_HALO, :] = ext_ref[tm:tm + CONV_HALO, :]
        ext_ref[CONV_HALO:, :] = un_ref[...]

    vec_items = [roll_window] + [functools.partial(shifted_copy, p) for p in range(1, SUBLANES)]
    vec_items += [functools.partial(conv_block, r * CONV_RB) for r in range(tm // CONV_RB)]

    def ff_in(j):
        cols = slice(j * FF_CHUNK, (j + 1) * FF_CHUNK)
        f = jnp.maximum(_dot(hg, w1_ref[:, cols]), 0.0)
        f_ref[:, cols] = (f * f * r2).astype(BF16)

    def ff_out(j):
        cols = slice(j * OUT_CHUNK, (j + 1) * OUT_CHUNK)
        o_ref[:, cols] = h[:, cols] + _dot(f_ref[...], w2_ref[:, cols])

    mxu_items = [functools.partial(ff_in, j) for j in range(D_FF // FF_CHUNK)]
    mxu_items += [functools.partial(ff_out, j) for j in range(D_MODEL // OUT_CHUNK)]
    _interleave(mxu_items, vec_items)

    acc = o_ref[...]
    ms2 = jnp.mean(acc * acc, axis=-1, keepdims=True)
    o_ref[...] = acc * lax.rsqrt(ms2 + EPS) * gf_ref[...]


def _mlp_conv_call(x2, og, u, cw, cb, gg, gb, wo, g2, w1, w2, gf):
    s = x2.shape[0]
    n = s // TM_MLP
    row = lambda i: (i, 0)
    nxt = lambda i: (jnp.minimum(i + 1, n - 1), 0)
    const = lambda i: (0, 0)
    single = pl.Buffered(1)
    return pl.pallas_call(
        _mlp_conv_kernel,
        grid=(n,),
        in_specs=[
            pl.BlockSpec((TM_MLP, D_MODEL), row),
            pl.BlockSpec((TM_MLP, GLA_VAL), row),
            pl.BlockSpec((TM_MLP, CONV_CH), nxt),
            pl.BlockSpec((TM_MLP, CONV_CH), const, pipeline_mode=single),
            pl.BlockSpec((CONV_W, 1, CONV_CH), lambda i: (0, 0, 0)),
            pl.BlockSpec((1, CONV_CH), const),
            pl.BlockSpec((1, CONV_CH), const),
            pl.BlockSpec((1, CONV_CH), const),
            pl.BlockSpec(memory_space=pl.ANY),
            pl.BlockSpec((1, D_MODEL), const),
            pl.BlockSpec(memory_space=pl.ANY),
            pl.BlockSpec(memory_space=pl.ANY),
            pl.BlockSpec((1, D_MODEL), const),
        ],
        out_specs=pl.BlockSpec((TM_MLP, D_MODEL), row),
        out_shape=jax.ShapeDtypeStruct((s, D_MODEL), F32),
        scratch_shapes=[
            pltpu.VMEM((TM_MLP + CONV_HALO, CONV_CH), F32),
            pltpu.VMEM((SUBLANES - 1, CONV_SH_ROWS, CONV_CH), F32),
            pltpu.VMEM((TM_MLP, CONV_CH), BF16),
            pltpu.VMEM((TM_MLP, D_FF), BF16),
            pltpu.VMEM((D_MODEL, D_MODEL), BF16),
            pltpu.VMEM((D_MODEL, D_FF), BF16),
            pltpu.VMEM((D_FF, D_MODEL), BF16),
            pltpu.SemaphoreType.DMA((3,)),
        ],
        compiler_params=pltpu.CompilerParams(
            dimension_semantics=("arbitrary",), vmem_limit_bytes=VMEM_LIMIT),
        name="mlp_conv",
    )(x2, og, u, u, cw, cb, gg, gb, wo, g2, w1, w2, gf)


def kernel(x, norm1_g, w_in, w_gate_up, b_gate, gla_norm_g, conv_w, conv_b, conv_norm_g,
           conv_norm_b, w_out, norm2_g, w_mlp_in, w_mlp_out, final_norm_g):
    bsz, seq, _ = x.shape
    x2 = x.reshape(bsz * seq, D_MODEL)
    assert bsz == 1, "state / halo carry across grid steps assumes one sequence"
    l = 0
    assert w_in.shape[0] == 1, "single layer: the projection weight is passed with its layer axis"
    w_in_t = jnp.swapaxes(w_in, 1, 2)
    q, k, la, v, sg, u, wo_b, w1_b, w2_b = _proj_call(
        x2, norm1_g[l][None, :], w_in_t, w_gate_up[l], b_gate[l][None, :], w_out[l], w_mlp_in[l],
        w_mlp_out[l])
    o_gla = _gla_call(q, k, la, v, sg, gla_norm_g[l][None, :])
    out = _mlp_conv_call(x2, o_gla, u, jnp.swapaxes(conv_w, 0, 1), conv_b[l][None, :], conv_norm_g[l][None, :],
                         conv_norm_b[l][None, :], wo_b, norm2_g[l][None, :], w1_b, w2_b,
                         final_norm_g[None, :])
    return out.reshape(bsz, seq, D_MODEL)
```

```python
import functools

import jax
import jax.numpy as jnp
from jax import lax
from jax.experimental import pallas as pl
from jax.experimental.pallas import tpu as pltpu

D_MODEL = 1024
GLA_HEADS = 4
GLA_DV = 128
GLA_DK = 64
GLA_KEY = GLA_HEADS * GLA_DK
GLA_VAL = GLA_HEADS * GLA_DV
GATE_RANK = 16
GATE_TAU = 16.0
CONV_CH = 512
CONV_GROUPS = 8
CONV_W = 31
D_FF = 4 * D_MODEL
EPS = 1e-6

OFF_Q = 0
OFF_K = OFF_Q + GLA_KEY
OFF_V = OFF_K + GLA_KEY
OFF_G = OFF_V + GLA_VAL
OFF_A = OFF_G + GLA_VAL
OFF_C = OFF_A + GATE_RANK
D_IN = OFF_C + 2 * CONV_CH

LANES = 128
SUBLANES = 8
Z_PAD = LANES
P_Z = 0
P_VA = P_Z + Z_PAD
P_CG = P_VA + LANES
P_CI = P_CG + CONV_CH
P_SPLIT = P_CI + CONV_CH
P_Q = P_SPLIT
P_G = P_Q + GLA_KEY
P_K = P_G + GLA_VAL
P_VB = P_K + GLA_KEY
D_INP = P_VB + GLA_VAL - LANES

TM_PROJ = 1024
GLA_SUB = 256
GLA_NSUB = 8
TM_GLA = GLA_SUB * GLA_NSUB
TM_MLP = 512
FF_CHUNK = 512
OUT_CHUNK = 256
CONV_HALO = 32
CONV_RB = 32
CONV_PRO_BLOCKS = 4
CONV_SH_ROWS = TM_MLP + CONV_HALO - SUBLANES
GLA_FAST_MIN_B = -60.0

V7X_VMEM_BYTES = 64 * 1024 * 1024
VMEM_LIMIT = V7X_VMEM_BYTES - 8 * 1024 * 1024

F32 = jnp.float32
BF16 = jnp.bfloat16


def _sigmoid(x):
    return 1.0 / (1.0 + jnp.exp(-x))


def _split_bf16(x):
    hi = x.astype(BF16)
    lo = (x - hi.astype(F32)).astype(BF16)
    return hi, lo


def _dot(a, b):
    return jnp.dot(a, b, preferred_element_type=F32)


def _dot_nt(a, b):
    return lax.dot_general(a, b, (((1,), (1,)), ((), ())), preferred_element_type=F32)


def _dot_tn(a, b):
    return lax.dot_general(a, b, (((0,), (0,)), ((), ())), preferred_element_type=F32)


def _proj_kernel(x_ref, g1_ref, w_ref, wg_ref, bg_ref, wo_ref, w1_ref, w2_ref,
                 q_ref, k_ref, la_ref, v_ref, sg_ref, u_ref, wob_ref, w1b_ref, w2b_ref,
                 wb_ref, wgb_ref):
    @pl.when(pl.program_id(0) == 0)
    def _():
        def put(dst, src, size, scale=None):
            w = w_ref[src:src + size, :] * g1_ref[...]
            wb_ref[dst:dst + size, :] = (w if scale is None else w * scale).astype(BF16)

        put(P_Q, OFF_Q, GLA_KEY, GLA_DK ** -0.5)
        put(P_K, OFF_K, GLA_KEY)
        put(P_VA, OFF_V, LANES)
        put(P_VB, OFF_V + LANES, GLA_VAL - LANES)
        put(P_G, OFF_G, GLA_VAL)
        put(P_CI, OFF_C, CONV_CH)
        put(P_CG, OFF_C + CONV_CH, CONV_CH)
        put(P_Z, OFF_A, GATE_RANK)
        wb_ref[P_Z + GATE_RANK:P_Z + Z_PAD, :] = jnp.zeros((Z_PAD - GATE_RANK, D_MODEL), BF16)
        wgb_ref[...] = jnp.zeros_like(wgb_ref)
        wgb_ref[0:GATE_RANK, :] = wg_ref[...].astype(BF16)

    x = x_ref[...]
    r = lax.rsqrt(jnp.mean(x * x, axis=-1, keepdims=True) + EPS)
    xb = x.astype(BF16)
    head = _dot_nt(xb, wb_ref[0:P_SPLIT, :]) * r
    z = head[:, P_Z:P_Z + Z_PAD].astype(BF16)
    a_logit = _dot(z, wgb_ref[...]) + bg_ref[...]
    rest = _dot_nt(xb, wb_ref[P_SPLIT:D_INP, :]) * r
    u_ref[...] = head[:, P_CI:P_CI + CONV_CH] * _sigmoid(head[:, P_CG:P_CG + CONV_CH])
    v_ref[:, 0:LANES] = head[:, P_VA:P_VA + LANES].astype(BF16)
    g = rest[:, P_G - P_SPLIT:P_G - P_SPLIT + GLA_VAL]
    sg_ref[...] = (g * _sigmoid(g)).astype(BF16)
    q_ref[...] = rest[:, P_Q - P_SPLIT:P_Q - P_SPLIT + GLA_KEY].astype(BF16)
    k_ref[...] = rest[:, P_K - P_SPLIT:P_K - P_SPLIT + GLA_KEY].astype(BF16)
    v_ref[:, LANES:GLA_VAL] = rest[:, P_VB - P_SPLIT:D_INP - P_SPLIT].astype(BF16)
    wob_ref[...] = wo_ref[...].astype(BF16)
    w1b_ref[...] = w1_ref[...].astype(BF16)
    w2b_ref[...] = w2_ref[...].astype(BF16)
    la = jnp.minimum(a_logit, 0.0) - jnp.log(1.0 + jnp.exp(-jnp.abs(a_logit)))
    la_ref[...] = la * (1.0 / GATE_TAU)


def _proj_call(x2, g1, w_in_t, w_gate_up, bg, wo, w1, w2):
    s = x2.shape[0]
    n = s // TM_PROJ
    row = lambda i: (i, 0)
    col = lambda i: (0, i)
    const = lambda i: (0, 0)
    return pl.pallas_call(
        _proj_kernel,
        grid=(n,),
        in_specs=[
            pl.BlockSpec((TM_PROJ, D_MODEL), row),
            pl.BlockSpec((1, D_MODEL), const),
            pl.BlockSpec((None, D_IN, D_MODEL), lambda i: (0, 0, 0), pipeline_mode=pl.Buffered(1)),
            pl.BlockSpec((GATE_RANK, GLA_KEY), const),
            pl.BlockSpec((1, GLA_KEY), const),
            pl.BlockSpec((D_MODEL // n, D_MODEL), row),
            pl.BlockSpec((D_MODEL, D_FF // n), col),
            pl.BlockSpec((D_FF // n, D_MODEL), row),
        ],
        out_specs=[
            pl.BlockSpec((TM_PROJ, GLA_KEY), row),
            pl.BlockSpec((TM_PROJ, GLA_KEY), row),
            pl.BlockSpec((TM_PROJ, GLA_KEY), row),
            pl.BlockSpec((TM_PROJ, GLA_VAL), row),
            pl.BlockSpec((TM_PROJ, GLA_VAL), row),
            pl.BlockSpec((TM_PROJ, CONV_CH), row),
            pl.BlockSpec((D_MODEL // n, D_MODEL), row),
            pl.BlockSpec((D_MODEL, D_FF // n), col),
            pl.BlockSpec((D_FF // n, D_MODEL), row),
        ],
        out_shape=[
            jax.ShapeDtypeStruct((s, GLA_KEY), BF16),
            jax.ShapeDtypeStruct((s, GLA_KEY), BF16),
            jax.ShapeDtypeStruct((s, GLA_KEY), F32),
            jax.ShapeDtypeStruct((s, GLA_VAL), BF16),
            jax.ShapeDtypeStruct((s, GLA_VAL), BF16),
            jax.ShapeDtypeStruct((s, CONV_CH), F32),
            jax.ShapeDtypeStruct((D_MODEL, D_MODEL), BF16),
            jax.ShapeDtypeStruct((D_MODEL, D_FF), BF16),
            jax.ShapeDtypeStruct((D_FF, D_MODEL), BF16),
        ],
        scratch_shapes=[
            pltpu.VMEM((D_INP, D_MODEL), BF16),
            pltpu.VMEM((Z_PAD, GLA_KEY), BF16),
        ],
        compiler_params=pltpu.CompilerParams(
            dimension_semantics=("arbitrary",), vmem_limit_bytes=VMEM_LIMIT),
        name="proj",
    )(x2, g1, w_in_t, w_gate_up, bg, wo, w1, w2)


def _gla_kernel(q_ref, k_ref, la_ref, v_ref, sg_ref, gn_ref, o_ref, st_ref, b_ref):
    c = GLA_SUB

    @pl.when(pl.program_id(0) == 0)
    def _():
        st_ref[...] = jnp.zeros_like(st_ref)

    ri = lax.broadcasted_iota(jnp.int32, (c, c), 0)
    ci = lax.broadcasted_iota(jnp.int32, (c, c), 1)
    first_head = lax.broadcasted_iota(jnp.int32, (c, LANES), 1) < GLA_DK
    st_row = lax.broadcasted_iota(jnp.int32, (2 * GLA_DV, LANES), 0)
    st_lane = lax.broadcasted_iota(jnp.int32, (2 * GLA_DV, LANES), 1)
    same_head = (st_row < GLA_DV) == (st_lane < GLA_DK)

    def masked_sum(m, rows):
        mb = m.astype(BF16)
        la_hi, la_lo = _split_bf16(la_ref[rows, :])
        return _dot(jnp.concatenate([mb, mb], axis=1), jnp.concatenate([la_hi, la_lo], axis=0))

    b_min = None
    for s in range(GLA_NSUB):
        rows = slice(s * c, (s + 1) * c)
        b_ref[rows, :] = masked_sum(ci <= ri, rows)
        total = jnp.sum(la_ref[rows, :], axis=0, keepdims=True)
        b_min = total if b_min is None else jnp.minimum(b_min, total)
    fast = jnp.min(b_min) > GLA_FAST_MIN_B

    def chunk_pre(rows, scores_fn):
        b = b_ref[rows, :]
        b_last = b[c - 1:c, :]
        q = q_ref[rows, :].astype(F32)
        k = k_ref[rows, :].astype(F32)
        q_in = (q * jnp.exp(b)).astype(BF16)
        k_out = (k * jnp.exp(b_last - b)).astype(BF16)
        intra, upds = [], []
        for p in range(GLA_HEADS // 2):
            kc = slice(p * LANES, (p + 1) * LANES)
            v_p = v_ref[rows, 2 * p * GLA_DV:2 * (p + 1) * GLA_DV]
            for j, s_h in enumerate(scores_fn(p, q, k, b, q_in)):
                intra.append(_dot(s_h.astype(BF16), v_p[:, j * GLA_DV:(j + 1) * GLA_DV]))
            upd = _dot_tn(v_p, k_out[:, kc])
            upds.append(jnp.where(same_head, upd, 0.0))
        return rows, q_in, jnp.exp(b_last), intra, upds

    def chunk_post(pre, states):
        rows, q_in, a_chunk, intra, upds = pre
        new_states = []
        for p in range(GLA_HEADS // 2):
            kc = slice(p * LANES, (p + 1) * LANES)
            st = states[p]
            o_inter = _dot_nt(q_in[:, kc], st.astype(BF16))
            for j in range(2):
                h = 2 * p + j
                o_h = intra[h] + o_inter[:, j * GLA_DV:(j + 1) * GLA_DV]
                ms = jnp.mean(o_h * o_h, axis=-1, keepdims=True)
                y = o_h * lax.rsqrt(ms + EPS) * gn_ref[...]
                cols = slice(h * GLA_DV, (h + 1) * GLA_DV)
                o_ref[rows, cols] = (y * sg_ref[rows, cols].astype(F32)).astype(BF16)
            new_states.append(a_chunk[:, kc] * st + upds[p])
        return new_states

    def fast_scores(p, q, k, b, q_in):
        kc = slice(p * LANES, (p + 1) * LANES)
        k_div = k[:, kc] * jnp.exp(-b[:, kc])
        k_blk = jnp.concatenate([jnp.where(first_head, k_div, 0.0),
                                 jnp.where(first_head, 0.0, k_div)], axis=0).astype(BF16)
        s2 = _dot_nt(q_in[:, kc], k_blk)
        return [jnp.where(ri >= ci, s2[:, j * c:(j + 1) * c], 0.0) for j in range(2)]

    def safe_scores(rows):
        q = q_ref[rows, :].astype(F32)
        k = k_ref[rows, :].astype(F32)
        qk = q * k
        scores = []
        for h in range(GLA_HEADS):
            dk = slice(h * GLA_DK, (h + 1) * GLA_DK)
            scores.append(jnp.where(ri == ci, jnp.sum(qk[:, dk], axis=-1, keepdims=True), 0.0))
        half = c // 2
        while half >= 1:
            ref = (ri & (-2 * half)) + (half - 1)
            lo = jnp.minimum(ri, ref)
            hi = jnp.maximum(ri, ref)
            e = jnp.exp(masked_sum((ci > lo) & (ci <= hi), rows))
            q_l = (q * e).astype(BF16)
            k_l = (k * e).astype(BF16)
            pair = (((ri ^ ci) & (-2 * half)) == 0) & ((ri & half) != 0) & ((ci & half) == 0)
            for h in range(GLA_HEADS):
                dk = slice(h * GLA_DK, (h + 1) * GLA_DK)
                scores[h] = jnp.where(pair, _dot_nt(q_l[:, dk], k_l[:, dk]), scores[h])
            half //= 2
        return scores

    @pl.when(fast)
    def _():
        pres = [chunk_pre(slice(s * c, (s + 1) * c), fast_scores) for s in range(GLA_NSUB)]
        states = [st_ref[p] for p in range(GLA_HEADS // 2)]
        for pre in pres:
            states = chunk_post(pre, states)
        for p in range(GLA_HEADS // 2):
            st_ref[p] = states[p]

    @pl.when(jnp.logical_not(fast))
    def _():
        def body(s, carry):
            rows = pl.ds(pl.multiple_of(s * c, c), c)
            scores = safe_scores(rows)
            pre = chunk_pre(rows, lambda p, *_: scores[2 * p:2 * p + 2])
            states = chunk_post(pre, [st_ref[p] for p in range(GLA_HEADS // 2)])
            for p in range(GLA_HEADS // 2):
                st_ref[p] = states[p]
            return carry

        lax.fori_loop(0, GLA_NSUB, body, 0)


def _gla_call(q, k, la, v, sg, gn):
    s = q.shape[0]
    n = s // TM_GLA
    row = lambda i: (i, 0)
    const = lambda i: (0, 0)
    return pl.pallas_call(
        _gla_kernel,
        grid=(n,),
        in_specs=[
            pl.BlockSpec((TM_GLA, GLA_KEY), row),
            pl.BlockSpec((TM_GLA, GLA_KEY), row),
            pl.BlockSpec((TM_GLA, GLA_KEY), row),
            pl.BlockSpec((TM_GLA, GLA_VAL), row),
            pl.BlockSpec((TM_GLA, GLA_VAL), row),
            pl.BlockSpec((1, GLA_DV), const),
        ],
        out_specs=pl.BlockSpec((TM_GLA, GLA_VAL), row),
        out_shape=jax.ShapeDtypeStruct((s, GLA_VAL), BF16),
        scratch_shapes=[
            pltpu.VMEM((GLA_HEADS // 2, 2 * GLA_DV, LANES), F32),
            pltpu.VMEM((TM_GLA, GLA_KEY), F32),
        ],
        compiler_params=pltpu.CompilerParams(
            dimension_semantics=("arbitrary",), vmem_limit_bytes=VMEM_LIMIT),
        name="gla",
    )(q, k, la, v, sg, gn)


def _group_mean(t, low_half, size):
    lo = jnp.sum(jnp.where(low_half, t, 0.0), axis=-1, keepdims=True)
    hi = jnp.sum(jnp.where(low_half, 0.0, t), axis=-1, keepdims=True)
    return jnp.where(low_half, lo, hi) * (1.0 / size)


def _interleave(mxu_items, vec_items):
    done = 0
    for i, item in enumerate(mxu_items):
        item()
        upto = (i + 1) * len(vec_items) // len(mxu_items)
        for v_item in vec_items[done:upto]:
            v_item()
        done = upto


def _mlp_conv_kernel(x_ref, og_ref, un_ref, u0_ref, cw_ref, cb_ref, gg_ref, gb_ref,
                     wo_hbm, g2_ref, w1_hbm, w2_hbm, gf_ref,
                     o_ref, ext_ref, sh_ref, oc_ref, f_ref, wo_ref, w1_ref, w2_ref, w_sem):
    tm = TM_MLP

    def weight_copies():
        return [pltpu.make_async_copy(src, dst, w_sem.at[j])
                for j, (src, dst) in enumerate(((wo_hbm, wo_ref), (w1_hbm, w1_ref), (w2_hbm, w2_ref)))]

    grp = CONV_CH // CONV_GROUPS
    low_half = lax.broadcasted_iota(jnp.int32, (CONV_RB, LANES), 1) < grp
    first = CONV_HALO - (CONV_W - 1)

    def shifted_copy(p):
        sh_ref[p - 1] = ext_ref[p:p + CONV_SH_ROWS, :]

    def conv_block(base):
        for cg in range(CONV_CH // LANES):
            cols = slice(cg * LANES, (cg + 1) * LANES)
            acc = jnp.broadcast_to(cb_ref[:, cols], (CONV_RB, LANES))
            for t in range(CONV_W):
                shift = first + t
                p = shift % SUBLANES
                rows = pl.ds(base + (shift - p), CONV_RB)
                tap = ext_ref[rows, cols] if p == 0 else sh_ref[p - 1, rows, cols]
                acc = acc + cw_ref[t, :, cols] * tap
            d = acc - _group_mean(acc, low_half, grp)
            var = _group_mean(d * d, low_half, grp)
            yn = d * lax.rsqrt(var + EPS) * gg_ref[:, cols] + gb_ref[:, cols]
            oc_ref[pl.ds(base, CONV_RB), cols] = (yn * _sigmoid(yn)).astype(BF16)

    @pl.when(pl.program_id(0) == 0)
    def _():
        for j, cp in enumerate(weight_copies()):
            cp.start(priority=j % 2)
        ext_ref[0:CONV_HALO, :] = jnp.zeros((CONV_HALO, CONV_CH), F32)
        ext_ref[CONV_HALO:, :] = u0_ref[...]
        for p in range(1, SUBLANES):
            shifted_copy(p)

        def body(r, carry):
            for k in range(CONV_PRO_BLOCKS):
                conv_block(pl.multiple_of((r * CONV_PRO_BLOCKS + k) * CONV_RB, CONV_RB))
            return carry

        lax.fori_loop(0, tm // (CONV_RB * CONV_PRO_BLOCKS), body, 0)
        for cp in weight_copies():
            cp.wait()

    h = x_ref[...] + _dot(og_ref[...], wo_ref[0:GLA_VAL, :]) + _dot(oc_ref[...], wo_ref[GLA_VAL:, :])
    hg = (h * g2_ref[...]).astype(BF16)
    r = lax.rsqrt(jnp.mean(h * h, axis=-1, keepdims=True) + EPS)
    r2 = r * r

    def roll_window():
        ext_ref[0:CONV_HALO, :] = ext_ref[tm:tm + CONV_HALO, :]
        ext_ref[CONV_HALO:, :] = un_ref[...]

    vec_items = [roll_window] + [functools.partial(shifted_copy, p) for p in range(1, SUBLANES)]
    vec_items += [functools.partial(conv_block, r * CONV_RB) for r in range(tm // CONV_RB)]

    def ff_in(j):
        cols = slice(j * FF_CHUNK, (j + 1) * FF_CHUNK)
        f = jnp.maximum(_dot(hg, w1_ref[:, cols]), 0.0)
        f_ref[:, cols] = (f * f * r2).astype(BF16)

    def ff_out(j):
        cols = slice(j * OUT_CHUNK, (j + 1) * OUT_CHUNK)
        o_ref[:, cols] = h[:, cols] + _dot(f_ref[...], w2_ref[:, cols])

    mxu_items = [functools.partial(ff_in, j) for j in range(D_FF // FF_CHUNK)]
    mxu_items += [functools.partial(ff_out, j) for j in range(D_MODEL // OUT_CHUNK)]
    _interleave(mxu_items, vec_items)

    acc = o_ref[...]
    ms2 = jnp.mean(acc * acc, axis=-1, keepdims=True)
    o_ref[...] = acc * lax.rsqrt(ms2 + EPS) * gf_ref[...]


def _mlp_conv_call(x2, og, u, cw, cb, gg, gb, wo, g2, w1, w2, gf):
    s = x2.shape[0]
    n = s // TM_MLP
    row = lambda i: (i, 0)
    nxt = lambda i: (jnp.minimum(i + 1, n - 1), 0)
    const = lambda i: (0, 0)
    single = pl.Buffered(1)
    return pl.pallas_call(
        _mlp_conv_kernel,
        grid=(n,),
        in_specs=[
            pl.BlockSpec((TM_MLP, D_MODEL), row),
            pl.BlockSpec((TM_MLP, GLA_VAL), row),
            pl.BlockSpec((TM_MLP, CONV_CH), nxt),
            pl.BlockSpec((TM_MLP, CONV_CH), const, pipeline_mode=single),
            pl.BlockSpec((CONV_W, 1, CONV_CH), lambda i: (0, 0, 0)),
            pl.BlockSpec((1, CONV_CH), const),
            pl.BlockSpec((1, CONV_CH), const),
            pl.BlockSpec((1, CONV_CH), const),
            pl.BlockSpec(memory_space=pl.ANY),
            pl.BlockSpec((1, D_MODEL), const),
            pl.BlockSpec(memory_space=pl.ANY),
            pl.BlockSpec(memory_space=pl.ANY),
            pl.BlockSpec((1, D_MODEL), const),
        ],
        out_specs=pl.BlockSpec((TM_MLP, D_MODEL), row),
        out_shape=jax.ShapeDtypeStruct((s, D_MODEL), F32),
        scratch_shapes=[
            pltpu.VMEM((TM_MLP + CONV_HALO, CONV_CH), F32),
            pltpu.VMEM((SUBLANES - 1, CONV_SH_ROWS, CONV_CH), F32),
            pltpu.VMEM((TM_MLP, CONV_CH), BF16),
            pltpu.VMEM((TM_MLP, D_FF), BF16),
            pltpu.VMEM((D_MODEL, D_MODEL), BF16),
            pltpu.VMEM((D_MODEL, D_FF), BF16),
            pltpu.VMEM((D_FF, D_MODEL), BF16),
            pltpu.SemaphoreType.DMA((3,)),
        ],
        compiler_params=pltpu.CompilerParams(
            dimension_semantics=("arbitrary",), vmem_limit_bytes=VMEM_LIMIT),
        name="mlp_conv",
    )(x2, og, u, u, cw, cb, gg, gb, wo, g2, w1, w2, gf)


def kernel(x, norm1_g, w_in, w_gate_up, b_gate, gla_norm_g, conv_w, conv_b, conv_norm_g,
           conv_norm_b, w_out, norm2_g, w_mlp_in, w_mlp_out, final_norm_g):
    bsz, seq, _ = x.shape
    x2 = x.reshape(bsz * seq, D_MODEL)
    assert bsz == 1, "state / halo carry across grid steps assumes one sequence"
    l = 0
    assert w_in.shape[0] == 1, "single layer: the projection weight is passed with its layer axis"
    w_in_t = jnp.swapaxes(w_in, 1, 2)
    q, k, la, v, sg, u, wo_b, w1_b, w2_b = _proj_call(
        x2, norm1_g[l][None, :], w_in_t, w_gate_up[l], b_gate[l][None, :], w_out[l], w_mlp_in[l],
        w_mlp_out[l])
    o_gla = _gla_call(q, k, la, v, sg, gla_norm_g[l][None, :])
    out = _mlp_conv_call(x2, o_gla, u, jnp.swapaxes(conv_w, 0, 1), conv_b[l][None, :], conv_norm_g[l][None, :],
                         conv_norm_b[l][None, :], wo_b, norm2_g[l][None, :], w1_b, w2_b,
                         final_norm_g[None, :])
    return out.reshape(bsz, seq, D_MODEL)
```

```python
import functools

import jax
import jax.numpy as jnp
from jax import lax
from jax.experimental import pallas as pl
from jax.experimental.pallas import tpu as pltpu

D_MODEL = 1024
GLA_HEADS = 4
GLA_DV = 128
GLA_DK = 64
GLA_KEY = GLA_HEADS * GLA_DK
GLA_VAL = GLA_HEADS * GLA_DV
GATE_RANK = 16
GATE_TAU = 16.0
CONV_CH = 512
CONV_GROUPS = 8
CONV_W = 31
D_FF = 4 * D_MODEL
EPS = 1e-6

OFF_Q = 0
OFF_K = OFF_Q + GLA_KEY
OFF_V = OFF_K + GLA_KEY
OFF_G = OFF_V + GLA_VAL
OFF_A = OFF_G + GLA_VAL
OFF_C = OFF_A + GATE_RANK
D_IN = OFF_C + 2 * CONV_CH

LANES = 128
SUBLANES = 8
Z_PAD = LANES
P_Z = 0
P_VA = P_Z + Z_PAD
P_CG = P_VA + LANES
P_CI = P_CG + CONV_CH
P_SPLIT = P_CI + CONV_CH
P_Q = P_SPLIT
P_G = P_Q + GLA_KEY
P_K = P_G + GLA_VAL
P_VB = P_K + GLA_KEY
D_INP = P_VB + GLA_VAL - LANES

TM_PROJ = 1024
GLA_SUB = 256
GLA_NSUB = 8
TM_GLA = GLA_SUB * GLA_NSUB
TM_MLP = 512
FF_CHUNK = 512
OUT_CHUNK = 256
CONV_HALO = 32
CONV_RB = 32
CONV_PRO_BLOCKS = 4
CONV_SH_ROWS = TM_MLP + CONV_HALO - SUBLANES
GLA_FAST_MIN_B = -60.0

V7X_VMEM_BYTES = 64 * 1024 * 1024
VMEM_LIMIT = V7X_VMEM_BYTES - 8 * 1024 * 1024

F32 = jnp.float32
BF16 = jnp.bfloat16


def _sigmoid(x):
    return 1.0 / (1.0 + jnp.exp(-x))


def _split_bf16(x):
    hi = x.astype(BF16)
    lo = (x - hi.astype(F32)).astype(BF16)
    return hi, lo


def _dot(a, b):
    return jnp.dot(a, b, preferred_element_type=F32)


def _dot_nt(a, b):
    return lax.dot_general(a, b, (((1,), (1,)), ((), ())), preferred_element_type=F32)


def _dot_tn(a, b):
    return lax.dot_general(a, b, (((0,), (0,)), ((), ())), preferred_element_type=F32)


def _proj_kernel(x_ref, g1_ref, w_hbm, wg_ref, bg_ref, wo_ref, w1_ref, w2_ref,
                 q_ref, k_ref, la_ref, v_ref, sg_ref, u_ref, wob_ref, w1b_ref, w2b_ref,
                 wb_ref, wgb_ref, wf_ref, w_sem):
    head_rows = ((OFF_A, GATE_RANK), (OFF_V, LANES), (OFF_C, 2 * CONV_CH))
    rest_rows = ((OFF_Q, 2 * GLA_KEY), (OFF_V + LANES, GLA_VAL - LANES + GLA_VAL))

    def weight_copies(ranges, first_sem):
        return [pltpu.make_async_copy(w_hbm.at[0, pl.ds(lo, n), :], wf_ref.at[pl.ds(lo, n), :],
                                      w_sem.at[first_sem + j]) for j, (lo, n) in enumerate(ranges)]

    def put(dst, src, size, scale=None):
        w = wf_ref[src:src + size, :] * g1_ref[...]
        wb_ref[dst:dst + size, :] = (w if scale is None else w * scale).astype(BF16)

    def body(first_step):
        if first_step:
            for cp in weight_copies(head_rows, 0) + weight_copies(rest_rows, len(head_rows)):
                cp.start()
        x = x_ref[...]
        r = lax.rsqrt(jnp.mean(x * x, axis=-1, keepdims=True) + EPS)
        xb = x.astype(BF16)
        if first_step:
            for cp in weight_copies(head_rows, 0):
                cp.wait()
            put(P_Z, OFF_A, GATE_RANK)
            wb_ref[P_Z + GATE_RANK:P_Z + Z_PAD, :] = jnp.zeros((Z_PAD - GATE_RANK, D_MODEL), BF16)
            put(P_VA, OFF_V, LANES)
            put(P_CG, OFF_C + CONV_CH, CONV_CH)
            put(P_CI, OFF_C, CONV_CH)
            wgb_ref[...] = jnp.zeros_like(wgb_ref)
            wgb_ref[0:GATE_RANK, :] = wg_ref[...].astype(BF16)
        head = _dot_nt(xb, wb_ref[0:P_SPLIT, :]) * r
        z = head[:, P_Z:P_Z + Z_PAD].astype(BF16)
        a_logit = _dot(z, wgb_ref[...]) + bg_ref[...]
        if first_step:
            for cp in weight_copies(rest_rows, len(head_rows)):
                cp.wait()
            put(P_Q, OFF_Q, GLA_KEY, GLA_DK ** -0.5)
            put(P_K, OFF_K, GLA_KEY)
            put(P_VB, OFF_V + LANES, GLA_VAL - LANES)
            put(P_G, OFF_G, GLA_VAL)
        rest = _dot_nt(xb, wb_ref[P_SPLIT:D_INP, :]) * r
        u_ref[...] = head[:, P_CI:P_CI + CONV_CH] * _sigmoid(head[:, P_CG:P_CG + CONV_CH])
        v_ref[:, 0:LANES] = head[:, P_VA:P_VA + LANES].astype(BF16)
        g = rest[:, P_G - P_SPLIT:P_G - P_SPLIT + GLA_VAL]
        sg_ref[...] = (g * _sigmoid(g)).astype(BF16)
        q_ref[...] = rest[:, P_Q - P_SPLIT:P_Q - P_SPLIT + GLA_KEY].astype(BF16)
        k_ref[...] = rest[:, P_K - P_SPLIT:P_K - P_SPLIT + GLA_KEY].astype(BF16)
        v_ref[:, LANES:GLA_VAL] = rest[:, P_VB - P_SPLIT:D_INP - P_SPLIT].astype(BF16)
        wob_ref[...] = wo_ref[...].astype(BF16)
        w1b_ref[...] = w1_ref[...].astype(BF16)
        w2b_ref[...] = w2_ref[...].astype(BF16)
        la = jnp.minimum(a_logit, 0.0) - jnp.log(1.0 + jnp.exp(-jnp.abs(a_logit)))
        la_ref[...] = la * (1.0 / GATE_TAU)

    pl.when(pl.program_id(0) == 0)(functools.partial(body, True))
    pl.when(pl.program_id(0) != 0)(functools.partial(body, False))


def _proj_call(x2, g1, w_in_t, w_gate_up, bg, wo, w1, w2):
    s = x2.shape[0]
    n = s // TM_PROJ
    row = lambda i: (i, 0)
    col = lambda i: (0, i)
    const = lambda i: (0, 0)
    return pl.pallas_call(
        _proj_kernel,
        grid=(n,),
        in_specs=[
            pl.BlockSpec((TM_PROJ, D_MODEL), row),
            pl.BlockSpec((1, D_MODEL), const),
            pl.BlockSpec(memory_space=pl.ANY),
            pl.BlockSpec((GATE_RANK, GLA_KEY), const),
            pl.BlockSpec((1, GLA_KEY), const),
            pl.BlockSpec((D_MODEL // n, D_MODEL), row),
            pl.BlockSpec((D_MODEL, D_FF // n), col),
            pl.BlockSpec((D_FF // n, D_MODEL), row),
        ],
        out_specs=[
            pl.BlockSpec((TM_PROJ, GLA_KEY), row),
            pl.BlockSpec((TM_PROJ, GLA_KEY), row),
            pl.BlockSpec((TM_PROJ, GLA_KEY), row),
            pl.BlockSpec((TM_PROJ, GLA_VAL), row),
            pl.BlockSpec((TM_PROJ, GLA_VAL), row),
            pl.BlockSpec((TM_PROJ, CONV_CH), row),
            pl.BlockSpec((D_MODEL // n, D_MODEL), row),
            pl.BlockSpec((D_MODEL, D_FF // n), col),
            pl.BlockSpec((D_FF // n, D_MODEL), row),
        ],
        out_shape=[
            jax.ShapeDtypeStruct((s, GLA_KEY), BF16),
            jax.ShapeDtypeStruct((s, GLA_KEY), BF16),
            jax.ShapeDtypeStruct((s, GLA_KEY), F32),
            jax.ShapeDtypeStruct((s, GLA_VAL), BF16),
            jax.ShapeDtypeStruct((s, GLA_VAL), BF16),
            jax.ShapeDtypeStruct((s, CONV_CH), F32),
            jax.ShapeDtypeStruct((D_MODEL, D_MODEL), BF16),
            jax.ShapeDtypeStruct((D_MODEL, D_FF), BF16),
            jax.ShapeDtypeStruct((D_FF, D_MODEL), BF16),
        ],
        scratch_shapes=[
            pltpu.VMEM((D_INP, D_MODEL), BF16),
            pltpu.VMEM((Z_PAD, GLA_KEY), BF16),
            pltpu.VMEM((D_IN, D_MODEL), F32),
            pltpu.SemaphoreType.DMA((5,)),
        ],
        compiler_params=pltpu.CompilerParams(
            dimension_semantics=("arbitrary",), vmem_limit_bytes=VMEM_LIMIT),
        name="proj",
    )(x2, g1, w_in_t, w_gate_up, bg, wo, w1, w2)


def _gla_kernel(q_ref, k_ref, la_ref, v_ref, sg_ref, gn_ref, o_ref, st_ref, b_ref):
    c = GLA_SUB

    @pl.when(pl.program_id(0) == 0)
    def _():
        st_ref[...] = jnp.zeros_like(st_ref)

    ri = lax.broadcasted_iota(jnp.int32, (c, c), 0)
    ci = lax.broadcasted_iota(jnp.int32, (c, c), 1)
    first_head = lax.broadcasted_iota(jnp.int32, (c, LANES), 1) < GLA_DK
    st_row = lax.broadcasted_iota(jnp.int32, (2 * GLA_DV, LANES), 0)
    st_lane = lax.broadcasted_iota(jnp.int32, (2 * GLA_DV, LANES), 1)
    same_head = (st_row < GLA_DV) == (st_lane < GLA_DK)

    def masked_sum(m, rows):
        mb = m.astype(BF16)
        la_hi, la_lo = _split_bf16(la_ref[rows, :])
        return _dot(jnp.concatenate([mb, mb], axis=1), jnp.concatenate([la_hi, la_lo], axis=0))

    b_min = None
    for s in range(GLA_NSUB):
        rows = slice(s * c, (s + 1) * c)
        b_ref[rows, :] = masked_sum(ci <= ri, rows)
        total = jnp.sum(la_ref[rows, :], axis=0, keepdims=True)
        b_min = total if b_min is None else jnp.minimum(b_min, total)
    fast = jnp.min(b_min) > GLA_FAST_MIN_B

    def chunk_pre(rows, scores_fn):
        b = b_ref[rows, :]
        b_last = b[c - 1:c, :]
        q = q_ref[rows, :].astype(F32)
        k = k_ref[rows, :].astype(F32)
        q_in = (q * jnp.exp(b)).astype(BF16)
        k_out = (k * jnp.exp(b_last - b)).astype(BF16)
        intra, upds = [], []
        for p in range(GLA_HEADS // 2):
            kc = slice(p * LANES, (p + 1) * LANES)
            v_p = v_ref[rows, 2 * p * GLA_DV:2 * (p + 1) * GLA_DV]
            for j, s_h in enumerate(scores_fn(p, q, k, b, q_in)):
                intra.append(_dot(s_h.astype(BF16), v_p[:, j * GLA_DV:(j + 1) * GLA_DV]))
            upd = _dot_tn(v_p, k_out[:, kc])
            upds.append(jnp.where(same_head, upd, 0.0))
        return rows, q_in, jnp.exp(b_last), intra, upds

    def chunk_post(pre, states):
        rows, q_in, a_chunk, intra, upds = pre
        new_states = []
        for p in range(GLA_HEADS // 2):
            kc = slice(p * LANES, (p + 1) * LANES)
            st = states[p]
            o_inter = _dot_nt(q_in[:, kc], st.astype(BF16))
            for j in range(2):
                h = 2 * p + j
                o_h = intra[h] + o_inter[:, j * GLA_DV:(j + 1) * GLA_DV]
                ms = jnp.mean(o_h * o_h, axis=-1, keepdims=True)
                y = o_h * lax.rsqrt(ms + EPS) * gn_ref[...]
                cols = slice(h * GLA_DV, (h + 1) * GLA_DV)
                o_ref[rows, cols] = (y * sg_ref[rows, cols].astype(F32)).astype(BF16)
            new_states.append(a_chunk[:, kc] * st + upds[p])
        return new_states

    def fast_scores(p, q, k, b, q_in):
        kc = slice(p * LANES, (p + 1) * LANES)
        k_div = k[:, kc] * jnp.exp(-b[:, kc])
        k_blk = jnp.concatenate([jnp.where(first_head, k_div, 0.0),
                                 jnp.where(first_head, 0.0, k_div)], axis=0).astype(BF16)
        s2 = _dot_nt(q_in[:, kc], k_blk)
        return [jnp.where(ri >= ci, s2[:, j * c:(j + 1) * c], 0.0) for j in range(2)]

    def safe_scores(rows):
        q = q_ref[rows, :].astype(F32)
        k = k_ref[rows, :].astype(F32)
        qk = q * k
        scores = []
        for h in range(GLA_HEADS):
            dk = slice(h * GLA_DK, (h + 1) * GLA_DK)
            scores.append(jnp.where(ri == ci, jnp.sum(qk[:, dk], axis=-1, keepdims=True), 0.0))
        half = c // 2
        while half >= 1:
            ref = (ri & (-2 * half)) + (half - 1)
            lo = jnp.minimum(ri, ref)
            hi = jnp.maximum(ri, ref)
            e = jnp.exp(masked_sum((ci > lo) & (ci <= hi), rows))
            q_l = (q * e).astype(BF16)
            k_l = (k * e).astype(BF16)
            pair = (((ri ^ ci) & (-2 * half)) == 0) & ((ri & half) != 0) & ((ci & half) == 0)
            for h in range(GLA_HEADS):
                dk = slice(h * GLA_DK, (h + 1) * GLA_DK)
                scores[h] = jnp.where(pair, _dot_nt(q_l[:, dk], k_l[:, dk]), scores[h])
            half //= 2
        return scores

    @pl.when(fast)
    def _():
        pres = [chunk_pre(slice(s * c, (s + 1) * c), fast_scores) for s in range(GLA_NSUB)]
        states = [st_ref[p] for p in range(GLA_HEADS // 2)]
        for pre in pres:
            states = chunk_post(pre, states)
        for p in range(GLA_HEADS // 2):
            st_ref[p] = states[p]

    @pl.when(jnp.logical_not(fast))
    def _():
        def body(s, carry):
            rows = pl.ds(pl.multiple_of(s * c, c), c)
            scores = safe_scores(rows)
            pre = chunk_pre(rows, lambda p, *_: scores[2 * p:2 * p + 2])
            states = chunk_post(pre, [st_ref[p] for p in range(GLA_HEADS // 2)])
            for p in range(GLA_HEADS // 2):
                st_ref[p] = states[p]
            return carry

        lax.fori_loop(0, GLA_NSUB, body, 0)


def _gla_call(q, k, la, v, sg, gn):
    s = q.shape[0]
    n = s // TM_GLA
    row = lambda i: (i, 0)
    const = lambda i: (0, 0)
    return pl.pallas_call(
        _gla_kernel,
        grid=(n,),
        in_specs=[
            pl.BlockSpec((TM_GLA, GLA_KEY), row),
            pl.BlockSpec((TM_GLA, GLA_KEY), row),
            pl.BlockSpec((TM_GLA, GLA_KEY), row),
            pl.BlockSpec((TM_GLA, GLA_VAL), row),
            pl.BlockSpec((TM_GLA, GLA_VAL), row),
            pl.BlockSpec((1, GLA_DV), const),
        ],
        out_specs=pl.BlockSpec((TM_GLA, GLA_VAL), row),
        out_shape=jax.ShapeDtypeStruct((s, GLA_VAL), BF16),
        scratch_shapes=[
            pltpu.VMEM((GLA_HEADS // 2, 2 * GLA_DV, LANES), F32),
            pltpu.VMEM((TM_GLA, GLA_KEY), F32),
        ],
        compiler_params=pltpu.CompilerParams(
            dimension_semantics=("arbitrary",), vmem_limit_bytes=VMEM_LIMIT),
        name="gla",
    )(q, k, la, v, sg, gn)


def _group_mean(t, low_half, size):
    lo = jnp.sum(jnp.where(low_half, t, 0.0), axis=-1, keepdims=True)
    hi = jnp.sum(jnp.where(low_half, 0.0, t), axis=-1, keepdims=True)
    return jnp.where(low_half, lo, hi) * (1.0 / size)


def _interleave(mxu_items, vec_items):
    done = 0
    for i, item in enumerate(mxu_items):
        item()
        upto = (i + 1) * len(vec_items) // len(mxu_items)
        for v_item in vec_items[done:upto]:
            v_item()
        done = upto


def _mlp_conv_kernel(x_ref, og_ref, un_ref, u0_ref, cw_ref, cb_ref, gg_ref, gb_ref,
                     wo_hbm, g2_ref, w1_hbm, w2_hbm, gf_ref,
                     o_ref, ext_ref, sh_ref, oc_ref, f_ref, wo_ref, w1_ref, w2_ref, w_sem):
    tm = TM_MLP

    def weight_copies():
        return [pltpu.make_async_copy(src, dst, w_sem.at[j])
                for j, (src, dst) in enumerate(((wo_hbm, wo_ref), (w1_hbm, w1_ref), (w2_hbm, w2_ref)))]

    grp = CONV_CH // CONV_GROUPS
    low_half = lax.broadcasted_iota(jnp.int32, (CONV_RB, LANES), 1) < grp
    first = CONV_HALO - (CONV_W - 1)

    def shifted_copy(p):
        sh_ref[p - 1] = ext_ref[p:p + CONV_SH_ROWS, :]

    def conv_block(base):
        for cg in range(CONV_CH // LANES):
            cols = slice(cg * LANES, (cg + 1) * LANES)
            acc = jnp.broadcast_to(cb_ref[:, cols], (CONV_RB, LANES))
            for t in range(CONV_W):
                shift = first + t
                p = shift % SUBLANES
                rows = pl.ds(base + (shift - p), CONV_RB)
                tap = ext_ref[rows, cols] if p == 0 else sh_ref[p - 1, rows, cols]
                acc = acc + cw_ref[t, :, cols] * tap
            d = acc - _group_mean(acc, low_half, grp)
            var = _group_mean(d * d, low_half, grp)
            yn = d * lax.rsqrt(var + EPS) * gg_ref[:, cols] + gb_ref[:, cols]
            oc_ref[pl.ds(base, CONV_RB), cols] = (yn * _sigmoid(yn)).astype(BF16)

    @pl.when(pl.program_id(0) == 0)
    def _():
        for j, cp in enumerate(weight_copies()):
            cp.start(priority=j % 2)
        ext_ref[0:CONV_HALO, :] = jnp.zeros((CONV_HALO, CONV_CH), F32)
        ext_ref[CONV_HALO:, :] = u0_ref[...]
        for p in range(1, SUBLANES):
            shifted_copy(p)

        def body(r, carry):
            for k in range(CONV_PRO_BLOCKS):
                conv_block(pl.multiple_of((r * CONV_PRO_BLOCKS + k) * CONV_RB, CONV_RB))
            return carry

        lax.fori_loop(0, tm // (CONV_RB * CONV_PRO_BLOCKS), body, 0)
        for cp in weight_copies():
            cp.wait()

    h = x_ref[...] + _dot(og_ref[...], wo_ref[0:GLA_VAL, :]) + _dot(oc_ref[...], wo_ref[GLA_VAL:, :])
    hg = (h * g2_ref[...]).astype(BF16)
    r = lax.rsqrt(jnp.mean(h * h, axis=-1, keepdims=True) + EPS)
    r2 = r * r

    def roll_window():
        ext_ref[0:CONV_HALO, :] = ext_ref[tm:tm + CONV_HALO, :]
        ext_ref[CONV_HALO:, :] = un_ref[...]

    vec_items = [roll_window] + [functools.partial(shifted_copy, p) for p in range(1, SUBLANES)]
    vec_items += [functools.partial(conv_block, r * CONV_RB) for r in range(tm // CONV_RB)]

    def ff_in(j):
        cols = slice(j * FF_CHUNK, (j + 1) * FF_CHUNK)
        f = jnp.maximum(_dot(hg, w1_ref[:, cols]), 0.0)
        f_ref[:, cols] = (f * f * r2).astype(BF16)

    def ff_out(j):
        cols = slice(j * OUT_CHUNK, (j + 1) * OUT_CHUNK)
        o_ref[:, cols] = h[:, cols] + _dot(f_ref[...], w2_ref[:, cols])

    mxu_items = [functools.partial(ff_in, j) for j in range(D_FF // FF_CHUNK)]
    mxu_items += [functools.partial(ff_out, j) for j in range(D_MODEL // OUT_CHUNK)]
    _interleave(mxu_items, vec_items)

    acc = o_ref[...]
    ms2 = jnp.mean(acc * acc, axis=-1, keepdims=True)
    o_ref[...] = acc * lax.rsqrt(ms2 + EPS) * gf_ref[...]


def _mlp_conv_call(x2, og, u, cw, cb, gg, gb, wo, g2, w1, w2, gf):
    s = x2.shape[0]
    n = s // TM_MLP
    row = lambda i: (i, 0)
    nxt = lambda i: (jnp.minimum(i + 1, n - 1), 0)
    const = lambda i: (0, 0)
    single = pl.Buffered(1)
    return pl.pallas_call(
        _mlp_conv_kernel,
        grid=(n,),
        in_specs=[
            pl.BlockSpec((TM_MLP, D_MODEL), row),
            pl.BlockSpec((TM_MLP, GLA_VAL), row),
            pl.BlockSpec((TM_MLP, CONV_CH), nxt),
            pl.BlockSpec((TM_MLP, CONV_CH), const, pipeline_mode=single),
            pl.BlockSpec((CONV_W, 1, CONV_CH), lambda i: (0, 0, 0)),
            pl.BlockSpec((1, CONV_CH), const),
            pl.BlockSpec((1, CONV_CH), const),
            pl.BlockSpec((1, CONV_CH), const),
            pl.BlockSpec(memory_space=pl.ANY),
            pl.BlockSpec((1, D_MODEL), const),
            pl.BlockSpec(memory_space=pl.ANY),
            pl.BlockSpec(memory_space=pl.ANY),
            pl.BlockSpec((1, D_MODEL), const),
        ],
        out_specs=pl.BlockSpec((TM_MLP, D_MODEL), row),
        out_shape=jax.ShapeDtypeStruct((s, D_MODEL), F32),
        scratch_shapes=[
            pltpu.VMEM((TM_MLP + CONV_HALO, CONV_CH), F32),
            pltpu.VMEM((SUBLANES - 1, CONV_SH_ROWS, CONV_CH), F32),
            pltpu.VMEM((TM_MLP, CONV_CH), BF16),
            pltpu.VMEM((TM_MLP, D_FF), BF16),
            pltpu.VMEM((D_MODEL, D_MODEL), BF16),
            pltpu.VMEM((D_MODEL, D_FF), BF16),
            pltpu.VMEM((D_FF, D_MODEL), BF16),
            pltpu.SemaphoreType.DMA((3,)),
        ],
        compiler_params=pltpu.CompilerParams(
            dimension_semantics=("arbitrary",), vmem_limit_bytes=VMEM_LIMIT),
        name="mlp_conv",
    )(x2, og, u, u, cw, cb, gg, gb, wo, g2, w1, w2, gf)


def kernel(x, norm1_g, w_in, w_gate_up, b_gate, gla_norm_g, conv_w, conv_b, conv_norm_g,
           conv_norm_b, w_out, norm2_g, w_mlp_in, w_mlp_out, final_norm_g):
    bsz, seq, _ = x.shape
    x2 = x.reshape(bsz * seq, D_MODEL)
    assert bsz == 1, "state / halo carry across grid steps assumes one sequence"
    l = 0
    assert w_in.shape[0] == 1, "single layer: the projection weight is passed with its layer axis"
    w_in_t = jnp.swapaxes(w_in, 1, 2)
    q, k, la, v, sg, u, wo_b, w1_b, w2_b = _proj_call(
        x2, norm1_g[l][None, :], w_in_t, w_gate_up[l], b_gate[l][None, :], w_out[l], w_mlp_in[l],
        w_mlp_out[l])
    o_gla = _gla_call(q, k, la, v, sg, gla_norm_g[l][None, :])
    out = _mlp_conv_call(x2, o_gla, u, jnp.swapaxes(conv_w, 0, 1), conv_b[l][None, :], conv_norm_g[l][None, :],
                         conv_norm_b[l][None, :], wo_b, norm2_g[l][None, :], w1_b, w2_b,
                         final_norm_g[None, :])
    return out.reshape(bsz, seq, D_MODEL)
```

```python
import functools

import jax
import jax.numpy as jnp
from jax import lax
from jax.experimental import pallas as pl
from jax.experimental.pallas import tpu as pltpu

D_MODEL = 1024
GLA_HEADS = 4
GLA_DV = 128
GLA_DK = 64
GLA_KEY = GLA_HEADS * GLA_DK
GLA_VAL = GLA_HEADS * GLA_DV
GATE_RANK = 16
GATE_TAU = 16.0
CONV_CH = 512
CONV_GROUPS = 8
CONV_W = 31
D_FF = 4 * D_MODEL
EPS = 1e-6

OFF_Q = 0
OFF_K = OFF_Q + GLA_KEY
OFF_V = OFF_K + GLA_KEY
OFF_G = OFF_V + GLA_VAL
OFF_A = OFF_G + GLA_VAL
OFF_C = OFF_A + GATE_RANK
D_IN = OFF_C + 2 * CONV_CH

LANES = 128
SUBLANES = 8
Z_PAD = LANES
P_Z = 0
P_VA = P_Z + Z_PAD
P_CG = P_VA + LANES
P_CI = P_CG + CONV_CH
P_SPLIT = P_CI + CONV_CH
P_Q = P_SPLIT
P_G = P_Q + GLA_KEY
P_K = P_G + GLA_VAL
P_VB = P_K + GLA_KEY
D_INP = P_VB + GLA_VAL - LANES

TM_PROJ = 1024
GLA_SUB = 256
GLA_NSUB = 8
TM_GLA = GLA_SUB * GLA_NSUB
TM_MLP = 512
FF_CHUNK = 512
OUT_CHUNK = 256
CONV_HALO = 32
CONV_RB = 32
CONV_PRO_BLOCKS = 4
CONV_SH_ROWS = TM_MLP + CONV_HALO - SUBLANES
GLA_FAST_MIN_B = -60.0

V7X_VMEM_BYTES = 64 * 1024 * 1024
VMEM_LIMIT = V7X_VMEM_BYTES - 8 * 1024 * 1024

F32 = jnp.float32
BF16 = jnp.bfloat16


def _sigmoid(x):
    return 1.0 / (1.0 + jnp.exp(-x))


def _split_bf16(x):
    hi = x.astype(BF16)
    lo = (x - hi.astype(F32)).astype(BF16)
    return hi, lo


def _dot(a, b):
    return jnp.dot(a, b, preferred_element_type=F32)


def _dot_nt(a, b):
    return lax.dot_general(a, b, (((1,), (1,)), ((), ())), preferred_element_type=F32)


def _dot_tn(a, b):
    return lax.dot_general(a, b, (((0,), (0,)), ((), ())), preferred_element_type=F32)


def _proj_kernel(x_ref, g1_ref, w_ref, wg_ref, bg_ref, wo_ref, w1_ref, w2_ref,
                 q_ref, k_ref, la_ref, v_ref, sg_ref, u_ref, wob_ref, w1b_ref, w2b_ref,
                 wb_ref, wgb_ref):
    @pl.when(pl.program_id(0) == 0)
    def _():
        def put(dst, src, size, scale=None):
            w = w_ref[src:src + size, :] * g1_ref[...]
            wb_ref[dst:dst + size, :] = (w if scale is None else w * scale).astype(BF16)

        put(P_Q, OFF_Q, GLA_KEY, GLA_DK ** -0.5)
        put(P_K, OFF_K, GLA_KEY)
        put(P_VA, OFF_V, LANES)
        put(P_VB, OFF_V + LANES, GLA_VAL - LANES)
        put(P_G, OFF_G, GLA_VAL)
        put(P_CI, OFF_C, CONV_CH)
        put(P_CG, OFF_C + CONV_CH, CONV_CH)
        put(P_Z, OFF_A, GATE_RANK)
        wb_ref[P_Z + GATE_RANK:P_Z + Z_PAD, :] = jnp.zeros((Z_PAD - GATE_RANK, D_MODEL), BF16)
        wgb_ref[...] = jnp.zeros_like(wgb_ref)
        wgb_ref[0:GATE_RANK, :] = wg_ref[...].astype(BF16)

    x = x_ref[...]
    r = lax.rsqrt(jnp.mean(x * x, axis=-1, keepdims=True) + EPS)
    xb = x.astype(BF16)
    head = _dot_nt(xb, wb_ref[0:P_SPLIT, :]) * r
    z = head[:, P_Z:P_Z + Z_PAD].astype(BF16)
    a_logit = _dot(z, wgb_ref[...]) + bg_ref[...]
    rest = _dot_nt(xb, wb_ref[P_SPLIT:D_INP, :]) * r
    u_ref[...] = head[:, P_CI:P_CI + CONV_CH] * _sigmoid(head[:, P_CG:P_CG + CONV_CH])
    v_ref[:, 0:LANES] = head[:, P_VA:P_VA + LANES].astype(BF16)
    g = rest[:, P_G - P_SPLIT:P_G - P_SPLIT + GLA_VAL]
    sg_ref[...] = (g * _sigmoid(g)).astype(BF16)
    q_ref[...] = rest[:, P_Q - P_SPLIT:P_Q - P_SPLIT + GLA_KEY].astype(BF16)
    k_ref[...] = rest[:, P_K - P_SPLIT:P_K - P_SPLIT + GLA_KEY].astype(BF16)
    v_ref[:, LANES:GLA_VAL] = rest[:, P_VB - P_SPLIT:D_INP - P_SPLIT].astype(BF16)
    wob_ref[...] = wo_ref[...].astype(BF16)
    w1b_ref[...] = w1_ref[...].astype(BF16)
    w2b_ref[...] = w2_ref[...].astype(BF16)
    la = jnp.minimum(a_logit, 0.0) - jnp.log(1.0 + jnp.exp(-jnp.abs(a_logit)))
    la_ref[...] = la * (1.0 / GATE_TAU)


def _proj_call(x2, g1, w_in_t, w_gate_up, bg, wo, w1, w2):
    s = x2.shape[0]
    n = s // TM_PROJ
    row = lambda i: (i, 0)
    col = lambda i: (0, i)
    const = lambda i: (0, 0)
    return pl.pallas_call(
        _proj_kernel,
        grid=(n,),
        in_specs=[
            pl.BlockSpec((TM_PROJ, D_MODEL), row),
            pl.BlockSpec((1, D_MODEL), const),
            pl.BlockSpec((None, D_IN, D_MODEL), lambda i: (0, 0, 0), pipeline_mode=pl.Buffered(1)),
            pl.BlockSpec((GATE_RANK, GLA_KEY), const),
            pl.BlockSpec((1, GLA_KEY), const),
            pl.BlockSpec((D_MODEL // n, D_MODEL), row),
            pl.BlockSpec((D_MODEL, D_FF // n), col),
            pl.BlockSpec((D_FF // n, D_MODEL), row),
        ],
        out_specs=[
            pl.BlockSpec((TM_PROJ, GLA_KEY), row),
            pl.BlockSpec((TM_PROJ, GLA_KEY), row),
            pl.BlockSpec((TM_PROJ, GLA_KEY), row),
            pl.BlockSpec((TM_PROJ, GLA_VAL), row),
            pl.BlockSpec((TM_PROJ, GLA_VAL), row),
            pl.BlockSpec((TM_PROJ, CONV_CH), row),
            pl.BlockSpec((D_MODEL // n, D_MODEL), row),
            pl.BlockSpec((D_MODEL, D_FF // n), col),
            pl.BlockSpec((D_FF // n, D_MODEL), row),
        ],
        out_shape=[
            jax.ShapeDtypeStruct((s, GLA_KEY), BF16),
            jax.ShapeDtypeStruct((s, GLA_KEY), BF16),
            jax.ShapeDtypeStruct((s, GLA_KEY), F32),
            jax.ShapeDtypeStruct((s, GLA_VAL), BF16),
            jax.ShapeDtypeStruct((s, GLA_VAL), BF16),
            jax.ShapeDtypeStruct((s, CONV_CH), F32),
            jax.ShapeDtypeStruct((D_MODEL, D_MODEL), BF16),
            jax.ShapeDtypeStruct((D_MODEL, D_FF), BF16),
            jax.ShapeDtypeStruct((D_FF, D_MODEL), BF16),
        ],
        scratch_shapes=[
            pltpu.VMEM((D_INP, D_MODEL), BF16),
            pltpu.VMEM((Z_PAD, GLA_KEY), BF16),
        ],
        compiler_params=pltpu.CompilerParams(
            dimension_semantics=("arbitrary",), vmem_limit_bytes=VMEM_LIMIT),
        name="proj",
    )(x2, g1, w_in_t, w_gate_up, bg, wo, w1, w2)


def _gla_kernel(q_ref, k_ref, la_ref, v_ref, sg_ref, gn_ref, o_ref, st_ref, b_ref):
    c = GLA_SUB

    @pl.when(pl.program_id(0) == 0)
    def _():
        st_ref[...] = jnp.zeros_like(st_ref)

    ri = lax.broadcasted_iota(jnp.int32, (c, c), 0)
    ci = lax.broadcasted_iota(jnp.int32, (c, c), 1)
    first_head = lax.broadcasted_iota(jnp.int32, (c, LANES), 1) < GLA_DK
    st_row = lax.broadcasted_iota(jnp.int32, (2 * GLA_DV, LANES), 0)
    st_lane = lax.broadcasted_iota(jnp.int32, (2 * GLA_DV, LANES), 1)
    same_head = (st_row < GLA_DV) == (st_lane < GLA_DK)

    def masked_sum(m, rows):
        mb = m.astype(BF16)
        la_hi, la_lo = _split_bf16(la_ref[rows, :])
        return _dot(jnp.concatenate([mb, mb], axis=1), jnp.concatenate([la_hi, la_lo], axis=0))

    b_min = None
    for s in range(GLA_NSUB):
        rows = slice(s * c, (s + 1) * c)
        b_ref[rows, :] = masked_sum(ci <= ri, rows)
        total = jnp.sum(la_ref[rows, :], axis=0, keepdims=True)
        b_min = total if b_min is None else jnp.minimum(b_min, total)
    fast = jnp.min(b_min) > GLA_FAST_MIN_B

    def chunk_pre(rows, scores_fn):
        b = b_ref[rows, :]
        b_last = b[c - 1:c, :]
        q = q_ref[rows, :].astype(F32)
        k = k_ref[rows, :].astype(F32)
        q_in = (q * jnp.exp(b)).astype(BF16)
        k_out = (k * jnp.exp(b_last - b)).astype(BF16)
        intra, upds = [], []
        for p in range(GLA_HEADS // 2):
            kc = slice(p * LANES, (p + 1) * LANES)
            v_p = v_ref[rows, 2 * p * GLA_DV:2 * (p + 1) * GLA_DV]
            for j, s_h in enumerate(scores_fn(p, q, k, b, q_in)):
                intra.append(_dot(s_h.astype(BF16), v_p[:, j * GLA_DV:(j + 1) * GLA_DV]))
            upd = _dot_tn(v_p, k_out[:, kc])
            upds.append(jnp.where(same_head, upd, 0.0))
        return rows, q_in, jnp.exp(b_last), intra, upds

    def chunk_post(pre, states):
        rows, q_in, a_chunk, intra, upds = pre
        new_states = []
        for p in range(GLA_HEADS // 2):
            kc = slice(p * LANES, (p + 1) * LANES)
            st = states[p]
            o_inter = _dot_nt(q_in[:, kc], st.astype(BF16))
            for j in range(2):
                h = 2 * p + j
                o_h = intra[h] + o_inter[:, j * GLA_DV:(j + 1) * GLA_DV]
                ms = jnp.mean(o_h * o_h, axis=-1, keepdims=True)
                y = o_h * lax.rsqrt(ms + EPS) * gn_ref[...]
                cols = slice(h * GLA_DV, (h + 1) * GLA_DV)
                o_ref[rows, cols] = (y * sg_ref[rows, cols].astype(F32)).astype(BF16)
            new_states.append(a_chunk[:, kc] * st + upds[p])
        return new_states

    def fast_scores(p, q, k, b, q_in):
        kc = slice(p * LANES, (p + 1) * LANES)
        k_div = k[:, kc] * jnp.exp(-b[:, kc])
        k_blk = jnp.concatenate([jnp.where(first_head, k_div, 0.0),
                                 jnp.where(first_head, 0.0, k_div)], axis=0).astype(BF16)
        s2 = _dot_nt(q_in[:, kc], k_blk)
        return [jnp.where(ri >= ci, s2[:, j * c:(j + 1) * c], 0.0) for j in range(2)]

    def safe_scores(rows):
        q = q_ref[rows, :].astype(F32)
        k = k_ref[rows, :].astype(F32)
        qk = q * k
        scores = []
        for h in range(GLA_HEADS):
            dk = slice(h * GLA_DK, (h + 1) * GLA_DK)
            scores.append(jnp.where(ri == ci, jnp.sum(qk[:, dk], axis=-1, keepdims=True), 0.0))
        half = c // 2
        while half >= 1:
            ref = (ri & (-2 * half)) + (half - 1)
            lo = jnp.minimum(ri, ref)
            hi = jnp.maximum(ri, ref)
            e = jnp.exp(masked_sum((ci > lo) & (ci <= hi), rows))
            q_l = (q * e).astype(BF16)
            k_l = (k * e).astype(BF16)
            pair = (((ri ^ ci) & (-2 * half)) == 0) & ((ri & half) != 0) & ((ci & half) == 0)
            for h in range(GLA_HEADS):
                dk = slice(h * GLA_DK, (h + 1) * GLA_DK)
                scores[h] = jnp.where(pair, _dot_nt(q_l[:, dk], k_l[:, dk]), scores[h])
            half //= 2
        return scores

    @pl.when(fast)
    def _():
        pres = [chunk_pre(slice(s * c, (s + 1) * c), fast_scores) for s in range(GLA_NSUB)]
        states = [st_ref[p] for p in range(GLA_HEADS // 2)]
        for pre in pres:
            states = chunk_post(pre, states)
        for p in range(GLA_HEADS // 2):
            st_ref[p] = states[p]

    @pl.when(jnp.logical_not(fast))
    def _():
        def body(s, carry):
            rows = pl.ds(pl.multiple_of(s * c, c), c)
            scores = safe_scores(rows)
            pre = chunk_pre(rows, lambda p, *_: scores[2 * p:2 * p + 2])
            states = chunk_post(pre, [st_ref[p] for p in range(GLA_HEADS // 2)])
            for p in range(GLA_HEADS // 2):
                st_ref[p] = states[p]
            return carry

        lax.fori_loop(0, GLA_NSUB, body, 0)


def _gla_call(q, k, la, v, sg, gn):
    s = q.shape[0]
    n = s // TM_GLA
    row = lambda i: (i, 0)
    const = lambda i: (0, 0)
    return pl.pallas_call(
        _gla_kernel,
        grid=(n,),
        in_specs=[
            pl.BlockSpec((TM_GLA, GLA_KEY), row),
            pl.BlockSpec((TM_GLA, GLA_KEY), row),
            pl.BlockSpec((TM_GLA, GLA_KEY), row),
            pl.BlockSpec((TM_GLA, GLA_VAL), row),
            pl.BlockSpec((TM_GLA, GLA_VAL), row),
            pl.BlockSpec((1, GLA_DV), const),
        ],
        out_specs=pl.BlockSpec((TM_GLA, GLA_VAL), row),
        out_shape=jax.ShapeDtypeStruct((s, GLA_VAL), BF16),
        scratch_shapes=[
            pltpu.VMEM((GLA_HEADS // 2, 2 * GLA_DV, LANES), F32),
            pltpu.VMEM((TM_GLA, GLA_KEY), F32),
        ],
        compiler_params=pltpu.CompilerParams(
            dimension_semantics=("arbitrary",), vmem_limit_bytes=VMEM_LIMIT),
        name="gla",
    )(q, k, la, v, sg, gn)


def _group_mean(t, low_half, size):
    lo = jnp.sum(jnp.where(low_half, t, 0.0), axis=-1, keepdims=True)
    hi = jnp.sum(jnp.where(low_half, 0.0, t), axis=-1, keepdims=True)
    return jnp.where(low_half, lo, hi) * (1.0 / size)


def _interleave(mxu_items, vec_items):
    done = 0
    for i, item in enumerate(mxu_items):
        item()
        upto = (i + 1) * len(vec_items) // len(mxu_items)
        for v_item in vec_items[done:upto]:
            v_item()
        done = upto


def _mlp_conv_kernel(x_ref, og_ref, un_ref, u0_ref, cw_ref, cb_ref, gg_ref, gb_ref,
                     wo_hbm, g2_ref, w1_hbm, w2_hbm, gf_ref,
                     o_ref, ext_ref, sh_ref, oc_ref, f_ref, wo_ref, w1_ref, w2_ref, w_sem):
    tm = TM_MLP

    def weight_copies():
        cut = D_FF // 8
        pairs = ((wo_hbm, wo_ref), (w1_hbm, w1_ref),
                 (w2_hbm.at[pl.ds(0, cut), :], w2_ref.at[pl.ds(0, cut), :]),
                 (w2_hbm.at[pl.ds(cut, D_FF - cut), :], w2_ref.at[pl.ds(cut, D_FF - cut), :]))
        return [pltpu.make_async_copy(src, dst, w_sem.at[j]) for j, (src, dst) in enumerate(pairs)]

    grp = CONV_CH // CONV_GROUPS
    low_half = lax.broadcasted_iota(jnp.int32, (CONV_RB, LANES), 1) < grp
    first = CONV_HALO - (CONV_W - 1)

    def shifted_copy(p):
        sh_ref[p - 1] = ext_ref[p:p + CONV_SH_ROWS, :]

    def conv_block(base):
        for cg in range(CONV_CH // LANES):
            cols = slice(cg * LANES, (cg + 1) * LANES)
            acc = jnp.broadcast_to(cb_ref[:, cols], (CONV_RB, LANES))
            for t in range(CONV_W):
                shift = first + t
                p = shift % SUBLANES
                rows = pl.ds(base + (shift - p), CONV_RB)
                tap = ext_ref[rows, cols] if p == 0 else sh_ref[p - 1, rows, cols]
                acc = acc + cw_ref[t, :, cols] * tap
            d = acc - _group_mean(acc, low_half, grp)
            var = _group_mean(d * d, low_half, grp)
            yn = d * lax.rsqrt(var + EPS) * gg_ref[:, cols] + gb_ref[:, cols]
            oc_ref[pl.ds(base, CONV_RB), cols] = (yn * _sigmoid(yn)).astype(BF16)

    @pl.when(pl.program_id(0) == 0)
    def _():
        for j, cp in enumerate(weight_copies()):
            cp.start(priority=(0, 1, 1, 0)[j])
        ext_ref[0:CONV_HALO, :] = jnp.zeros((CONV_HALO, CONV_CH), F32)
        ext_ref[CONV_HALO:, :] = u0_ref[...]
        for p in range(1, SUBLANES):
            shifted_copy(p)

        def body(r, carry):
            for k in range(CONV_PRO_BLOCKS):
                conv_block(pl.multiple_of((r * CONV_PRO_BLOCKS + k) * CONV_RB, CONV_RB))
            return carry

        lax.fori_loop(0, tm // (CONV_RB * CONV_PRO_BLOCKS), body, 0)
        for cp in weight_copies():
            cp.wait()

    h = x_ref[...] + _dot(og_ref[...], wo_ref[0:GLA_VAL, :]) + _dot(oc_ref[...], wo_ref[GLA_VAL:, :])
    hg = (h * g2_ref[...]).astype(BF16)
    r = lax.rsqrt(jnp.mean(h * h, axis=-1, keepdims=True) + EPS)
    r2 = r * r

    def roll_window():
        ext_ref[0:CONV_HALO, :] = ext_ref[tm:tm + CONV_HALO, :]
        ext_ref[CONV_HALO:, :] = un_ref[...]

    vec_items = [roll_window] + [functools.partial(shifted_copy, p) for p in range(1, SUBLANES)]
    vec_items += [functools.partial(conv_block, r * CONV_RB) for r in range(tm // CONV_RB)]

    def ff_in(j):
        cols = slice(j * FF_CHUNK, (j + 1) * FF_CHUNK)
        f = jnp.maximum(_dot(hg, w1_ref[:, cols]), 0.0)
        f_ref[:, cols] = (f * f * r2).astype(BF16)

    def ff_out(j):
        cols = slice(j * OUT_CHUNK, (j + 1) * OUT_CHUNK)
        o_ref[:, cols] = h[:, cols] + _dot(f_ref[...], w2_ref[:, cols])

    mxu_items = [functools.partial(ff_in, j) for j in range(D_FF // FF_CHUNK)]
    mxu_items += [functools.partial(ff_out, j) for j in range(D_MODEL // OUT_CHUNK)]
    _interleave(mxu_items, vec_items)

    acc = o_ref[...]
    ms2 = jnp.mean(acc * acc, axis=-1, keepdims=True)
    o_ref[...] = acc * lax.rsqrt(ms2 + EPS) * gf_ref[...]


def _mlp_conv_call(x2, og, u, cw, cb, gg, gb, wo, g2, w1, w2, gf):
    s = x2.shape[0]
    n = s // TM_MLP
    row = lambda i: (i, 0)
    nxt = lambda i: (jnp.minimum(i + 1, n - 1), 0)
    const = lambda i: (0, 0)
    single = pl.Buffered(1)
    return pl.pallas_call(
        _mlp_conv_kernel,
        grid=(n,),
        in_specs=[
            pl.BlockSpec((TM_MLP, D_MODEL), row),
            pl.BlockSpec((TM_MLP, GLA_VAL), row),
            pl.BlockSpec((TM_MLP, CONV_CH), nxt),
            pl.BlockSpec((TM_MLP, CONV_CH), const, pipeline_mode=single),
            pl.BlockSpec((CONV_W, 1, CONV_CH), lambda i: (0, 0, 0)),
            pl.BlockSpec((1, CONV_CH), const),
            pl.BlockSpec((1, CONV_CH), const),
            pl.BlockSpec((1, CONV_CH), const),
            pl.BlockSpec(memory_space=pl.ANY),
            pl.BlockSpec((1, D_MODEL), const),
            pl.BlockSpec(memory_space=pl.ANY),
            pl.BlockSpec(memory_space=pl.ANY),
            pl.BlockSpec((1, D_MODEL), const),
        ],
        out_specs=pl.BlockSpec((TM_MLP, D_MODEL), row),
        out_shape=jax.ShapeDtypeStruct((s, D_MODEL), F32),
        scratch_shapes=[
            pltpu.VMEM((TM_MLP + CONV_HALO, CONV_CH), F32),
            pltpu.VMEM((SUBLANES - 1, CONV_SH_ROWS, CONV_CH), F32),
            pltpu.VMEM((TM_MLP, CONV_CH), BF16),
            pltpu.VMEM((TM_MLP, D_FF), BF16),
            pltpu.VMEM((D_MODEL, D_MODEL), BF16),
            pltpu.VMEM((D_MODEL, D_FF), BF16),
            pltpu.VMEM((D_FF, D_MODEL), BF16),
            pltpu.SemaphoreType.DMA((4,)),
        ],
        compiler_params=pltpu.CompilerParams(
            dimension_semantics=("arbitrary",), vmem_limit_bytes=VMEM_LIMIT),
        name="mlp_conv",
    )(x2, og, u, u, cw, cb, gg, gb, wo, g2, w1, w2, gf)


def kernel(x, norm1_g, w_in, w_gate_up, b_gate, gla_norm_g, conv_w, conv_b, conv_norm_g,
           conv_norm_b, w_out, norm2_g, w_mlp_in, w_mlp_out, final_norm_g):
    bsz, seq, _ = x.shape
    x2 = x.reshape(bsz * seq, D_MODEL)
    assert bsz == 1, "state / halo carry across grid steps assumes one sequence"
    l = 0
    assert w_in.shape[0] == 1, "single layer: the projection weight is passed with its layer axis"
    w_in_t = jnp.swapaxes(w_in, 1, 2)
    q, k, la, v, sg, u, wo_b, w1_b, w2_b = _proj_call(
        x2, norm1_g[l][None, :], w_in_t, w_gate_up[l], b_gate[l][None, :], w_out[l], w_mlp_in[l],
        w_mlp_out[l])
    o_gla = _gla_call(q, k, la, v, sg, gla_norm_g[l][None, :])
    out = _mlp_conv_call(x2, o_gla, u, jnp.swapaxes(conv_w, 0, 1), conv_b[l][None, :], conv_norm_g[l][None, :],
                         conv_norm_b[l][None, :], wo_b, norm2_g[l][None, :], w1_b, w2_b,
                         final_norm_g[None, :])
    return out.reshape(bsz, seq, D_MODEL)
```
